```python
import jax, jax.numpy as jnp
from jax import lax
import numpy as np

D_MODEL = 1024
BATCH = 16
SEQ = 4096
DEPTH = 1
DEC_BATCH = 128
DEC_SEQ = 8
PAST_LEN = 8192
PAGE_SIZE = 128

HEAD_DIM = 64
NSA_HEADS = 8
NSA_KV_HEADS = 2
NSA_HPG = NSA_HEADS // NSA_KV_HEADS
FOX_HEADS = 8
FOX_KV_HEADS = 2
FOX_HPG = FOX_HEADS // FOX_KV_HEADS
CMP_BLOCK = 64
SLC_TOPK = 8
WINDOW = 512
Q_BLOCK = 128
D_FF = 2816
N_SUB = 3
RMS_EPS = 1e-6
NEG_INF = -1e30
FORCE_BONUS = 1e4
COLS_NSA_Q = NSA_HEADS * HEAD_DIM
COLS_NSA_KV = 3 * 2 * NSA_KV_HEADS * HEAD_DIM
COLS_NSA_GATE = 3 * NSA_HEADS
COLS_FOX_Q = FOX_HEADS * HEAD_DIM
COLS_FOX_KV = 2 * FOX_KV_HEADS * HEAD_DIM
COLS_FOX_F = FOX_HEADS
COLS_MERGE = 2 * D_MODEL
D_IN = COLS_NSA_Q + COLS_NSA_KV + COLS_NSA_GATE + COLS_FOX_Q + COLS_FOX_KV + COLS_FOX_F + COLS_MERGE

kernel_name = 'nsa_fox_macaron_adaln_step'


def rms_norm(x, g):
    xf = x.astype(jnp.float32)
    y = xf * lax.rsqrt(jnp.mean(xf * xf, axis=-1, keepdims=True) + RMS_EPS)
    return (y * g.astype(jnp.float32)).astype(x.dtype)


def swiglu(h, w_up, w_down):
    a, b = jnp.split(h @ w_up, 2, axis=-1)
    return (jax.nn.silu(a) * b) @ w_down


def split_points():
    sizes = (COLS_NSA_Q, COLS_NSA_KV, COLS_NSA_GATE, COLS_FOX_Q, COLS_FOX_KV, COLS_FOX_F)
    pts, acc = [], 0
    for s in sizes:
        acc += s
        pts.append(acc)
    return pts


def alibi_slopes():
    h = jnp.arange(1, NSA_HEADS + 1, dtype=jnp.float32)
    return jnp.exp2(-8.0 * h / NSA_HEADS).reshape(NSA_KV_HEADS, NSA_HPG)


def gather_pages(pool, layer, page_table):
    g = pool[layer, page_table]
    return g.reshape((g.shape[0], g.shape[1] * g.shape[2]) + g.shape[3:])


def project_tokens(h, w_in, b_fgt, g_q_nsa, g_k_nsa, g_q_fox, g_k_fox):
    B, L, _ = h.shape
    z = h @ w_in
    q_n, kv_n, gt_n, q_f, kv_f, f_f, g_m = jnp.split(z, split_points(), axis=-1)
    q_n = rms_norm(q_n.reshape(B, L, NSA_KV_HEADS, NSA_HPG, HEAD_DIM), g_q_nsa)
    kv_n = kv_n.reshape(B, L, 3, 2, NSA_KV_HEADS, HEAD_DIM)
    k_n = rms_norm(kv_n[:, :, :, 0], g_k_nsa[:, None, :])
    nsa_kv = jnp.stack([k_n, kv_n[:, :, :, 1]], axis=3)
    gates_n = jax.nn.sigmoid(gt_n).reshape(B, L, 3, NSA_KV_HEADS, NSA_HPG)
    q_f = rms_norm(q_f.reshape(B, L, FOX_KV_HEADS, FOX_HPG, HEAD_DIM), g_q_fox)
    kv_f = kv_f.reshape(B, L, 2, FOX_KV_HEADS, HEAD_DIM)
    fox_kv = jnp.stack([rms_norm(kv_f[:, :, 0], g_k_fox), kv_f[:, :, 1]], axis=2)
    logf = jax.nn.log_sigmoid((f_f + b_fgt).astype(jnp.float32))
    merge = jax.nn.sigmoid(g_m).reshape(B, L, 2, D_MODEL)
    return q_n, nsa_kv, gates_n, q_f, fox_kv, logf, merge


def nsa_block(q, gates, q_pos, cmp_kv, c_end, c_mid, slc, win, w_pos, slopes):
    f32 = jnp.float32
    scale = HEAD_DIM ** -0.5
    sl = slopes[None, :, :, None, None]
    s_c = jnp.einsum('bqghd,bngd->bghqn', q, cmp_kv[:, :, 0]).astype(f32) * scale
    s_c = s_c - sl * (q_pos[:, None] - c_mid[None, :])
    valid_c = c_end[None, :] <= q_pos[:, None]
    p_c = jnp.where(valid_c, jax.nn.softmax(jnp.where(valid_c, s_c, NEG_INF), axis=-1), 0.0)
    o_c = jnp.einsum('bghqn,bngd->bqghd', p_c.astype(q.dtype), cmp_kv[:, :, 1])
    nb, nbs = cmp_kv.shape[1], slc.shape[3]
    imp = jnp.pad(p_c.sum(axis=2), ((0, 0), (0, 0), (0, 0), (0, nbs - nb)))
    blk = jnp.arange(nbs)[None, :]
    cur = (q_pos // CMP_BLOCK)[:, None]
    forced = (blk == 0) | (blk == cur) | (blk == cur - 1)
    started = blk * CMP_BLOCK <= q_pos[:, None]
    score = jnp.where(started, imp + jnp.where(forced, FORCE_BONUS, 0.0), NEG_INF)
    _, idx = lax.top_k(score, min(SLC_TOPK, nbs))
    sel = jax.vmap(jax.vmap(lambda kv, ix: kv[:, ix]))(slc, idx)
    s_pos = idx[..., None] * CMP_BLOCK + jnp.arange(CMP_BLOCK)
    dist_s = (q_pos[:, None, None] - s_pos)[:, :, None]
    s_s = jnp.einsum('bqghd,bgqkld->bghqkl', q, sel[:, :, 0]).astype(f32) * scale
    s_s = jnp.where(dist_s >= 0, s_s - sl[..., None] * dist_s, NEG_INF)
    p_s = jax.nn.softmax(s_s, axis=(-2, -1))
    o_s = jnp.einsum('bghqkl,bgqkld->bqghd', p_s.astype(q.dtype), sel[:, :, 1])
    dist_w = q_pos[:, None] - w_pos[None, :]
    valid_w = (dist_w >= 0) & (dist_w < WINDOW) & (w_pos[None, :] >= 0)
    s_w = jnp.einsum('bqghd,bsgd->bghqs', q, win[:, :, 0]).astype(f32) * scale - sl * dist_w
    p_w = jax.nn.softmax(jnp.where(valid_w, s_w, NEG_INF), axis=-1)
    o_w = jnp.einsum('bghqs,bsgd->bqghd', p_w.astype(q.dtype), win[:, :, 1])
    o = (gates[:, :, 0, ..., None] * o_c + gates[:, :, 1, ..., None] * o_s
         + gates[:, :, 2, ..., None] * o_w)
    return o.reshape(o.shape[0], o.shape[1], NSA_HEADS * HEAD_DIM)


def fox_block(q, f_q, fox_rows, f_k, q_pos):
    k_pos = jnp.arange(fox_rows.shape[1])
    s = jnp.einsum('bqghd,bsgd->bghqs', q, fox_rows[:, :, 0]).astype(jnp.float32) * HEAD_DIM ** -0.5
    s = s + f_q[..., :, None] - f_k[..., None, :]
    causal = k_pos[None, :] <= q_pos[:, None]
    p = jax.nn.softmax(jnp.where(causal, s, NEG_INF), axis=-1)
    o = jnp.einsum('bghqs,bsgd->bqghd', p.astype(q.dtype), fox_rows[:, :, 1])
    return o.reshape(o.shape[0], o.shape[1], FOX_HEADS * HEAD_DIM)


def attend(q_n, gates_n, q_f, nsa_rows, win_rows, fox_rows, logf_rows, w_cmp):
    B, L = nsa_rows.shape[:2]
    n_q = q_n.shape[1]
    q_pos0 = L - n_q
    qb = Q_BLOCK if n_q % Q_BLOCK == 0 else n_q
    slopes = alibi_slopes()
    nb = L // CMP_BLOCK
    blocks = nsa_rows[:, :nb * CMP_BLOCK, :2].reshape(B, nb, CMP_BLOCK, 2, NSA_KV_HEADS, HEAD_DIM)
    cmp_kv = jnp.einsum('bnlcgd,cld->bncgd', blocks, w_cmp)
    c_end = (jnp.arange(nb) + 1) * CMP_BLOCK - 1
    c_mid = jnp.arange(nb, dtype=jnp.float32) * CMP_BLOCK + 0.5 * (CMP_BLOCK - 1)
    nbs = -(-L // CMP_BLOCK)
    slc = jnp.pad(nsa_rows[:, :, 2:], ((0, 0), (0, nbs * CMP_BLOCK - L), (0, 0), (0, 0), (0, 0)))
    slc = slc.reshape(B, nbs, CMP_BLOCK, 2, NSA_KV_HEADS, HEAD_DIM).transpose(0, 4, 3, 1, 2, 5)
    f_k = jnp.cumsum(logf_rows.astype(jnp.float32), axis=1)
    f_k = f_k.reshape(B, L, FOX_KV_HEADS, FOX_HPG).transpose(0, 2, 3, 1)

    def one_block(j0):
        q_pos = q_pos0 + j0 + jnp.arange(qb)
        w_pos = q_pos0 + j0 - WINDOW + jnp.arange(qb + WINDOW)
        o_n = nsa_block(lax.dynamic_slice_in_dim(q_n, j0, qb, axis=1),
                        lax.dynamic_slice_in_dim(gates_n, j0, qb, axis=1),
                        q_pos, cmp_kv, c_end, c_mid, slc,
                        lax.dynamic_slice_in_dim(win_rows, j0, qb + WINDOW, axis=1),
                        w_pos, slopes)
        o_f = fox_block(lax.dynamic_slice_in_dim(q_f, j0, qb, axis=1),
                        lax.dynamic_slice_in_dim(f_k, q_pos0 + j0, qb, axis=3),
                        fox_rows, f_k, q_pos)
        return o_n, o_f

    o_n, o_f = lax.map(one_block, jnp.arange(0, n_q, qb))
    o_n = o_n.transpose(1, 0, 2, 3).reshape(B, n_q, NSA_HEADS * HEAD_DIM)
    o_f = o_f.transpose(1, 0, 2, 3).reshape(B, n_q, FOX_HEADS * HEAD_DIM)
    return o_n, o_f


def token_mixer(h, past, w_in, b_fgt, g_q_nsa, g_k_nsa, g_q_fox, g_k_fox, w_cmp, w_br_nsa, w_br_fox, w_out):
    B, n_q = h.shape[:2]
    q_n, nsa_kv, gates_n, q_f, fox_kv, logf, merge = project_tokens(
        h, w_in, b_fgt, g_q_nsa, g_k_nsa, g_q_fox, g_k_fox)
    nsa_new = nsa_kv[:, :, :2].reshape(B, n_q, 4, NSA_KV_HEADS, HEAD_DIM)
    win_new = nsa_kv[:, :, 2]
    if past is None:
        nsa_rows, fox_rows, logf_rows = nsa_new, fox_kv, logf
        win_rows = jnp.pad(win_new, ((0, 0), (WINDOW, 0), (0, 0), (0, 0), (0, 0)))
        win_state = win_new[:, -min(WINDOW, n_q):]
    else:
        cache_nsa, cache_fox, cache_logf, win_buf, page_table, layer = past
        nsa_rows = jnp.concatenate([gather_pages(cache_nsa, layer, page_table), nsa_new], axis=1)
        fox_rows = jnp.concatenate([gather_pages(cache_fox, layer, page_table), fox_kv], axis=1)
        logf_rows = jnp.concatenate(
            [gather_pages(cache_logf, layer, page_table).astype(jnp.float32), logf], axis=1)
        wb = win_buf.shape[1]
        win_cat = jnp.concatenate([win_buf, win_new], axis=1)
        win_rows = jnp.pad(win_cat, ((0, 0), (WINDOW - wb, 0), (0, 0), (0, 0), (0, 0)))
        win_state = win_cat[:, -wb:]
    o_n, o_f = attend(q_n, gates_n, q_f, nsa_rows, win_rows, fox_rows, logf_rows, w_cmp)
    y = merge[:, :, 0] * (o_n @ w_br_nsa) + merge[:, :, 1] * (o_f @ w_br_fox)
    return y @ w_out, (nsa_new, fox_kv, logf, win_state)


def decoder_layer(x, c, past, w_ada, b_ada, g_norm, w_ffn_up, w_ffn_down, w_in, b_fgt,
                  g_q_nsa, g_k_nsa, g_q_fox, g_k_fox, w_cmp, w_br_nsa, w_br_fox, w_out):
    mod = (jax.nn.silu(c) @ w_ada + b_ada).reshape(c.shape[0], N_SUB, 3, 1, D_MODEL)

    def pre(v, i):
        return rms_norm(v, g_norm[i]) * (1.0 + mod[:, i, 1]) + mod[:, i, 0]

    x = x + 0.5 * mod[:, 0, 2] * swiglu(pre(x, 0), w_ffn_up[0], w_ffn_down[0])
    mix, state = token_mixer(pre(x, 1), past, w_in, b_fgt, g_q_nsa, g_k_nsa, g_q_fox, g_k_fox,
                             w_cmp, w_br_nsa, w_br_fox, w_out)
    x = x + mod[:, 1, 2] * mix
    x = x + 0.5 * mod[:, 2, 2] * swiglu(pre(x, 2), w_ffn_up[1], w_ffn_down[1])
    return x, state


def setup_inputs(seed: int = 0) -> dict:
    key = jax.random.key(seed)
    ks = jax.random.split(key, 24)
    f32 = jnp.float32
    n_pages = PAST_LEN // PAGE_SIZE
    n_pool = (DEC_BATCH * n_pages * 5) // 4
    win_buf = min(WINDOW, PAST_LEN)

    def nrm(k, shape, s=1.0):
        return jax.random.normal(k, shape, f32) * s

    page_table = jax.random.permutation(ks[0], n_pool)[:DEC_BATCH * n_pages]
    page_table = page_table.reshape(DEC_BATCH, n_pages).astype(jnp.int32)
    return {
        'x_prompt': nrm(ks[1], (BATCH, SEQ, D_MODEL)),
        'x_sample': nrm(ks[2], (DEC_BATCH, DEC_SEQ, D_MODEL)),
        'cache_nsa_kv': nrm(ks[3], (DEPTH, n_pool, PAGE_SIZE, 4, NSA_KV_HEADS, HEAD_DIM)),
        'cache_fox_kv': nrm(ks[4], (DEPTH, n_pool, PAGE_SIZE, 2, FOX_KV_HEADS, HEAD_DIM)),
        'cache_fox_logf': jax.nn.log_sigmoid(3.0 + nrm(ks[5], (DEPTH, n_pool, PAGE_SIZE, FOX_HEADS))),
        'state_win_kv': nrm(ks[6], (DEPTH, DEC_BATCH, win_buf, 2, NSA_KV_HEADS, HEAD_DIM)),
        'page_table': page_table,
        'c_prompt': nrm(ks[7], (BATCH, D_MODEL)),
        'c_sample': nrm(ks[8], (DEC_BATCH, D_MODEL)),
        'w_ada': nrm(ks[9], (DEPTH, D_MODEL, N_SUB * 3 * D_MODEL), D_MODEL ** -0.5),
        'b_ada': nrm(ks[10], (DEPTH, N_SUB * 3 * D_MODEL), 0.02),
        'g_norm': 1.0 + nrm(ks[11], (DEPTH, N_SUB, D_MODEL), 0.02),
        'w_ffn_up': nrm(ks[12], (DEPTH, 2, D_MODEL, 2 * D_FF), D_MODEL ** -0.5),
        'w_ffn_down': nrm(ks[13], (DEPTH, 2, D_FF, D_MODEL), D_FF ** -0.5),
        'w_in': nrm(ks[14], (DEPTH, D_MODEL, D_IN), D_MODEL ** -0.5),
        'b_fgt': 2.0 + 2.0 * jax.random.uniform(ks[15], (DEPTH, FOX_HEADS), f32),
        'g_q_nsa': 1.0 + nrm(ks[16], (DEPTH, HEAD_DIM), 0.02),
        'g_k_nsa': 1.0 + nrm(ks[17], (DEPTH, 3, HEAD_DIM), 0.02),
        'g_q_fox': 1.0 + nrm(ks[18], (DEPTH, HEAD_DIM), 0.02),
        'g_k_fox': 1.0 + nrm(ks[19], (DEPTH, HEAD_DIM), 0.02),
        'w_cmp': nrm(ks[20], (DEPTH, 2, CMP_BLOCK, HEAD_DIM), CMP_BLOCK ** -0.5),
        'w_br_nsa': nrm(ks[21], (DEPTH, NSA_HEADS * HEAD_DIM, D_MODEL), (NSA_HEADS * HEAD_DIM) ** -0.5),
        'w_br_fox': nrm(ks[22], (DEPTH, FOX_HEADS * HEAD_DIM, D_MODEL), (FOX_HEADS * HEAD_DIM) ** -0.5),
        'w_out': nrm(ks[23], (DEPTH, D_MODEL, D_MODEL), D_MODEL ** -0.5),
    }


def reference(x_prompt, x_sample, cache_nsa_kv, cache_fox_kv, cache_fox_logf, state_win_kv, page_table,
              c_prompt, c_sample, w_ada, b_ada, g_norm, w_ffn_up, w_ffn_down, w_in, b_fgt,
              g_q_nsa, g_k_nsa, g_q_fox, g_k_fox, w_cmp, w_br_nsa, w_br_fox, w_out):
    y_prompt, y_sample = x_prompt, x_sample
    st_p, st_s = [], []
    for l in range(DEPTH):
        wl = (w_ada[l], b_ada[l], g_norm[l], w_ffn_up[l], w_ffn_down[l], w_in[l], b_fgt[l],
              g_q_nsa[l], g_k_nsa[l], g_q_fox[l], g_k_fox[l], w_cmp[l], w_br_nsa[l], w_br_fox[l], w_out[l])
        y_prompt, sp = decoder_layer(y_prompt, c_prompt, None, *wl)
        past = (cache_nsa_kv, cache_fox_kv, cache_fox_logf, state_win_kv[l], page_table, l)
        y_sample, ss = decoder_layer(y_sample, c_sample, past, *wl)
        st_p.append(sp)
        st_s.append(ss)
    nsa_p, fox_p, logf_p, win_p = [jnp.stack([s[i] for s in st_p]) for i in range(4)]
    nsa_s, fox_s, logf_s, win_s = [jnp.stack([s[i] for s in st_s]) for i in range(4)]
    return (y_prompt, y_sample, nsa_p, nsa_s, fox_p, fox_s, logf_p, logf_s, win_p, win_s)
```

```python
import functools

import jax
import jax.numpy as jnp
from jax import lax
from jax.experimental import pallas as pl
from jax.experimental.pallas import tpu as pltpu

HEAD_DIM = 64
NSA_HEADS = 8
NSA_KV_HEADS = 2
FOX_HEADS = 8
FOX_KV_HEADS = 2
HPG = 4
CMP_BLOCK = 64
CMP_SHIFT = 6
SLC_TOPK = 8
WINDOW = 512
N_SUB = 3
RMS_EPS = 1e-6
NEG_INF = -1e30
FORCE_BONUS = 1e4
LANES = 128
PAIR_PERM = (0, 4, 1, 5, 2, 6, 3, 7)
SLOPES = tuple(2.0 ** -(h + 1) for h in range(NSA_HEADS))
VMEM_LIMIT = 56 * 1024 * 1024

F32 = jnp.float32
BF16 = jnp.bfloat16


def _nt(a, b):
    return lax.dot_general(a, b, (((1,), (1,)), ((), ())), preferred_element_type=F32)


def _nn(a, b):
    return jnp.dot(a, b, preferred_element_type=F32)


def _split3(x):
    hi = x.astype(BF16)
    r1 = x - hi.astype(F32)
    mid = r1.astype(BF16)
    lo = (r1 - mid.astype(F32)).astype(BF16)
    return hi, mid, lo


def _params(sem):
    return pltpu.CompilerParams(dimension_semantics=sem, vmem_limit_bytes=VMEM_LIMIT)


def _const_spec(shape):
    nd = len(shape)
    return pl.BlockSpec(shape, lambda *_: (0,) * nd, pipeline_mode=pl.Buffered(1))


def _mod_kernel(c_ref, w_ref, b_ref, o_ref):
    c = c_ref[...]
    a = (c * jax.nn.sigmoid(c)).astype(BF16)
    o_ref[...] = _nn(a, w_ref[...].astype(BF16)) + b_ref[...]


def _mod_call(c_all, w_ada, b_ada):
    bc, d = c_all.shape
    n = w_ada.shape[1]
    tn = 1024 if n % 1024 == 0 else n
    return pl.pallas_call(
        _mod_kernel,
        grid=(n // tn,),
        in_specs=[pl.BlockSpec((bc, d), lambda i: (0, 0)),
                  pl.BlockSpec((d, tn), lambda i: (0, i)),
                  pl.BlockSpec((1, tn), lambda i: (0, i))],
        out_specs=pl.BlockSpec((bc, tn), lambda i: (0, i)),
        out_shape=jax.ShapeDtypeStruct((bc, n), F32),
        compiler_params=_params(("parallel",)),
        name="adaln_mod",
    )(c_all, w_ada, b_ada.reshape(1, n))


def _pre(x, mod_ref, g_ref, sub):
    ms = jnp.mean(x * x, axis=-1, keepdims=True)
    y = x * lax.rsqrt(ms + RMS_EPS) * g_ref[sub:sub + 1, :]
    return y * (1.0 + mod_ref[:, 3 * sub + 1:3 * sub + 2, :]) + mod_ref[:, 3 * sub:3 * sub + 1, :]


def _ffn_kernel(x_ref, mod_ref, g_ref, wa_ref, wb_ref, wd_ref, o_ref, act_ref, *, sub, fc):
    x = x_ref[...]
    nb, nt, d = x.shape
    h = _pre(x, mod_ref, g_ref, sub).reshape(nb * nt, d).astype(BF16)
    dff = wa_ref.shape[1]
    for c in range(dff // fc):
        a = _nn(h, wa_ref[:, c * fc:(c + 1) * fc])
        b = _nn(h, wb_ref[:, c * fc:(c + 1) * fc])
        act_ref[:, c * fc:(c + 1) * fc] = (a * jax.nn.sigmoid(a) * b).astype(BF16)
    y = _nn(act_ref[...], wd_ref[...]).reshape(nb, nt, d)
    o_ref[...] = x + 0.5 * mod_ref[:, 3 * sub + 2:3 * sub + 3, :] * y


def _ffn_call(x3, mod, g_norm, wa, wb, wd, sub, blk):
    bs, lt, d = x3.shape
    nb, nt = blk
    dff = wa.shape[1]
    fc = 256 if dff % 256 == 0 else dff
    grid = (bs // nb, lt // nt)
    return pl.pallas_call(
        functools.partial(_ffn_kernel, sub=sub, fc=fc),
        grid=grid,
        in_specs=[pl.BlockSpec((nb, nt, d), lambda i, j: (i, j, 0)),
                  pl.BlockSpec((nb, 3 * N_SUB, d), lambda i, j: (i, 0, 0)),
                  _const_spec(g_norm.shape), _const_spec(wa.shape), _const_spec(wb.shape),
                  _const_spec(wd.shape)],
        out_specs=pl.BlockSpec((nb, nt, d), lambda i, j: (i, j, 0)),
        out_shape=jax.ShapeDtypeStruct(x3.shape, F32),
        scratch_shapes=[pltpu.VMEM((nb * nt, dff), BF16)],
        compiler_params=_params(("parallel", "parallel")),
        name="ffn_sub%d" % sub,
    )(x3, mod, g_norm, wa, wb, wd)


N_NORMED = 12


def _inproj_kernel(x_ref, mod_ref, g_ref, w_ref, gain_ref, bf_ref, seg_ref, *rest, with_cmp):
    if with_cmp:
        wc_ref, qn_ref, qf_ref, nsa_ref, win_ref, fox_ref, sm_ref, kv_ref, cmp_ref = rest
    else:
        qn_ref, qf_ref, nsa_ref, win_ref, fox_ref, sm_ref, kv_ref = rest
    x = x_ref[...]
    nb, nt, d = x.shape
    h = _pre(x, mod_ref, g_ref, 1).reshape(nb * nt, d).astype(BF16)
    z = _nn(h, w_ref[...])
    seg = seg_ref[...]
    normed = []
    for c in range(N_NORMED):
        zc = z[:, c * LANES:(c + 1) * LANES]
        zz = zc * zc
        hi = zz.astype(BF16)
        lo = (zz - hi.astype(F32)).astype(BF16)
        ss = _nn(hi, seg) + _nn(lo, seg)
        normed.append(zc * lax.rsqrt(ss * (1.0 / HEAD_DIM) + RMS_EPS) * gain_ref[:, c * LANES:(c + 1) * LANES])
    raw = [z[:, (N_NORMED + c) * LANES:(N_NORMED + c + 1) * LANES] for c in range(4)]
    k_cmp, k_slc, k_win, k_fox = normed[8:12]
    v_cmp, v_slc, v_win, v_fox = raw
    for j in range(4):
        qn_ref[:, j * LANES:(j + 1) * LANES] = normed[j].astype(BF16)
        qf_ref[:, j * LANES:(j + 1) * LANES] = normed[4 + j].astype(BF16)
    for c, t in enumerate((k_cmp, v_cmp, k_slc, v_slc)):
        nsa_ref[:, c * LANES:(c + 1) * LANES] = t
    win_ref[:, :LANES] = k_win
    win_ref[:, LANES:] = v_win
    fox_ref[:, :LANES] = k_fox
    fox_ref[:, LANES:] = v_fox
    for c, t in enumerate((k_slc, v_slc, k_win, v_win, k_fox, v_fox)):
        kv_ref[:, c * LANES:(c + 1) * LANES] = t.astype(BF16)
    zs = z[:, 16 * LANES:17 * LANES]
    lane = lax.broadcasted_iota(jnp.int32, zs.shape, 1)
    zf = zs + bf_ref[...]
    logf = jnp.minimum(zf, 0.0) - jnp.log1p(jnp.exp(-jnp.abs(zf)))
    sm_ref[...] = jnp.where(lane < FOX_HEADS, logf, jax.nn.sigmoid(zs))
    if with_cmp:
        rows = nb * nt
        kvc = jnp.concatenate([k_cmp, v_cmp], axis=1).reshape(rows // CMP_BLOCK, CMP_BLOCK, 2 * LANES)
        cmp_ref[...] = jnp.sum(kvc * wc_ref[...][None], axis=1)


def _inproj_call(x3, mod, g_norm, w1, gains, bf_pad, seg, wc, blk, with_cmp):
    bs, lt, d = x3.shape
    nb, nt = blk
    rows = nb * nt
    t = bs * lt
    grid = (bs // nb, lt // nt)
    nj = lt // nt

    def tok(cols):
        return pl.BlockSpec((rows, cols), lambda i, j: (i * nj + j, 0))

    in_specs = [pl.BlockSpec((nb, nt, d), lambda i, j: (i, j, 0)),
                pl.BlockSpec((nb, 3 * N_SUB, d), lambda i, j: (i, 0, 0)),
                _const_spec(g_norm.shape), _const_spec(w1.shape), _const_spec(gains.shape),
                _const_spec(bf_pad.shape), _const_spec(seg.shape)]
    args = [x3, mod, g_norm, w1, gains, bf_pad, seg]
    out_specs = [tok(512), tok(512), tok(512), tok(256), tok(256), tok(128), tok(768)]
    out_shape = [jax.ShapeDtypeStruct((t, 512), BF16), jax.ShapeDtypeStruct((t, 512), BF16),
                 jax.ShapeDtypeStruct((t, 512), F32), jax.ShapeDtypeStruct((t, 256), F32),
                 jax.ShapeDtypeStruct((t, 256), F32), jax.ShapeDtypeStruct((t, 128), F32),
                 jax.ShapeDtypeStruct((t, 768), BF16)]
    if with_cmp:
        in_specs.append(_const_spec(wc.shape))
        args.append(wc)
        out_specs.append(pl.BlockSpec((rows // CMP_BLOCK, 256), lambda i, j: (i * nj + j, 0)))
        out_shape.append(jax.ShapeDtypeStruct((t // CMP_BLOCK, 256), F32))
    return pl.pallas_call(
        functools.partial(_inproj_kernel, with_cmp=with_cmp),
        grid=grid, in_specs=in_specs, out_specs=out_specs, out_shape=out_shape,
        compiler_params=_params(("parallel", "parallel")),
        name="in_proj",
    )(*args)


def _cumsum_kernel(tok_ref, hm_ref, ftok_ref, fhm_ref, ctok_ref, chm_ref):
    @pl.when(pl.program_id(1) == 0)
    def _():
        ctok_ref[...] = jnp.zeros_like(ctok_ref)
        chm_ref[...] = jnp.zeros_like(chm_ref)

    tc = tok_ref.shape[0]
    r = lax.broadcasted_iota(jnp.int32, (tc, tc), 0)
    c = lax.broadcasted_iota(jnp.int32, (tc, tc), 1)
    lower = jnp.where(r >= c, 1.0, 0.0).astype(BF16)
    upper = jnp.where(r <= c, 1.0, 0.0).astype(BF16)
    cs = ctok_ref[...] + sum(_nn(lower, p) for p in _split3(tok_ref[...]))
    ftok_ref[...] = cs
    ctok_ref[...] = cs[tc - 1:tc, :]
    ch = chm_ref[...] + sum(_nn(p, upper) for p in _split3(hm_ref[...]))
    fhm_ref[...] = ch
    chm_ref[...] = jnp.broadcast_to(ch[:, tc - 1:tc], chm_ref.shape)


def _cumsum_call(small3, logf_hm, tc):
    b, l, _ = small3.shape
    nk = l // tc
    return pl.pallas_call(
        _cumsum_kernel,
        grid=(b, nk),
        in_specs=[pl.BlockSpec((None, tc, LANES), lambda i, j: (i, j, 0)),
                  pl.BlockSpec((None, FOX_HEADS, tc), lambda i, j: (i, 0, j))],
        out_specs=[pl.BlockSpec((None, tc, LANES), lambda i, j: (i, j, 0)),
                   pl.BlockSpec((None, None, FOX_HEADS, tc), lambda i, j: (i, j, 0, 0))],
        out_shape=[jax.ShapeDtypeStruct((b, l, LANES), F32),
                   jax.ShapeDtypeStruct((b, nk, FOX_HEADS, tc), F32)],
        scratch_shapes=[pltpu.VMEM((1, LANES), F32), pltpu.VMEM((FOX_HEADS, tc), F32)],
        compiler_params=_params(("parallel", "arbitrary")),
        name="fox_cumsum",
    )(small3, logf_hm)


def _flash_step(s, v, m, l, acc):
    m_new = jnp.maximum(m, jnp.max(s, axis=-1, keepdims=True))
    alpha = jnp.exp(m - m_new)
    p = jnp.exp(s - m_new)
    l = alpha * l + jnp.sum(p, axis=-1, keepdims=True)
    acc = alpha * acc + _nn(p.astype(BF16), v)
    return m_new, l, acc


def _flash_init(rows):
    return (jnp.full((rows, 1), NEG_INF, F32), jnp.zeros((rows, 1), F32), jnp.zeros((rows, LANES), F32))


def _group_masked(qp, g):
    lane = lax.broadcasted_iota(jnp.int32, qp.shape, 1)
    keep = (lane < HEAD_DIM) if g == 0 else (lane >= HEAD_DIM)
    return jnp.where(keep, qp, jnp.zeros_like(qp))


def _top_blocks(score, n_pick):
    idx = lax.broadcasted_iota(jnp.int32, score.shape, 1).astype(F32)
    sel = jnp.zeros(score.shape, F32)
    for _ in range(n_pick):
        mx = jnp.max(score, axis=-1, keepdims=True)
        first = jnp.min(jnp.where(score == mx, idx, 1e9), axis=-1, keepdims=True)
        pick = idx == first
        sel = jnp.where(pick, 1.0, sel)
        score = jnp.where(pick, -jnp.inf, score)
    return sel


def _nsa_prompt_kernel(q_ref, ks_ref, vs_ref, kw_ref, vw_ref, cmp_ref, sm_ref, o_ref, *, tq, nb):
    qi = pl.program_id(1)
    row_t = qi * tq + lax.broadcasted_iota(jnp.int32, (tq, 1), 0)
    tf = row_t.astype(F32)
    di = (lax.broadcasted_iota(jnp.int32, (tq, tq), 0) - lax.broadcasted_iota(jnp.int32, (tq, tq), 1))
    cmpv = cmp_ref[...]
    ckb = cmpv[:, :LANES].astype(BF16)
    cvb = cmpv[:, LANES:].astype(BF16)
    nidx = lax.broadcasted_iota(jnp.int32, (tq, nb), 1)
    cmid = nidx.astype(F32) * CMP_BLOCK + 0.5 * (CMP_BLOCK - 1)
    valid_c = (nidx + 1) * CMP_BLOCK - 1 <= row_t
    cur = lax.shift_right_logical(row_t, CMP_SHIFT)
    forced = (nidx == 0) | (nidx == cur) | (nidx == cur - 1)
    started = nidx * CMP_BLOCK <= row_t

    qm = [None] * NSA_HEADS
    o_c = [None] * NSA_HEADS
    sels = []
    for g in range(NSA_KV_HEADS):
        imp = jnp.zeros((tq, nb), F32)
        for j in range(HPG):
            h = g * HPG + j
            qm[h] = _group_masked(q_ref[:, j * LANES:(j + 1) * LANES], g)
            s = _nt(qm[h], ckb) - SLOPES[h] * (tf - cmid)
            s = jnp.where(valid_c, s, NEG_INF)
            e = jnp.exp(s - jnp.max(s, axis=-1, keepdims=True))
            p = jnp.where(valid_c, e / jnp.sum(e, axis=-1, keepdims=True), 0.0)
            imp = imp + p
            o_c[h] = _nn(p.astype(BF16), cvb)
        score = jnp.where(started, imp + jnp.where(forced, FORCE_BONUS, 0.0), NEG_INF)
        sels.append(_top_blocks(score, min(SLC_TOPK, nb)).astype(BF16))

    bpt = tq // CMP_BLOCK
    erow = lax.broadcasted_iota(jnp.int32, (nb, tq), 0)
    ecol = lax.shift_right_logical(lax.broadcasted_iota(jnp.int32, (nb, tq), 1), CMP_SHIFT)

    def slc_body(kj, carry):
        off = pl.multiple_of(kj * tq, tq)
        k = ks_ref[pl.ds(off, tq), :]
        v = vs_ref[pl.ds(off, tq), :]
        dist = di + (qi - kj) * tq
        distf = dist.astype(F32)
        expand = jnp.where(erow == ecol + kj * bpt, 1.0, 0.0).astype(BF16)
        out = []
        for g in range(NSA_KV_HEADS):
            ok = (_nn(sels[g], expand) > 0.5) & (dist >= 0)
            for j in range(HPG):
                h = g * HPG + j
                s = jnp.where(ok, _nt(qm[h], k) - SLOPES[h] * distf, NEG_INF)
                out.append(_flash_step(s, v, *carry[h]))
        return tuple(out)

    def win_body(kj, carry):
        off = pl.multiple_of(kj * tq, tq)
        k = kw_ref[pl.ds(off, tq), :]
        v = vw_ref[pl.ds(off, tq), :]
        dist = di + (qi - kj) * tq
        distf = dist.astype(F32)
        ok = (dist >= 0) & (dist < WINDOW)
        out = []
        for h in range(NSA_HEADS):
            s = jnp.where(ok, _nt(qm[h], k) - SLOPES[h] * distf, NEG_INF)
            out.append(_flash_step(s, v, *carry[h]))
        return tuple(out)

    init = tuple(_flash_init(tq) for _ in range(NSA_HEADS))
    st_s = lax.fori_loop(0, qi + 1, slc_body, init)
    st_w = lax.fori_loop(jnp.maximum(qi - (WINDOW + tq - 1) // tq, 0), qi + 1, win_body, init)

    sm = sm_ref[...]
    lane = lax.broadcasted_iota(jnp.int32, (tq, LANES), 1)
    for j in range(HPG):
        halves = []
        for g in range(NSA_KV_HEADS):
            h = g * HPG + j
            c0 = FOX_HEADS + h
            g_c = sm[:, c0:c0 + 1]
            g_s = sm[:, c0 + NSA_HEADS:c0 + NSA_HEADS + 1]
            g_w = sm[:, c0 + 2 * NSA_HEADS:c0 + 2 * NSA_HEADS + 1]
            halves.append(g_c * o_c[h] + g_s * (st_s[h][2] / st_s[h][1]) + g_w * (st_w[h][2] / st_w[h][1]))
        o_ref[:, j * LANES:(j + 1) * LANES] = jnp.where(lane < HEAD_DIM, halves[0], halves[1]).astype(BF16)


def _nsa_prompt_call(qn3, kv3, cmp3, small3, tq):
    b, l, _ = qn3.shape
    nb = cmp3.shape[1]

    def kvspec(c):
        return pl.BlockSpec((None, l, LANES), lambda i, j: (i, 0, c))

    return pl.pallas_call(
        functools.partial(_nsa_prompt_kernel, tq=tq, nb=nb),
        grid=(b, l // tq),
        in_specs=[pl.BlockSpec((None, tq, 512), lambda i, j: (i, j, 0)),
                  kvspec(0), kvspec(1), kvspec(2), kvspec(3),
                  pl.BlockSpec((None, nb, 256), lambda i, j: (i, 0, 0)),
                  pl.BlockSpec((None, tq, LANES), lambda i, j: (i, j, 0))],
        out_specs=pl.BlockSpec((None, tq, 512), lambda i, j: (i, j, 0)),
        out_shape=jax.ShapeDtypeStruct((b, l, 512), BF16),
        compiler_params=_params(("parallel", "arbitrary")),
        name="nsa_prompt",
    )(qn3, kv3, kv3, kv3, kv3, cmp3, small3)


def _fox_prompt_kernel(q_ref, k_ref, v_ref, ftok_ref, fhm_ref, o_ref, *, tq):
    qi = pl.program_id(1)
    di = (lax.broadcasted_iota(jnp.int32, (tq, tq), 0) - lax.broadcasted_iota(jnp.int32, (tq, tq), 1))
    ftok = ftok_ref[...]
    qm, ft = [], []
    for g in range(FOX_KV_HEADS):
        for j in range(HPG):
            h = g * HPG + j
            qm.append(_group_masked(q_ref[:, j * LANES:(j + 1) * LANES], g))
            ft.append(ftok[:, h:h + 1])

    def body(kj, carry):
        off = pl.multiple_of(kj * tq, tq)
        k = k_ref[pl.ds(off, tq), :]
        v = v_ref[pl.ds(off, tq), :]
        fs = fhm_ref[kj]
        ok = di + (qi - kj) * tq >= 0
        out = []
        for h in range(FOX_HEADS):
            s = _nt(qm[h], k) + (ft[h] - fs[h:h + 1, :])
            out.append(_flash_step(jnp.where(ok, s, NEG_INF), v, *carry[h]))
        return tuple(out)

    st = lax.fori_loop(0, qi + 1, body, tuple(_flash_init(tq) for _ in range(FOX_HEADS)))
    lane = lax.broadcasted_iota(jnp.int32, (tq, LANES), 1)
    for j in range(HPG):
        lo = st[j][2] / st[j][1]
        hi = st[HPG + j][2] / st[HPG + j][1]
        o_ref[:, j * LANES:(j + 1) * LANES] = jnp.where(lane < HEAD_DIM, lo, hi).astype(BF16)


def _fox_prompt_call(qf3, kv3, ftok3, fhm4, tq):
    b, l, _ = qf3.shape
    nk = l // tq
    return pl.pallas_call(
        functools.partial(_fox_prompt_kernel, tq=tq),
        grid=(b, nk),
        in_specs=[pl.BlockSpec((None, tq, 512), lambda i, j: (i, j, 0)),
                  pl.BlockSpec((None, l, LANES), lambda i, j: (i, 0, 4)),
                  pl.BlockSpec((None, l, LANES), lambda i, j: (i, 0, 5)),
                  pl.BlockSpec((None, tq, LANES), lambda i, j: (i, j, 0)),
                  pl.BlockSpec((None, nk, FOX_HEADS, tq), lambda i, j: (i, 0, 0, 0))],
        out_specs=pl.BlockSpec((None, tq, 512), lambda i, j: (i, j, 0)),
        out_shape=jax.ShapeDtypeStruct((b, l, 512), BF16),
        compiler_params=_params(("parallel", "arbitrary")),
        name="fox_prompt",
    )(qf3, kv3, kv3, ftok3, fhm4)


def _lane_cumsum(x):
    n = x.shape[1]
    r = lax.broadcasted_iota(jnp.int32, (n, n), 0)
    c = lax.broadcasted_iota(jnp.int32, (n, n), 1)
    upper = jnp.where(r <= c, 1.0, 0.0).astype(BF16)
    return sum(_nn(p, upper) for p in _split3(x))


def _dec_cmp_kernel(pt_ref, q_ref, rowc_ref, wc_ref, *rest, pp, nb, past_len):
    del pt_ref
    pages = rest[:pp]
    lfs = rest[pp:2 * pp]
    oc_ref, sel_ref, ftot_ref, cmp_scr, ftot_scr = rest[2 * pp:]
    c = pl.program_id(1)

    @pl.when(c == 0)
    def _():
        ftot_scr[...] = jnp.zeros_like(ftot_scr)

    big = jnp.concatenate([p[...] for p in pages], axis=0)
    rows = big.shape[0]
    nblk = rows // CMP_BLOCK
    blk = jnp.sum(big.reshape(nblk, CMP_BLOCK, 2 * LANES) * wc_ref[...][None], axis=1)
    cmp_scr[pl.ds(pl.multiple_of(c * nblk, nblk), nblk), :] = blk
    tot = lfs[0][...]
    for p in lfs[1:]:
        tot = tot + p[...]
    ftot_scr[...] += jnp.broadcast_to(jnp.sum(tot, axis=-1, keepdims=True), ftot_scr.shape)

    @pl.when(c == pl.num_programs(1) - 1)
    def _():
        ftot_ref[...] = ftot_scr[...]
        cmpv = cmp_scr[...]
        ckb = cmpv[:, :LANES].astype(BF16)
        cvb = cmpv[:, LANES:].astype(BF16)
        nrow = q_ref.shape[0]
        rowc = rowc_ref[...]
        slope = rowc[:, 0:1]
        t = past_len + rowc[:, 1:2]
        nidx = lax.broadcasted_iota(jnp.int32, (nrow, nb), 1)
        cmid = nidx.astype(F32) * CMP_BLOCK + 0.5 * (CMP_BLOCK - 1)
        valid = cmid + 0.5 * (CMP_BLOCK - 1) <= t
        s = jnp.where(valid, _nt(q_ref[...], ckb) - slope * (t - cmid), NEG_INF)
        e = jnp.exp(s - jnp.max(s, axis=-1, keepdims=True))
        p = jnp.where(valid, e / jnp.sum(e, axis=-1, keepdims=True), 0.0)
        oc_ref[...] = _nn(p.astype(BF16), cvb)
        nq = nrow // NSA_HEADS
        n8 = lax.broadcasted_iota(jnp.int32, (nq, nb), 1)
        forced = (n8 == 0) | (n8 == nb - 1)
        sel_rows = []
        for g in range(NSA_KV_HEADS):
            imp = jnp.zeros((nq, nb), F32)
            for j in range(HPG):
                r0 = (g * HPG + j) * nq
                imp = imp + p[r0:r0 + nq, :]
            sel = _top_blocks(imp + jnp.where(forced, FORCE_BONUS, 0.0), SLC_TOPK - 1)
            sel_rows += [sel] * HPG
        sel_ref[...] = jnp.concatenate(sel_rows, axis=0).astype(BF16)


def _page_specs(pp, n_pages, block, col):
    def one(i):
        return pl.BlockSpec(block, lambda b, c, pt: (pt[b * n_pages + c * pp + i], 0, col))
    return [one(i) for i in range(pp)]


def _dec_cmp_call(pt_flat, q_dec, rowc, wc, nsa_pool, lf_pool, pp, n_pages, past_len):
    b, nrow, _ = q_dec.shape
    nb = past_len // CMP_BLOCK
    page = nsa_pool.shape[1]
    nc = n_pages // pp
    per = lambda shape: pl.BlockSpec((None,) + shape, lambda i, c, pt: (i, 0, 0))
    cst = lambda shape: pl.BlockSpec(shape, lambda i, c, pt: (0, 0))
    grid_spec = pltpu.PrefetchScalarGridSpec(
        num_scalar_prefetch=1,
        grid=(b, nc),
        in_specs=[per((nrow, LANES)), cst(rowc.shape), cst(wc.shape)]
        + _page_specs(pp, n_pages, (None, page, 2 * LANES), 0)
        + _page_specs(pp, n_pages, (None, FOX_HEADS, page), 0),
        out_specs=[per((nrow, LANES)), per((nrow, nb)), per((FOX_HEADS, LANES))],
        scratch_shapes=[pltpu.VMEM((nb, 2 * LANES), F32), pltpu.VMEM((FOX_HEADS, LANES), F32)],
    )
    return pl.pallas_call(
        functools.partial(_dec_cmp_kernel, pp=pp, nb=nb, past_len=past_len),
        grid_spec=grid_spec,
        out_shape=[jax.ShapeDtypeStruct((b, nrow, LANES), F32),
                   jax.ShapeDtypeStruct((b, nrow, nb), BF16),
                   jax.ShapeDtypeStruct((b, FOX_HEADS, LANES), F32)],
        compiler_params=_params(("parallel", "arbitrary")),
        name="decode_cmp",
    )(pt_flat, q_dec, rowc, wc, *([nsa_pool] * pp), *([lf_pool] * pp))


def _rep_rows(x, n):
    return jnp.concatenate([jnp.broadcast_to(x[h:h + 1, :], (n, x.shape[1])) for h in range(x.shape[0])], axis=0)


def _scr_flash(s, v, m_ref, l_ref, acc_ref):
    m, l, acc = _flash_step(s, v, m_ref[:, 0:1], l_ref[:, 0:1], acc_ref[...])
    m_ref[...] = jnp.broadcast_to(m, m_ref.shape)
    l_ref[...] = jnp.broadcast_to(l, l_ref.shape)
    acc_ref[...] = acc


def _dec_main_kernel(pt_ref, qn_ref, qf_ref, sel_ref, rowc_ref, oc_ref, gate_ref, ftot_ref,
                     newn_ref, newf_ref, newlf_ref, wbuf_ref, wnew_ref, *rest, pp, nb, past_len, nq):
    del pt_ref
    slc = rest[:pp]
    fox = rest[pp:2 * pp]
    lfs = rest[2 * pp:3 * pp]
    (on_ref, of_ref, wst_ref, ms_ref, ls_ref, as_ref, mf_ref, lf_ref, af_ref, fcar_ref, ftc_ref) = rest[3 * pp:]
    c = pl.program_id(1)
    nrow = qn_ref.shape[0]
    page = slc[0].shape[0]
    kc = pp * page
    rowc = rowc_ref[...]
    slope = rowc[:, 0:1]
    t = past_len + rowc[:, 1:2]
    qn = qn_ref[...]
    qf = qf_ref[...]

    @pl.when(c == 0)
    def _():
        for r in (ms_ref, mf_ref):
            r[...] = jnp.full(r.shape, NEG_INF, F32)
        for r in (ls_ref, as_ref, lf_ref, af_ref, fcar_ref):
            r[...] = jnp.zeros_like(r)
        f_new = ftot_ref[...] + _lane_cumsum(newlf_ref[...])
        rep = _rep_rows(f_new, nq)
        lane = lax.broadcasted_iota(jnp.int32, rep.shape, 1)
        tok = lax.broadcasted_iota(jnp.int32, rep.shape, 0) & (nq - 1)
        ftc_ref[...] = jnp.broadcast_to(
            jnp.sum(jnp.where(lane == tok, rep, 0.0), axis=-1, keepdims=True), ftc_ref.shape)

    ft = ftc_ref[:, 0:1]

    ks = jnp.concatenate([p[:, :LANES].astype(BF16) for p in slc], axis=0)
    vs = jnp.concatenate([p[:, LANES:].astype(BF16) for p in slc], axis=0)
    pos = (c * kc + lax.broadcasted_iota(jnp.int32, (1, kc), 1)).astype(F32)
    erow = lax.broadcasted_iota(jnp.int32, (nb, kc), 0)
    ecol = lax.shift_right_logical(lax.broadcasted_iota(jnp.int32, (nb, kc), 1), CMP_SHIFT) + c * (kc // CMP_BLOCK)
    expand = jnp.where(erow == ecol, 1.0, 0.0).astype(BF16)
    ok = _nn(sel_ref[...], expand) > 0.5
    s = jnp.where(ok, _nt(qn, ks) - slope * (t - pos), NEG_INF)
    _scr_flash(s, vs, ms_ref, ls_ref, as_ref)

    kf = jnp.concatenate([p[:, :LANES].astype(BF16) for p in fox], axis=0)
    vf = jnp.concatenate([p[:, LANES:].astype(BF16) for p in fox], axis=0)
    car = fcar_ref[:, 0:1]
    fs_parts = []
    for p in lfs:
        cs = car + _lane_cumsum(p[...])
        fs_parts.append(cs)
        car = cs[:, page - 1:page]
    fcar_ref[...] = jnp.broadcast_to(car, fcar_ref.shape)
    fs = _rep_rows(jnp.concatenate(fs_parts, axis=1), nq)
    _scr_flash(_nt(qf, kf) + (ft - fs), vf, mf_ref, lf_ref, af_ref)

    @pl.when(c == pl.num_programs(1) - 1)
    def _():
        npad = newn_ref.shape[0]
        lane_i = lax.broadcasted_iota(jnp.int32, (1, npad), 1)
        posn = (past_len + lane_i).astype(F32)
        okn = (lane_i < nq) & (posn <= t)
        newn = newn_ref[...]
        sn = jnp.where(okn, _nt(qn, newn[:, :LANES].astype(BF16)) - slope * (t - posn), NEG_INF)
        m, l, acc = _flash_step(sn, newn[:, LANES:].astype(BF16), ms_ref[:, 0:1], ls_ref[:, 0:1], as_ref[...])
        o_s = acc / l
        newf = newf_ref[...]
        fsn = _rep_rows(fcar_ref[:, 0:1] + _lane_cumsum(newlf_ref[...]), nq)
        sf = jnp.where(okn, _nt(qf, newf[:, :LANES].astype(BF16)) + (ft - fsn), NEG_INF)
        m, l, acc = _flash_step(sf, newf[:, LANES:].astype(BF16), mf_ref[:, 0:1], lf_ref[:, 0:1], af_ref[...])
        of_ref[...] = acc / l
        wbuf = wbuf_ref[...]
        wnew = wnew_ref[...]
        nbuf = wbuf.shape[0]
        kw = jnp.concatenate([wbuf[:, :LANES], wnew[:, :LANES]], axis=0).astype(BF16)
        vw = jnp.concatenate([wbuf[:, LANES:], wnew[:, LANES:]], axis=0).astype(BF16)
        wl = lax.broadcasted_iota(jnp.int32, (1, nbuf + npad), 1)
        posw = jnp.where(wl < nbuf, past_len - nbuf + wl, past_len + wl - nbuf).astype(F32)
        dw = t - posw
        okw = (dw >= 0) & (dw < WINDOW) & ((wl < nbuf) | (wl < nbuf + nq))
        sw = jnp.where(okw, _nt(qn, kw) - slope * dw, NEG_INF)
        ew = jnp.exp(sw - jnp.max(sw, axis=-1, keepdims=True))
        o_w = _nn(ew.astype(BF16), vw) / jnp.sum(ew, axis=-1, keepdims=True)
        gate = gate_ref[...]
        on_ref[...] = gate[:, 0:1] * oc_ref[...] + gate[:, 1:2] * o_s + gate[:, 2:3] * o_w
        wst_ref[0:nbuf - nq, :] = wbuf[nq:, :]
        wst_ref[nbuf - nq:, :] = wnew[0:nq, :]


def _dec_main_call(pt_flat, qn_dec, qf_dec, sel, rowc, o_c, gates, ftot, newn, newf, newlf, wbuf, wnew,
                   nsa_pool, fox_pool, lf_pool, pp, n_pages, past_len, nq):
    b, nrow, _ = qn_dec.shape
    nb = past_len // CMP_BLOCK
    page = nsa_pool.shape[1]
    nc = n_pages // pp
    nbuf = wbuf.shape[1]
    per = lambda shape: pl.BlockSpec((None,) + shape, lambda i, c, pt: (i, 0, 0))
    cst = lambda shape: pl.BlockSpec(shape, lambda i, c, pt: (0, 0))
    rl = (nrow, LANES)
    grid_spec = pltpu.PrefetchScalarGridSpec(
        num_scalar_prefetch=1,
        grid=(b, nc),
        in_specs=[per(rl), per(rl), per((nrow, nb)), cst(rowc.shape), per(rl), per(rl), per((FOX_HEADS, LANES)),
                  per(newn.shape[1:]), per(newf.shape[1:]), per(newlf.shape[1:]),
                  per((nbuf, 2 * LANES)), per(wnew.shape[1:])]
        + _page_specs(pp, n_pages, (None, page, 2 * LANES), 1)
        + _page_specs(pp, n_pages, (None, page, 2 * LANES), 0)
        + _page_specs(pp, n_pages, (None, FOX_HEADS, page), 0),
        out_specs=[per(rl), per(rl), per((nbuf, 2 * LANES))],
        scratch_shapes=[pltpu.VMEM(rl, F32)] * 6 + [pltpu.VMEM((FOX_HEADS, LANES), F32), pltpu.VMEM(rl, F32)],
    )
    return pl.pallas_call(
        functools.partial(_dec_main_kernel, pp=pp, nb=nb, past_len=past_len, nq=nq),
        grid_spec=grid_spec,
        out_shape=[jax.ShapeDtypeStruct((b,) + rl, F32), jax.ShapeDtypeStruct((b,) + rl, F32),
                   jax.ShapeDtypeStruct((b, nbuf, 2 * LANES), F32)],
        compiler_params=_params(("parallel", "arbitrary")),
        name="decode_main",
    )(pt_flat, qn_dec, qf_dec, sel, rowc, o_c, gates, ftot, newn, newf, newlf, wbuf, wnew,
      *([nsa_pool] * pp), *([fox_pool] * pp), *([lf_pool] * pp))


def _outproj_kernel(x_ref, mod_ref, g_ref, on_ref, of_ref, wm_ref, wa_ref, wb_ref, wo_ref, o_ref):
    x = x_ref[...]
    nb, nt, d = x.shape
    h = _pre(x, mod_ref, g_ref, 1).reshape(nb * nt, d).astype(BF16)
    gm = jax.nn.sigmoid(_nn(h, wm_ref[...]))
    y = gm[:, :d] * _nn(on_ref[...], wa_ref[...]) + gm[:, d:] * _nn(of_ref[...], wb_ref[...])
    mix = _nn(y.astype(BF16), wo_ref[...]).reshape(nb, nt, d)
    o_ref[...] = x + mod_ref[:, 5:6, :] * mix


def _outproj_call(x3, mod, g_norm, o_n, o_f, wm, wa, wb, wo, blk):
    bs, lt, d = x3.shape
    nb, nt = blk
    rows = nb * nt
    nj = lt // nt
    tok = pl.BlockSpec((rows, o_n.shape[1]), lambda i, j: (i * nj + j, 0))
    return pl.pallas_call(
        _outproj_kernel,
        grid=(bs // nb, nj),
        in_specs=[pl.BlockSpec((nb, nt, d), lambda i, j: (i, j, 0)),
                  pl.BlockSpec((nb, 3 * N_SUB, d), lambda i, j: (i, 0, 0)),
                  _const_spec(g_norm.shape), tok, tok,
                  _const_spec(wm.shape), _const_spec(wa.shape), _const_spec(wb.shape), _const_spec(wo.shape)],
        out_specs=pl.BlockSpec((nb, nt, d), lambda i, j: (i, j, 0)),
        out_shape=jax.ShapeDtypeStruct(x3.shape, F32),
        compiler_params=_params(("parallel", "parallel")),
        name="out_proj",
    )(x3, mod, g_norm, o_n, o_f, wm, wa, wb, wo)


def _pair_major_cols(w):
    d = w.shape[0]
    return w.reshape(d, NSA_HEADS, HEAD_DIM)[:, jnp.array(PAIR_PERM), :].reshape(d, NSA_HEADS * HEAD_DIM)


def _pair_major_rows(w):
    d = w.shape[1]
    return w.reshape(NSA_HEADS, HEAD_DIM, d)[jnp.array(PAIR_PERM)].reshape(NSA_HEADS * HEAD_DIM, d)


def _prep_weights(w_in, b_fgt, g_q_nsa, g_k_nsa, g_q_fox, g_k_fox, w_cmp):
    d = w_in.shape[0]
    hq = NSA_HEADS * HEAD_DIM
    o = 0
    w_qn = w_in[:, o:o + hq]; o += hq
    w_kvn = w_in[:, o:o + 3 * 2 * LANES]; o += 3 * 2 * LANES
    w_gt = w_in[:, o:o + 3 * NSA_HEADS]; o += 3 * NSA_HEADS
    w_qf = w_in[:, o:o + hq]; o += hq
    w_kvf = w_in[:, o:o + 2 * LANES]; o += 2 * LANES
    w_ff = w_in[:, o:o + FOX_HEADS]; o += FOX_HEADS
    w_merge = w_in[:, o:]
    k_n = [w_kvn[:, br * 256:br * 256 + LANES] for br in range(3)]
    v_n = [w_kvn[:, br * 256 + LANES:(br + 1) * 256] for br in range(3)]
    small = jnp.concatenate([w_ff, w_gt, jnp.zeros((d, LANES - FOX_HEADS - 3 * NSA_HEADS), w_in.dtype)], axis=1)
    w1 = jnp.concatenate([_pair_major_cols(w_qn), _pair_major_cols(w_qf), k_n[0], k_n[1], k_n[2], w_kvf[:, :LANES],
                          v_n[0], v_n[1], v_n[2], w_kvf[:, LANES:], small], axis=1).astype(BF16)
    scale = HEAD_DIM ** -0.5
    gains = jnp.concatenate([jnp.tile(g_q_nsa * scale, NSA_HEADS), jnp.tile(g_q_fox * scale, FOX_HEADS),
                             jnp.tile(g_k_nsa[0], 2), jnp.tile(g_k_nsa[1], 2), jnp.tile(g_k_nsa[2], 2),
                             jnp.tile(g_k_fox, 2)]).reshape(1, N_NORMED * LANES)
    bf_pad = jnp.concatenate([b_fgt, jnp.zeros((LANES - FOX_HEADS,), F32)]).reshape(1, LANES)
    lane = jnp.arange(LANES)
    seg = (lane[:, None] // HEAD_DIM == lane[None, :] // HEAD_DIM).astype(BF16)
    wc = jnp.concatenate([w_cmp[0], w_cmp[0], w_cmp[1], w_cmp[1]], axis=1)
    return w1, gains, bf_pad, seg, wc, w_merge.astype(BF16)


def _to_decode_rows(q, b, nq):
    q5 = q.reshape(b, nq, HPG, 2, HEAD_DIM)
    eye = jnp.eye(2, dtype=q.dtype)
    return jnp.einsum('btjgd,gk->bgjtkd', q5, eye).reshape(b, NSA_HEADS * nq, LANES)


def _from_decode_rows(o, b, nq):
    o6 = o.reshape(b, 2, HPG, nq, 2, HEAD_DIM)
    own = jnp.stack([o6[:, 0, :, :, 0], o6[:, 1, :, :, 1]], axis=1)
    return own.transpose(0, 3, 1, 2, 4).reshape(b * nq, NSA_HEADS * HEAD_DIM)


def _pad_rows(x, n):
    return jnp.pad(x, ((0, 0), (0, n - x.shape[1]), (0, 0)))


def kernel(x_prompt, x_sample, cache_nsa_kv, cache_fox_kv, cache_fox_logf, state_win_kv, page_table,
           c_prompt, c_sample, w_ada, b_ada, g_norm, w_ffn_up, w_ffn_down, w_in, b_fgt,
           g_q_nsa, g_k_nsa, g_q_fox, g_k_fox, w_cmp, w_br_nsa, w_br_fox, w_out):
    assert w_ada.shape[0] == 1, "single-layer trunk"
    bp, lp, d = x_prompt.shape
    bs, nq, _ = x_sample.shape
    n_pool, page = cache_nsa_kv.shape[1:3]
    n_pages = page_table.shape[1]
    past_len = n_pages * page
    nbuf = state_win_kv.shape[2]
    dff = w_ffn_down.shape[2]
    assert nq % 8 == 0 and nq & (nq - 1) == 0 and nq <= CMP_BLOCK and past_len // CMP_BLOCK >= SLC_TOPK
    assert nbuf == WINDOW and lp % 512 == 0

    mod = _mod_call(jnp.concatenate([c_prompt, c_sample], axis=0), w_ada[0], b_ada[0]).reshape(bp + bs, 3 * N_SUB, d)
    mod_p, mod_s = mod[:bp], mod[bp:]
    gn = g_norm[0]
    up = w_ffn_up[0].astype(BF16)
    ffn_w = [(up[i, :, :dff], up[i, :, dff:], w_ffn_down[0, i].astype(BF16)) for i in range(2)]
    w1, gains, bf_pad, seg, wc, w_merge = _prep_weights(
        w_in[0], b_fgt[0], g_q_nsa[0], g_k_nsa[0], g_q_fox[0], g_k_fox[0], w_cmp[0])
    wo = w_out[0].astype(BF16)
    wa_nat, wb_nat = w_br_nsa[0].astype(BF16), w_br_fox[0].astype(BF16)
    wa_pm, wb_pm = _pair_major_rows(wa_nat), _pair_major_rows(wb_nat)

    pblk = (1, 512)
    tq = 256
    x1 = _ffn_call(x_prompt, mod_p, gn, *ffn_w[0], 0, pblk)
    qn, qf, nsa_new, win_new, fox_new, small, kvb, cmpb = _inproj_call(
        x1, mod_p, gn, w1, gains, bf_pad, seg, wc, pblk, True)
    small3 = small.reshape(bp, lp, LANES)
    logf_p = small3[:, :, :FOX_HEADS]
    ftok, fhm = _cumsum_call(small3, logf_p.transpose(0, 2, 1), tq)
    kv3 = kvb.reshape(bp, lp, 768)
    o_n = _nsa_prompt_call(qn.reshape(bp, lp, 512), kv3, cmpb.reshape(bp, lp // CMP_BLOCK, 256), small3, tq)
    o_f = _fox_prompt_call(qf.reshape(bp, lp, 512), kv3, ftok, fhm, tq)
    x2 = _outproj_call(x1, mod_p, gn, o_n.reshape(bp * lp, 512), o_f.reshape(bp * lp, 512),
                       w_merge, wa_pm, wb_pm, wo, pblk)
    y_prompt = _ffn_call(x2, mod_p, gn, *ffn_w[1], 2, pblk)
    nsa_p = nsa_new.reshape(1, bp, lp, 4, NSA_KV_HEADS, HEAD_DIM)
    fox_p = fox_new.reshape(1, bp, lp, 2, FOX_KV_HEADS, HEAD_DIM)
    win_p = win_new.reshape(bp, lp, 2, NSA_KV_HEADS, HEAD_DIM)[None, :, lp - min(WINDOW, lp):]

    sblk = (min(bs, 64), nq)
    pp = 8 if n_pages % 8 == 0 else 1
    x1s = _ffn_call(x_sample, mod_s, gn, *ffn_w[0], 0, sblk)
    qn_s, qf_s, nsa_s, win_s, fox_s, small_s, _ = _inproj_call(
        x1s, mod_s, gn, w1, gains, bf_pad, seg, wc, sblk, False)
    nrow = NSA_HEADS * nq
    qn_dec = _to_decode_rows(qn_s, bs, nq)
    qf_dec = _to_decode_rows(qf_s, bs, nq)
    rows = jnp.arange(nrow)
    rowc = jnp.zeros((nrow, LANES), F32).at[:, 0].set(jnp.array(SLOPES, F32)[rows // nq]).at[:, 1].set(
        (rows % nq).astype(F32))
    sm3 = small_s.reshape(bs, nq, LANES)
    gates = sm3[:, :, FOX_HEADS:FOX_HEADS + 3 * NSA_HEADS].reshape(bs, nq, 3, NSA_HEADS)
    gates = jnp.pad(gates.transpose(0, 3, 1, 2).reshape(bs, nrow, 3), ((0, 0), (0, 0), (0, LANES - 3)))
    newlf = _pad_rows(sm3[:, :, :FOX_HEADS], LANES).transpose(0, 2, 1)
    newn = _pad_rows(nsa_s.reshape(bs, nq, 512)[:, :, 256:], LANES)
    newf = _pad_rows(fox_s.reshape(bs, nq, 256), LANES)
    wnew = _pad_rows(win_s.reshape(bs, nq, 256), LANES)
    nsa_pool = cache_nsa_kv[0].reshape(n_pool, page, 512)
    fox_pool = cache_fox_kv[0].reshape(n_pool, page, 256)
    lf_pool = cache_fox_logf[0].astype(F32).transpose(0, 2, 1)
    wbuf = state_win_kv[0].reshape(bs, nbuf, 256)
    pt_flat = page_table.reshape(-1).astype(jnp.int32)
    o_c, sel, ftot = _dec_cmp_call(pt_flat, qn_dec, rowc, wc, nsa_pool, lf_pool, pp, n_pages, past_len)
    on_dec, of_dec, wst = _dec_main_call(pt_flat, qn_dec, qf_dec, sel, rowc, o_c, gates, ftot, newn, newf, newlf,
                                         wbuf, wnew, nsa_pool, fox_pool, lf_pool, pp, n_pages, past_len, nq)
    o_ns = _from_decode_rows(on_dec, bs, nq).astype(BF16)
    o_fs = _from_decode_rows(of_dec, bs, nq).astype(BF16)
    x2s = _outproj_call(x1s, mod_s, gn, o_ns, o_fs, w_merge, wa_nat, wb_nat, wo, sblk)
    y_sample = _ffn_call(x2s, mod_s, gn, *ffn_w[1], 2, sblk)
    nsa_o = nsa_s.reshape(1, bs, nq, 4, NSA_KV_HEADS, HEAD_DIM)
    fox_o = fox_s.reshape(1, bs, nq, 2, FOX_KV_HEADS, HEAD_DIM)
    win_o = wst.reshape(1, bs, nbuf, 2, NSA_KV_HEADS, HEAD_DIM)

    return (y_prompt, y_sample, nsa_p, nsa_o, fox_p, fox_o,
            logf_p[None], sm3[None, :, :, :FOX_HEADS], win_p, win_o)
```

```python
import functools

import jax
import jax.numpy as jnp
from jax import lax
from jax.experimental import pallas as pl
from jax.experimental.pallas import tpu as pltpu

HEAD_DIM = 64
NSA_HEADS = 8
NSA_KV_HEADS = 2
FOX_HEADS = 8
FOX_KV_HEADS = 2
HPG = 4
CMP_BLOCK = 64
CMP_SHIFT = 6
SLC_TOPK = 8
WINDOW = 512
N_SUB = 3
RMS_EPS = 1e-6
NEG_INF = -1e30
FORCE_BONUS = 1e4
LANES = 128
PAIR_PERM = (0, 4, 1, 5, 2, 6, 3, 7)
SLOPES = tuple(2.0 ** -(h + 1) for h in range(NSA_HEADS))
VMEM_LIMIT = 56 * 1024 * 1024

F32 = jnp.float32
BF16 = jnp.bfloat16


def _nt(a, b):
    return lax.dot_general(a, b, (((1,), (1,)), ((), ())), preferred_element_type=F32)


def _nn(a, b):
    return jnp.dot(a, b, preferred_element_type=F32)


def _split3(x):
    hi = x.astype(BF16)
    r1 = x - hi.astype(F32)
    mid = r1.astype(BF16)
    lo = (r1 - mid.astype(F32)).astype(BF16)
    return hi, mid, lo


def _params(sem):
    return pltpu.CompilerParams(dimension_semantics=sem, vmem_limit_bytes=VMEM_LIMIT)


def _const_spec(shape):
    nd = len(shape)
    return pl.BlockSpec(shape, lambda *_: (0,) * nd, pipeline_mode=pl.Buffered(1))


def _mod_kernel(c_ref, w_ref, b_ref, o_ref):
    c = c_ref[...]
    a = (c * jax.nn.sigmoid(c)).astype(BF16)
    o_ref[...] = _nn(a, w_ref[...].astype(BF16)) + b_ref[...]


def _mod_call(c_all, w_ada, b_ada):
    bc, d = c_all.shape
    n = w_ada.shape[1]
    tn = 1024 if n % 1024 == 0 else n
    return pl.pallas_call(
        _mod_kernel,
        grid=(n // tn,),
        in_specs=[pl.BlockSpec((bc, d), lambda i: (0, 0)),
                  pl.BlockSpec((d, tn), lambda i: (0, i)),
                  pl.BlockSpec((1, tn), lambda i: (0, i))],
        out_specs=pl.BlockSpec((bc, tn), lambda i: (0, i)),
        out_shape=jax.ShapeDtypeStruct((bc, n), F32),
        compiler_params=_params(("parallel",)),
        name="adaln_mod",
    )(c_all, w_ada, b_ada.reshape(1, n))


def _pre(x, mod_ref, g_ref, sub):
    ms = jnp.mean(x * x, axis=-1, keepdims=True)
    y = x * lax.rsqrt(ms + RMS_EPS) * g_ref[sub:sub + 1, :]
    return y * (1.0 + mod_ref[:, 3 * sub + 1:3 * sub + 2, :]) + mod_ref[:, 3 * sub:3 * sub + 1, :]


def _ffn_kernel(x_ref, mod_ref, g_ref, wa_ref, wb_ref, wd_ref, o_ref, act_ref, *, sub, fc):
    x = x_ref[...]
    nb, nt, d = x.shape
    h = _pre(x, mod_ref, g_ref, sub).reshape(nb * nt, d).astype(BF16)
    dff = wa_ref.shape[1]
    for c in range(dff // fc):
        a = _nn(h, wa_ref[:, c * fc:(c + 1) * fc])
        b = _nn(h, wb_ref[:, c * fc:(c + 1) * fc])
        act_ref[:, c * fc:(c + 1) * fc] = (a * jax.nn.sigmoid(a) * b).astype(BF16)
    y = _nn(act_ref[...], wd_ref[...]).reshape(nb, nt, d)
    o_ref[...] = x + 0.5 * mod_ref[:, 3 * sub + 2:3 * sub + 3, :] * y


def _ffn_call(x3, mod, g_norm, wa, wb, wd, sub, blk):
    bs, lt, d = x3.shape
    nb, nt = blk
    dff = wa.shape[1]
    fc = 256 if dff % 256 == 0 else dff
    grid = (bs // nb, lt // nt)
    return pl.pallas_call(
        functools.partial(_ffn_kernel, sub=sub, fc=fc),
        grid=grid,
        in_specs=[pl.BlockSpec((nb, nt, d), lambda i, j: (i, j, 0)),
                  pl.BlockSpec((nb, 3 * N_SUB, d), lambda i, j: (i, 0, 0)),
                  _const_spec(g_norm.shape), _const_spec(wa.shape), _const_spec(wb.shape),
                  _const_spec(wd.shape)],
        out_specs=pl.BlockSpec((nb, nt, d), lambda i, j: (i, j, 0)),
        out_shape=jax.ShapeDtypeStruct(x3.shape, F32),
        scratch_shapes=[pltpu.VMEM((nb * nt, dff), BF16)],
        compiler_params=_params(("parallel", "parallel")),
        name="ffn_sub%d" % sub,
    )(x3, mod, g_norm, wa, wb, wd)


N_NORMED = 12


def _inproj_kernel(x_ref, mod_ref, g_ref, w_ref, gain_ref, bf_ref, seg_ref, *rest, with_cmp):
    if with_cmp:
        wc_ref, qn_ref, qf_ref, nsa_ref, win_ref, fox_ref, sm_ref, kv_ref, cmp_ref = rest
    else:
        qn_ref, qf_ref, nsa_ref, win_ref, fox_ref, sm_ref, kv_ref = rest
    x = x_ref[...]
    nb, nt, d = x.shape
    h = _pre(x, mod_ref, g_ref, 1).reshape(nb * nt, d).astype(BF16)
    z = _nn(h, w_ref[...])
    seg = seg_ref[...]
    normed = []
    for c in range(N_NORMED):
        zc = z[:, c * LANES:(c + 1) * LANES]
        zz = zc * zc
        hi = zz.astype(BF16)
        lo = (zz - hi.astype(F32)).astype(BF16)
        ss = _nn(hi, seg) + _nn(lo, seg)
        normed.append(zc * lax.rsqrt(ss * (1.0 / HEAD_DIM) + RMS_EPS) * gain_ref[:, c * LANES:(c + 1) * LANES])
    raw = [z[:, (N_NORMED + c) * LANES:(N_NORMED + c + 1) * LANES] for c in range(4)]
    k_cmp, k_slc, k_win, k_fox = normed[8:12]
    v_cmp, v_slc, v_win, v_fox = raw
    for j in range(4):
        qn_ref[:, j * LANES:(j + 1) * LANES] = normed[j].astype(BF16)
        qf_ref[:, j * LANES:(j + 1) * LANES] = normed[4 + j].astype(BF16)
    for c, t in enumerate((k_cmp, v_cmp, k_slc, v_slc)):
        nsa_ref[:, c * LANES:(c + 1) * LANES] = t
    win_ref[:, :LANES] = k_win
    win_ref[:, LANES:] = v_win
    fox_ref[:, :LANES] = k_fox
    fox_ref[:, LANES:] = v_fox
    for c, t in enumerate((k_slc, v_slc, k_win, v_win, k_fox, v_fox)):
        kv_ref[:, c * LANES:(c + 1) * LANES] = t.astype(BF16)
    zs = z[:, 16 * LANES:17 * LANES]
    lane = lax.broadcasted_iota(jnp.int32, zs.shape, 1)
    zf = zs + bf_ref[...]
    logf = jnp.minimum(zf, 0.0) - jnp.log1p(jnp.exp(-jnp.abs(zf)))
    sm_ref[...] = jnp.where(lane < FOX_HEADS, logf, jax.nn.sigmoid(zs))
    if with_cmp:
        rows = nb * nt
        kvc = jnp.concatenate([k_cmp, v_cmp], axis=1).reshape(rows // CMP_BLOCK, CMP_BLOCK, 2 * LANES)
        cmp_ref[...] = jnp.sum(kvc * wc_ref[...][None], axis=1)


def _inproj_call(x3, mod, g_norm, w1, gains, bf_pad, seg, wc, blk, with_cmp):
    bs, lt, d = x3.shape
    nb, nt = blk
    rows = nb * nt
    t = bs * lt
    grid = (bs // nb, lt // nt)
    nj = lt // nt

    def tok(cols):
        return pl.BlockSpec((rows, cols), lambda i, j: (i * nj + j, 0))

    in_specs = [pl.BlockSpec((nb, nt, d), lambda i, j: (i, j, 0)),
                pl.BlockSpec((nb, 3 * N_SUB, d), lambda i, j: (i, 0, 0)),
                _const_spec(g_norm.shape), _const_spec(w1.shape), _const_spec(gains.shape),
                _const_spec(bf_pad.shape), _const_spec(seg.shape)]
    args = [x3, mod, g_norm, w1, gains, bf_pad, seg]
    out_specs = [tok(512), tok(512), tok(512), tok(256), tok(256), tok(128), tok(768)]
    out_shape = [jax.ShapeDtypeStruct((t, 512), BF16), jax.ShapeDtypeStruct((t, 512), BF16),
                 jax.ShapeDtypeStruct((t, 512), F32), jax.ShapeDtypeStruct((t, 256), F32),
                 jax.ShapeDtypeStruct((t, 256), F32), jax.ShapeDtypeStruct((t, 128), F32),
                 jax.ShapeDtypeStruct((t, 768), BF16)]
    if with_cmp:
        in_specs.append(_const_spec(wc.shape))
        args.append(wc)
        out_specs.append(pl.BlockSpec((rows // CMP_BLOCK, 256), lambda i, j: (i * nj + j, 0)))
        out_shape.append(jax.ShapeDtypeStruct((t // CMP_BLOCK, 256), F32))
    return pl.pallas_call(
        functools.partial(_inproj_kernel, with_cmp=with_cmp),
        grid=grid, in_specs=in_specs, out_specs=out_specs, out_shape=out_shape,
        compiler_params=_params(("parallel", "parallel")),
        name="in_proj",
    )(*args)


FQ0 = 0
FK0 = 3 * FOX_HEADS


def _cumsum_kernel(tok_ref, k_ref, kaug_ref, qx_ref, carry_ref):
    @pl.when(pl.program_id(1) == 0)
    def _():
        carry_ref[...] = jnp.zeros_like(carry_ref)

    tc = tok_ref.shape[0]
    r = lax.broadcasted_iota(jnp.int32, (tc, tc), 0)
    c = lax.broadcasted_iota(jnp.int32, (tc, tc), 1)
    lower = jnp.where(r >= c, 1.0, 0.0).astype(BF16)
    cs = carry_ref[...] + sum(_nn(lower, p) for p in _split3(tok_ref[...]))
    carry_ref[...] = cs[tc - 1:tc, :]
    hrow = lax.broadcasted_iota(jnp.int32, (LANES, LANES), 0)
    col = lax.broadcasted_iota(jnp.int32, (LANES, LANES), 1)
    placed = jnp.zeros((tc, LANES), F32)
    for p, piece in enumerate(_split3(cs)):
        tgt = 3 * hrow + p
        place = jnp.where((hrow < FOX_HEADS) & ((col == FQ0 + tgt) | (col == FK0 + tgt)), 1.0, 0.0).astype(BF16)
        placed = placed + _nn(piece, place)
    lane = lax.broadcasted_iota(jnp.int32, (tc, LANES), 1)
    in_q = lane < FK0
    in_k = (lane >= FK0) & (lane < 2 * FK0)
    qx_ref[...] = jnp.where(in_q, placed, jnp.where(in_k, 1.0, 0.0)).astype(BF16)
    kaug_ref[:, :LANES] = k_ref[...]
    kaug_ref[:, LANES:] = jnp.where(in_q, 1.0, jnp.where(in_k, -placed, 0.0)).astype(BF16)


def _cumsum_call(small3, kv3, tc):
    b, l, _ = small3.shape
    return pl.pallas_call(
        _cumsum_kernel,
        grid=(b, l // tc),
        in_specs=[pl.BlockSpec((None, tc, LANES), lambda i, j: (i, j, 0)),
                  pl.BlockSpec((None, tc, LANES), lambda i, j: (i, j, 4))],
        out_specs=[pl.BlockSpec((None, tc, 2 * LANES), lambda i, j: (i, j, 0)),
                   pl.BlockSpec((None, tc, LANES), lambda i, j: (i, j, 0))],
        out_shape=[jax.ShapeDtypeStruct((b, l, 2 * LANES), BF16),
                   jax.ShapeDtypeStruct((b, l, LANES), BF16)],
        scratch_shapes=[pltpu.VMEM((1, LANES), F32)],
        compiler_params=_params(("parallel", "arbitrary")),
        name="fox_cumsum",
    )(small3, kv3)


def _flash_step(s, v, m, l, acc):
    m_new = jnp.maximum(m, jnp.max(s, axis=-1, keepdims=True))
    alpha = jnp.exp(m - m_new)
    p = jnp.exp(s - m_new)
    l = alpha * l + jnp.sum(p, axis=-1, keepdims=True)
    acc = alpha * acc + _nn(p.astype(BF16), v)
    return m_new, l, acc


def _flash_init(rows):
    return (jnp.full((rows, 1), NEG_INF, F32), jnp.zeros((rows, 1), F32), jnp.zeros((rows, LANES), F32))


def _group_masked(qp, g):
    lane = lax.broadcasted_iota(jnp.int32, qp.shape, 1)
    keep = (lane < HEAD_DIM) if g == 0 else (lane >= HEAD_DIM)
    return jnp.where(keep, qp, jnp.zeros_like(qp))


def _top_blocks(score, n_pick, axis=1):
    idx = lax.broadcasted_iota(jnp.int32, score.shape, axis).astype(F32)
    sel = jnp.zeros(score.shape, F32)
    for _ in range(n_pick):
        mx = jnp.max(score, axis=axis, keepdims=True)
        first = jnp.min(jnp.where(score == mx, idx, 1e9), axis=axis, keepdims=True)
        pick = idx == first
        sel = jnp.where(pick, 1.0, sel)
        score = jnp.where(pick, -jnp.inf, score)
    return sel


XL_ONE_A = CMP_BLOCK
XL_ONE_B = CMP_BLOCK + 1
XL_HI = CMP_BLOCK + 2
XL_LO = CMP_BLOCK + 3
MASK_BIG = 1e30


VT_ROWS = LANES + 8


def _col_max(s):
    k = s.shape[0]
    if k % 64 == 0:
        s = jnp.max(s.reshape(8, k // 8, s.shape[1]), axis=0)
    return jnp.max(s, axis=0, keepdims=True)


def _heads_t_step(k, vt, qa_ref, m_ref, acc_ref, ok):
    nh = qa_ref.shape[0]
    scores = []
    for h in range(nh):
        s = _nt(k, qa_ref[h])
        scores.append(s if ok is None else jnp.where(ok, s, NEG_INF))
    probs, alphas = [], []
    for h in range(nh):
        m_old = m_ref[h:h + 1, :]
        m_new = jnp.maximum(m_old, _col_max(scores[h]))
        m_ref[h:h + 1, :] = m_new
        probs.append(jnp.exp(scores[h] - m_new).astype(BF16))
        alphas.append(jnp.exp(m_old - m_new))
    for h in range(nh):
        acc_ref[h] = alphas[h] * acc_ref[h] + _nn(vt, probs[h])


def _attn_scratch(tq):
    return [pltpu.VMEM((NSA_HEADS, tq, 2 * LANES), BF16), pltpu.VMEM((NSA_HEADS, tq, 2 * LANES), BF16),
            pltpu.VMEM((NSA_HEADS, tq), F32), pltpu.VMEM((NSA_HEADS, VT_ROWS, tq), F32),
            pltpu.VMEM((NSA_HEADS, LANES, tq), F32)]


def _finish_t(acc):
    return acc[:LANES, :] / acc[LANES:LANES + 1, :]


def _nsa_prompt_kernel(q_ref, ks_ref, kw_ref, kx_ref, vst_ref, vwt_ref, cmp_ref, cx_ref, sm_ref, o_ref,
                       qas_ref, qaw_ref, m_ref, acc_ref, out_ref, *, tq, tk, nb):
    qi = pl.program_id(1)
    lane = lax.broadcasted_iota(jnp.int32, (tq, LANES), 1)
    t2 = qi * tq + lax.broadcasted_iota(jnp.int32, (tq, LANES), 0)
    t_hi = lax.shift_right_logical(t2, CMP_SHIFT).astype(F32)
    t_lo = (t2 & (CMP_BLOCK - 1)).astype(F32)
    alibi_q = jnp.where(lane == XL_ONE_A, -CMP_BLOCK * t_hi,
                        jnp.where(lane == XL_ONE_B, -t_lo,
                                  jnp.where(lane == XL_HI, float(CMP_BLOCK), jnp.where(lane == XL_LO, 1.0, 0.0))))
    krow = lax.broadcasted_iota(jnp.int32, (tk, tq), 0)
    qcol = lax.broadcasted_iota(jnp.int32, (tk, tq), 1)
    rows = lax.broadcasted_iota(jnp.int32, (LANES, tq), 0)

    cmpv = cmp_ref[...]
    pad = jnp.zeros((LANES - nb, LANES), F32)
    ck = jnp.concatenate([cmpv[:, :LANES], pad], axis=0).astype(BF16)
    cvt = jnp.concatenate([cmpv[:, LANES:], pad], axis=0).T.astype(BF16)
    ck_aug = jnp.concatenate([ck, cx_ref[...]], axis=1)
    tcol = qi * tq + lax.broadcasted_iota(jnp.int32, (LANES, tq), 1)
    valid_c = ((rows + 1) * CMP_BLOCK - 1 <= tcol) & (rows < nb)
    cur = lax.shift_right_logical(tcol, CMP_SHIFT)
    forced = (rows == 0) | (rows == cur) | (rows == cur - 1)
    started = rows * CMP_BLOCK <= tcol
    smt = sm_ref[...].T

    for g in range(NSA_KV_HEADS):
        imp = jnp.zeros((LANES, tq), F32)
        qx = []
        for j in range(HPG):
            h = g * HPG + j
            qx.append(SLOPES[h] * alibi_q)
            qaw_ref[h, :, :LANES] = _group_masked(q_ref[:, j * LANES:(j + 1) * LANES], g)
            qaw_ref[h, :, LANES:] = qx[j].astype(BF16)
            s = jnp.where(valid_c, _nt(ck_aug, qaw_ref[h]), NEG_INF)
            e = jnp.exp(s - jnp.max(s, axis=0, keepdims=True))
            p = jnp.where(valid_c, e / jnp.sum(e, axis=0, keepdims=True), 0.0)
            imp = imp + p
            c0 = FOX_HEADS + h
            out_ref[h] = smt[c0:c0 + 1, :] * _nn(cvt, p.astype(BF16))
        score = jnp.where(started, imp + jnp.where(forced, FORCE_BONUS, 0.0), NEG_INF)
        score = jnp.where(rows < nb, score, 2 * NEG_INF)
        sel = _top_blocks(score, min(SLC_TOPK, nb), axis=0).T
        sel_q = jnp.where(lane < CMP_BLOCK, (sel - 1.0) * MASK_BIG, 0.0)
        for j in range(HPG):
            h = g * HPG + j
            qas_ref[h, :, :LANES] = qaw_ref[h, :, :LANES]
            qas_ref[h, :, LANES:] = (qx[j] + sel_q).astype(BF16)

    kpq = tq // tk

    def run_branch(qa_ref, k_ref, vt_ref, lo, window, gate_row):
        m_ref[...] = jnp.full(m_ref.shape, NEG_INF, F32)
        acc_ref[...] = jnp.zeros_like(acc_ref)

        def tile(kj, masked):
            off = pl.multiple_of(kj * tk, tk)
            ka = jnp.concatenate([k_ref[pl.ds(off, tk), :], kx_ref[pl.ds(off, tk), :]], axis=1)
            vt = vt_ref[kj]
            if masked:
                dist = qi * tq - kj * tk + qcol - krow
                ok = dist >= 0
                if window:
                    ok = ok & (dist < WINDOW)
            _heads_t_step(ka, vt, qa_ref, m_ref, acc_ref, ok if masked else None)

        def body(kj, c):
            tile(kj, window)
            return c

        if window:
            lax.fori_loop(lo, (qi + 1) * kpq, body, 0)
        else:
            lax.fori_loop(lo, qi * kpq, body, 0)
            for d in range(kpq):
                tile(qi * kpq + d, True)
        for h in range(NSA_HEADS):
            r = gate_row + h
            out_ref[h] = out_ref[h] + smt[r:r + 1, :] * _finish_t(acc_ref[h])

    run_branch(qas_ref, ks_ref, vst_ref, 0, False, FOX_HEADS + NSA_HEADS)
    run_branch(qaw_ref, kw_ref, vwt_ref, jnp.maximum(qi * kpq - (WINDOW + tk - 1) // tk, 0), True,
               FOX_HEADS + 2 * NSA_HEADS)
    for j in range(HPG):
        pair_t = jnp.where(rows < HEAD_DIM, out_ref[j], out_ref[HPG + j])
        o_ref[:, j * LANES:(j + 1) * LANES] = pair_t.T.astype(BF16)


def _nsa_prompt_call(qn3, kv3, kx, vst, vwt, cmp3, cx, small3, tq, tk):
    b, l, _ = qn3.shape
    nb = cmp3.shape[1]
    assert nb <= CMP_BLOCK, "one-hot block lanes cover at most CMP_BLOCK blocks"

    def kspec(c):
        return pl.BlockSpec((None, l, LANES), lambda i, j: (i, 0, c))

    vspec = pl.BlockSpec((None, l // tk, VT_ROWS, tk), lambda i, j: (i, 0, 0, 0))
    return pl.pallas_call(
        functools.partial(_nsa_prompt_kernel, tq=tq, tk=tk, nb=nb),
        grid=(b, l // tq),
        in_specs=[pl.BlockSpec((None, tq, 512), lambda i, j: (i, j, 0)),
                  kspec(0), kspec(2), pl.BlockSpec((l, LANES), lambda i, j: (0, 0)), vspec, vspec,
                  pl.BlockSpec((None, nb, 256), lambda i, j: (i, 0, 0)),
                  pl.BlockSpec((LANES, LANES), lambda i, j: (0, 0)),
                  pl.BlockSpec((None, tq, LANES), lambda i, j: (i, j, 0))],
        out_specs=pl.BlockSpec((None, tq, 512), lambda i, j: (i, j, 0)),
        out_shape=jax.ShapeDtypeStruct((b, l, 512), BF16),
        scratch_shapes=_attn_scratch(tq),
        compiler_params=_params(("parallel", "arbitrary")),
        name="nsa_prompt",
    )(qn3, kv3, kv3, kx, vst, vwt, cmp3, cx, small3)


def _fox_prompt_kernel(q_ref, qx_ref, k_ref, vt_ref, o_ref, qa_ref, m_ref, acc_ref, *, tq, tk):
    qi = pl.program_id(1)
    lane = lax.broadcasted_iota(jnp.int32, (tq, LANES), 1)
    krow = lax.broadcasted_iota(jnp.int32, (tk, tq), 0)
    qcol = lax.broadcasted_iota(jnp.int32, (tk, tq), 1)
    rows = lax.broadcasted_iota(jnp.int32, (LANES, tq), 0)
    qx = qx_ref[...]
    for g in range(FOX_KV_HEADS):
        for j in range(HPG):
            h = g * HPG + j
            own = ((lane >= FQ0 + 3 * h) & (lane < FQ0 + 3 * h + 3)) | ((lane >= FK0 + 3 * h) & (lane < FK0 + 3 * h + 3))
            qa_ref[h, :, :LANES] = _group_masked(q_ref[:, j * LANES:(j + 1) * LANES], g)
            qa_ref[h, :, LANES:] = jnp.where(own, qx, jnp.zeros_like(qx))
    m_ref[...] = jnp.full(m_ref.shape, NEG_INF, F32)
    acc_ref[...] = jnp.zeros_like(acc_ref)
    kpq = tq // tk

    def tile(kj, masked):
        k = k_ref[pl.ds(pl.multiple_of(kj * tk, tk), tk), :]
        vt = vt_ref[kj]
        if masked:
            ok = qi * tq - kj * tk + qcol - krow >= 0
        _heads_t_step(k, vt, qa_ref, m_ref, acc_ref, ok if masked else None)

    def body(kj, c):
        tile(kj, False)
        return c

    lax.fori_loop(0, qi * kpq, body, 0)
    for d in range(kpq):
        tile(qi * kpq + d, True)
    for j in range(HPG):
        pair_t = jnp.where(rows < HEAD_DIM, _finish_t(acc_ref[j]), _finish_t(acc_ref[HPG + j]))
        o_ref[:, j * LANES:(j + 1) * LANES] = pair_t.T.astype(BF16)


def _fox_prompt_call(qf3, qx3, kaug3, vft, tq, tk):
    b, l, _ = qf3.shape
    return pl.pallas_call(
        functools.partial(_fox_prompt_kernel, tq=tq, tk=tk),
        grid=(b, l // tq),
        in_specs=[pl.BlockSpec((None, tq, 512), lambda i, j: (i, j, 0)),
                  pl.BlockSpec((None, tq, LANES), lambda i, j: (i, j, 0)),
                  pl.BlockSpec((None, l, 2 * LANES), lambda i, j: (i, 0, 0)),
                  pl.BlockSpec((None, l // tk, VT_ROWS, tk), lambda i, j: (i, 0, 0, 0))],
        out_specs=pl.BlockSpec((None, tq, 512), lambda i, j: (i, j, 0)),
        out_shape=jax.ShapeDtypeStruct((b, l, 512), BF16),
        scratch_shapes=_attn_scratch(tq)[1:4],
        compiler_params=_params(("parallel", "arbitrary")),
        name="fox_prompt",
    )(qf3, qx3, kaug3, vft)


def _lane_cumsum(x):
    n = x.shape[1]
    r = lax.broadcasted_iota(jnp.int32, (n, n), 0)
    c = lax.broadcasted_iota(jnp.int32, (n, n), 1)
    upper = jnp.where(r <= c, 1.0, 0.0).astype(BF16)
    return sum(_nn(p, upper) for p in _split3(x))


def _dec_cmp_kernel(pt_ref, q_ref, rowc_ref, wc_ref, *rest, pp, nb, past_len):
    del pt_ref
    pages = rest[:pp]
    lfs = rest[pp:2 * pp]
    oc_ref, sel_ref, ftot_ref, cmp_scr, ftot_scr = rest[2 * pp:]
    c = pl.program_id(1)

    @pl.when(c == 0)
    def _():
        ftot_scr[...] = jnp.zeros_like(ftot_scr)

    big = jnp.concatenate([p[...] for p in pages], axis=0)
    rows = big.shape[0]
    nblk = rows // CMP_BLOCK
    blk = jnp.sum(big.reshape(nblk, CMP_BLOCK, 2 * LANES) * wc_ref[...][None], axis=1)
    cmp_scr[pl.ds(pl.multiple_of(c * nblk, nblk), nblk), :] = blk
    tot = lfs[0][...]
    for p in lfs[1:]:
        tot = tot + p[...]
    ftot_scr[...] += jnp.broadcast_to(jnp.sum(tot, axis=-1, keepdims=True), ftot_scr.shape)

    @pl.when(c == pl.num_programs(1) - 1)
    def _():
        ftot_ref[...] = ftot_scr[...]
        cmpv = cmp_scr[...]
        ckb = cmpv[:, :LANES].astype(BF16)
        cvb = cmpv[:, LANES:].astype(BF16)
        nrow = q_ref.shape[0]
        rowc = rowc_ref[...]
        slope = rowc[:, 0:1]
        t = past_len + rowc[:, 1:2]
        nidx = lax.broadcasted_iota(jnp.int32, (nrow, nb), 1)
        cmid = nidx.astype(F32) * CMP_BLOCK + 0.5 * (CMP_BLOCK - 1)
        valid = cmid + 0.5 * (CMP_BLOCK - 1) <= t
        s = jnp.where(valid, _nt(q_ref[...], ckb) - slope * (t - cmid), NEG_INF)
        e = jnp.exp(s - jnp.max(s, axis=-1, keepdims=True))
        p = jnp.where(valid, e / jnp.sum(e, axis=-1, keepdims=True), 0.0)
        oc_ref[...] = _nn(p.astype(BF16), cvb)
        nq = nrow // NSA_HEADS
        n8 = lax.broadcasted_iota(jnp.int32, (nq, nb), 1)
        forced = (n8 == 0) | (n8 == nb - 1)
        sel_rows = []
        for g in range(NSA_KV_HEADS):
            imp = jnp.zeros((nq, nb), F32)
            for j in range(HPG):
                r0 = (g * HPG + j) * nq
                imp = imp + p[r0:r0 + nq, :]
            sel = _top_blocks(imp + jnp.where(forced, FORCE_BONUS, 0.0), SLC_TOPK - 1)
            sel_rows += [sel] * HPG
        sel_ref[...] = jnp.concatenate(sel_rows, axis=0).astype(BF16)


def _page_specs(pp, n_pages, block, col):
    def one(i):
        return pl.BlockSpec(block, lambda b, c, pt: (pt[b * n_pages + c * pp + i], 0, col))
    return [one(i) for i in range(pp)]


def _dec_cmp_call(pt_flat, q_dec, rowc, wc, nsa_pool, lf_pool, pp, n_pages, past_len):
    b, nrow, _ = q_dec.shape
    nb = past_len // CMP_BLOCK
    page = nsa_pool.shape[1]
    nc = n_pages // pp
    per = lambda shape: pl.BlockSpec((None,) + shape, lambda i, c, pt: (i, 0, 0))
    cst = lambda shape: pl.BlockSpec(shape, lambda i, c, pt: (0, 0))
    grid_spec = pltpu.PrefetchScalarGridSpec(
        num_scalar_prefetch=1,
        grid=(b, nc),
        in_specs=[per((nrow, LANES)), cst(rowc.shape), cst(wc.shape)]
        + _page_specs(pp, n_pages, (None, page, 2 * LANES), 0)
        + _page_specs(pp, n_pages, (None, FOX_HEADS, page), 0),
        out_specs=[per((nrow, LANES)), per((nrow, nb)), per((FOX_HEADS, LANES))],
        scratch_shapes=[pltpu.VMEM((nb, 2 * LANES), F32), pltpu.VMEM((FOX_HEADS, LANES), F32)],
    )
    return pl.pallas_call(
        functools.partial(_dec_cmp_kernel, pp=pp, nb=nb, past_len=past_len),
        grid_spec=grid_spec,
        out_shape=[jax.ShapeDtypeStruct((b, nrow, LANES), F32),
                   jax.ShapeDtypeStruct((b, nrow, nb), BF16),
                   jax.ShapeDtypeStruct((b, FOX_HEADS, LANES), F32)],
        compiler_params=_params(("parallel", "arbitrary")),
        name="decode_cmp",
    )(pt_flat, q_dec, rowc, wc, *([nsa_pool] * pp), *([lf_pool] * pp))


def _rep_rows(x, n):
    return jnp.concatenate([jnp.broadcast_to(x[h:h + 1, :], (n, x.shape[1])) for h in range(x.shape[0])], axis=0)


def _scr_flash(s, v, m_ref, l_ref, acc_ref):
    m, l, acc = _flash_step(s, v, m_ref[:, 0:1], l_ref[:, 0:1], acc_ref[...])
    m_ref[...] = jnp.broadcast_to(m, m_ref.shape)
    l_ref[...] = jnp.broadcast_to(l, l_ref.shape)
    acc_ref[...] = acc


def _dec_main_kernel(pt_ref, qn_ref, qf_ref, sel_ref, rowc_ref, oc_ref, gate_ref, ftot_ref,
                     newn_ref, newf_ref, newlf_ref, wbuf_ref, wnew_ref, *rest, pp, nb, past_len, nq):
    del pt_ref
    slc = rest[:pp]
    fox = rest[pp:2 * pp]
    lfs = rest[2 * pp:3 * pp]
    (on_ref, of_ref, wst_ref, ms_ref, ls_ref, as_ref, mf_ref, lf_ref, af_ref, fcar_ref, ftc_ref) = rest[3 * pp:]
    c = pl.program_id(1)
    nrow = qn_ref.shape[0]
    page = slc[0].shape[0]
    kc = pp * page
    rowc = rowc_ref[...]
    slope = rowc[:, 0:1]
    t = past_len + rowc[:, 1:2]
    qn = qn_ref[...]
    qf = qf_ref[...]

    @pl.when(c == 0)
    def _():
        for r in (ms_ref, mf_ref):
            r[...] = jnp.full(r.shape, NEG_INF, F32)
        for r in (ls_ref, as_ref, lf_ref, af_ref, fcar_ref):
            r[...] = jnp.zeros_like(r)
        f_new = ftot_ref[...] + _lane_cumsum(newlf_ref[...])
        rep = _rep_rows(f_new, nq)
        lane = lax.broadcasted_iota(jnp.int32, rep.shape, 1)
        tok = lax.broadcasted_iota(jnp.int32, rep.shape, 0) & (nq - 1)
        ftc_ref[...] = jnp.broadcast_to(
            jnp.sum(jnp.where(lane == tok, rep, 0.0), axis=-1, keepdims=True), ftc_ref.shape)

    ft = ftc_ref[:, 0:1]

    ks = jnp.concatenate([p[:, :LANES].astype(BF16) for p in slc], axis=0)
    vs = jnp.concatenate([p[:, LANES:].astype(BF16) for p in slc], axis=0)
    pos = (c * kc + lax.broadcasted_iota(jnp.int32, (1, kc), 1)).astype(F32)
    erow = lax.broadcasted_iota(jnp.int32, (nb, kc), 0)
    ecol = lax.shift_right_logical(lax.broadcasted_iota(jnp.int32, (nb, kc), 1), CMP_SHIFT) + c * (kc // CMP_BLOCK)
    expand = jnp.where(erow == ecol, 1.0, 0.0).astype(BF16)
    ok = _nn(sel_ref[...], expand) > 0.5
    s = jnp.where(ok, _nt(qn, ks) - slope * (t - pos), NEG_INF)
    _scr_flash(s, vs, ms_ref, ls_ref, as_ref)

    kf = jnp.concatenate([p[:, :LANES].astype(BF16) for p in fox], axis=0)
    vf = jnp.concatenate([p[:, LANES:].astype(BF16) for p in fox], axis=0)
    car = fcar_ref[:, 0:1]
    fs_parts = []
    for p in lfs:
        cs = car + _lane_cumsum(p[...])
        fs_parts.append(cs)
        car = cs[:, page - 1:page]
    fcar_ref[...] = jnp.broadcast_to(car, fcar_ref.shape)
    fs = _rep_rows(jnp.concatenate(fs_parts, axis=1), nq)
    _scr_flash(_nt(qf, kf) + (ft - fs), vf, mf_ref, lf_ref, af_ref)

    @pl.when(c == pl.num_programs(1) - 1)
    def _():
        npad = newn_ref.shape[0]
        lane_i = lax.broadcasted_iota(jnp.int32, (1, npad), 1)
        posn = (past_len + lane_i).astype(F32)
        okn = (lane_i < nq) & (posn <= t)
        newn = newn_ref[...]
        sn = jnp.where(okn, _nt(qn, newn[:, :LANES].astype(BF16)) - slope * (t - posn), NEG_INF)
        m, l, acc = _flash_step(sn, newn[:, LANES:].astype(BF16), ms_ref[:, 0:1], ls_ref[:, 0:1], as_ref[...])
        o_s = acc / l
        newf = newf_ref[...]
        fsn = _rep_rows(fcar_ref[:, 0:1] + _lane_cumsum(newlf_ref[...]), nq)
        sf = jnp.where(okn, _nt(qf, newf[:, :LANES].astype(BF16)) + (ft - fsn), NEG_INF)
        m, l, acc = _flash_step(sf, newf[:, LANES:].astype(BF16), mf_ref[:, 0:1], lf_ref[:, 0:1], af_ref[...])
        of_ref[...] = acc / l
        wbuf = wbuf_ref[...]
        wnew = wnew_ref[...]
        nbuf = wbuf.shape[0]
        kw = jnp.concatenate([wbuf[:, :LANES], wnew[:, :LANES]], axis=0).astype(BF16)
        vw = jnp.concatenate([wbuf[:, LANES:], wnew[:, LANES:]], axis=0).astype(BF16)
        wl = lax.broadcasted_iota(jnp.int32, (1, nbuf + npad), 1)
        posw = jnp.where(wl < nbuf, past_len - nbuf + wl, past_len + wl - nbuf).astype(F32)
        dw = t - posw
        okw = (dw >= 0) & (dw < WINDOW) & ((wl < nbuf) | (wl < nbuf + nq))
        sw = jnp.where(okw, _nt(qn, kw) - slope * dw, NEG_INF)
        ew = jnp.exp(sw - jnp.max(sw, axis=-1, keepdims=True))
        o_w = _nn(ew.astype(BF16), vw) / jnp.sum(ew, axis=-1, keepdims=True)
        gate = gate_ref[...]
        on_ref[...] = gate[:, 0:1] * oc_ref[...] + gate[:, 1:2] * o_s + gate[:, 2:3] * o_w
        wst_ref[0:nbuf - nq, :] = wbuf[nq:, :]
        wst_ref[nbuf - nq:, :] = wnew[0:nq, :]


def _dec_main_call(pt_flat, qn_dec, qf_dec, sel, rowc, o_c, gates, ftot, newn, newf, newlf, wbuf, wnew,
                   nsa_pool, fox_pool, lf_pool, pp, n_pages, past_len, nq):
    b, nrow, _ = qn_dec.shape
    nb = past_len // CMP_BLOCK
    page = nsa_pool.shape[1]
    nc = n_pages // pp
    nbuf = wbuf.shape[1]
    per = lambda shape: pl.BlockSpec((None,) + shape, lambda i, c, pt: (i, 0, 0))
    cst = lambda shape: pl.BlockSpec(shape, lambda i, c, pt: (0, 0))
    rl = (nrow, LANES)
    grid_spec = pltpu.PrefetchScalarGridSpec(
        num_scalar_prefetch=1,
        grid=(b, nc),
        in_specs=[per(rl), per(rl), per((nrow, nb)), cst(rowc.shape), per(rl), per(rl), per((FOX_HEADS, LANES)),
                  per(newn.shape[1:]), per(newf.shape[1:]), per(newlf.shape[1:]),
                  per((nbuf, 2 * LANES)), per(wnew.shape[1:])]
        + _page_specs(pp, n_pages, (None, page, 2 * LANES), 1)
        + _page_specs(pp, n_pages, (None, page, 2 * LANES), 0)
        + _page_specs(pp, n_pages, (None, FOX_HEADS, page), 0),
        out_specs=[per(rl), per(rl), per((nbuf, 2 * LANES))],
        scratch_shapes=[pltpu.VMEM(rl, F32)] * 6 + [pltpu.VMEM((FOX_HEADS, LANES), F32), pltpu.VMEM(rl, F32)],
    )
    return pl.pallas_call(
        functools.partial(_dec_main_kernel, pp=pp, nb=nb, past_len=past_len, nq=nq),
        grid_spec=grid_spec,
        out_shape=[jax.ShapeDtypeStruct((b,) + rl, F32), jax.ShapeDtypeStruct((b,) + rl, F32),
                   jax.ShapeDtypeStruct((b, nbuf, 2 * LANES), F32)],
        compiler_params=_params(("parallel", "arbitrary")),
        name="decode_main",
    )(pt_flat, qn_dec, qf_dec, sel, rowc, o_c, gates, ftot, newn, newf, newlf, wbuf, wnew,
      *([nsa_pool] * pp), *([fox_pool] * pp), *([lf_pool] * pp))


def _outproj_kernel(x_ref, mod_ref, g_ref, on_ref, of_ref, wm_ref, wa_ref, wb_ref, wo_ref, o_ref):
    x = x_ref[...]
    nb, nt, d = x.shape
    h = _pre(x, mod_ref, g_ref, 1).reshape(nb * nt, d).astype(BF16)
    gm = jax.nn.sigmoid(_nn(h, wm_ref[...]))
    y = gm[:, :d] * _nn(on_ref[...], wa_ref[...]) + gm[:, d:] * _nn(of_ref[...], wb_ref[...])
    mix = _nn(y.astype(BF16), wo_ref[...]).reshape(nb, nt, d)
    o_ref[...] = x + mod_ref[:, 5:6, :] * mix


def _outproj_call(x3, mod, g_norm, o_n, o_f, wm, wa, wb, wo, blk):
    bs, lt, d = x3.shape
    nb, nt = blk
    rows = nb * nt
    nj = lt // nt
    tok = pl.BlockSpec((rows, o_n.shape[1]), lambda i, j: (i * nj + j, 0))
    return pl.pallas_call(
        _outproj_kernel,
        grid=(bs // nb, nj),
        in_specs=[pl.BlockSpec((nb, nt, d), lambda i, j: (i, j, 0)),
                  pl.BlockSpec((nb, 3 * N_SUB, d), lambda i, j: (i, 0, 0)),
                  _const_spec(g_norm.shape), tok, tok,
                  _const_spec(wm.shape), _const_spec(wa.shape), _const_spec(wb.shape), _const_spec(wo.shape)],
        out_specs=pl.BlockSpec((nb, nt, d), lambda i, j: (i, j, 0)),
        out_shape=jax.ShapeDtypeStruct(x3.shape, F32),
        compiler_params=_params(("parallel", "parallel")),
        name="out_proj",
    )(x3, mod, g_norm, o_n, o_f, wm, wa, wb, wo)


def _pair_major_cols(w):
    d = w.shape[0]
    return w.reshape(d, NSA_HEADS, HEAD_DIM)[:, jnp.array(PAIR_PERM), :].reshape(d, NSA_HEADS * HEAD_DIM)


def _pair_major_rows(w):
    d = w.shape[1]
    return w.reshape(NSA_HEADS, HEAD_DIM, d)[jnp.array(PAIR_PERM)].reshape(NSA_HEADS * HEAD_DIM, d)


def _prep_weights(w_in, b_fgt, g_q_nsa, g_k_nsa, g_q_fox, g_k_fox, w_cmp):
    d = w_in.shape[0]
    hq = NSA_HEADS * HEAD_DIM
    o = 0
    w_qn = w_in[:, o:o + hq]; o += hq
    w_kvn = w_in[:, o:o + 3 * 2 * LANES]; o += 3 * 2 * LANES
    w_gt = w_in[:, o:o + 3 * NSA_HEADS]; o += 3 * NSA_HEADS
    w_qf = w_in[:, o:o + hq]; o += hq
    w_kvf = w_in[:, o:o + 2 * LANES]; o += 2 * LANES
    w_ff = w_in[:, o:o + FOX_HEADS]; o += FOX_HEADS
    w_merge = w_in[:, o:]
    k_n = [w_kvn[:, br * 256:br * 256 + LANES] for br in range(3)]
    v_n = [w_kvn[:, br * 256 + LANES:(br + 1) * 256] for br in range(3)]
    small = jnp.concatenate([w_ff, w_gt, jnp.zeros((d, LANES - FOX_HEADS - 3 * NSA_HEADS), w_in.dtype)], axis=1)
    w1 = jnp.concatenate([_pair_major_cols(w_qn), _pair_major_cols(w_qf), k_n[0], k_n[1], k_n[2], w_kvf[:, :LANES],
                          v_n[0], v_n[1], v_n[2], w_kvf[:, LANES:], small], axis=1).astype(BF16)
    scale = HEAD_DIM ** -0.5
    gains = jnp.concatenate([jnp.tile(g_q_nsa * scale, NSA_HEADS), jnp.tile(g_q_fox * scale, FOX_HEADS),
                             jnp.tile(g_k_nsa[0], 2), jnp.tile(g_k_nsa[1], 2), jnp.tile(g_k_nsa[2], 2),
                             jnp.tile(g_k_fox, 2)]).reshape(1, N_NORMED * LANES)
    bf_pad = jnp.concatenate([b_fgt, jnp.zeros((LANES - FOX_HEADS,), F32)]).reshape(1, LANES)
    lane = jnp.arange(LANES)
    seg = (lane[:, None] // HEAD_DIM == lane[None, :] // HEAD_DIM).astype(BF16)
    wc = jnp.concatenate([w_cmp[0], w_cmp[0], w_cmp[1], w_cmp[1]], axis=1)
    return w1, gains, bf_pad, seg, wc, w_merge.astype(BF16)


def _position_lanes(l):
    pos = jnp.arange(l)
    lane = jnp.arange(LANES)[None, :]
    blk = (pos // CMP_BLOCK)[:, None]
    kx = jnp.where(lane < CMP_BLOCK, (lane == blk).astype(F32),
                   jnp.where((lane == XL_ONE_A) | (lane == XL_ONE_B), 1.0,
                             jnp.where(lane == XL_HI, blk.astype(F32),
                                       jnp.where(lane == XL_LO, (pos % CMP_BLOCK)[:, None].astype(F32), 0.0))))
    n = jnp.arange(LANES)[:, None]
    cx = jnp.where((lane == XL_ONE_A) | (lane == XL_ONE_B), 1.0,
                   jnp.where(lane == XL_HI, n.astype(F32), jnp.where(lane == XL_LO, 0.5 * (CMP_BLOCK - 1), 0.0)))
    return kx.astype(BF16), cx.astype(BF16)


def _to_decode_rows(q, b, nq):
    q5 = q.reshape(b, nq, HPG, 2, HEAD_DIM)
    eye = jnp.eye(2, dtype=q.dtype)
    return jnp.einsum('btjgd,gk->bgjtkd', q5, eye).reshape(b, NSA_HEADS * nq, LANES)


def _from_decode_rows(o, b, nq):
    o6 = o.reshape(b, 2, HPG, nq, 2, HEAD_DIM)
    own = jnp.stack([o6[:, 0, :, :, 0], o6[:, 1, :, :, 1]], axis=1)
    return own.transpose(0, 3, 1, 2, 4).reshape(b * nq, NSA_HEADS * HEAD_DIM)


def _pad_rows(x, n):
    return jnp.pad(x, ((0, 0), (0, n - x.shape[1]), (0, 0)))


def kernel(x_prompt, x_sample, cache_nsa_kv, cache_fox_kv, cache_fox_logf, state_win_kv, page_table,
           c_prompt, c_sample, w_ada, b_ada, g_norm, w_ffn_up, w_ffn_down, w_in, b_fgt,
           g_q_nsa, g_k_nsa, g_q_fox, g_k_fox, w_cmp, w_br_nsa, w_br_fox, w_out):
    assert w_ada.shape[0] == 1, "single-layer trunk"
    bp, lp, d = x_prompt.shape
    bs, nq, _ = x_sample.shape
    n_pool, page = cache_nsa_kv.shape[1:3]
    n_pages = page_table.shape[1]
    past_len = n_pages * page
    nbuf = state_win_kv.shape[2]
    dff = w_ffn_down.shape[2]
    assert nq % 8 == 0 and nq & (nq - 1) == 0 and nq <= CMP_BLOCK and past_len // CMP_BLOCK >= SLC_TOPK
    assert nbuf == WINDOW and lp % 512 == 0

    mod = _mod_call(jnp.concatenate([c_prompt, c_sample], axis=0), w_ada[0], b_ada[0]).reshape(bp + bs, 3 * N_SUB, d)
    mod_p, mod_s = mod[:bp], mod[bp:]
    gn = g_norm[0]
    up = w_ffn_up[0].astype(BF16)
    ffn_w = [(up[i, :, :dff], up[i, :, dff:], w_ffn_down[0, i].astype(BF16)) for i in range(2)]
    w1, gains, bf_pad, seg, wc, w_merge = _prep_weights(
        w_in[0], b_fgt[0], g_q_nsa[0], g_k_nsa[0], g_q_fox[0], g_k_fox[0], w_cmp[0])
    wo = w_out[0].astype(BF16)
    wa_nat, wb_nat = w_br_nsa[0].astype(BF16), w_br_fox[0].astype(BF16)
    wa_pm, wb_pm = _pair_major_rows(wa_nat), _pair_major_rows(wb_nat)

    pblk = (1, 512)
    tq, tk = 256, 256
    x1 = _ffn_call(x_prompt, mod_p, gn, *ffn_w[0], 0, pblk)
    qn, qf, nsa_new, win_new, fox_new, small, kvb, cmpb = _inproj_call(
        x1, mod_p, gn, w1, gains, bf_pad, seg, wc, pblk, True)
    small3 = small.reshape(bp, lp, LANES)
    logf_p = small3[:, :, :FOX_HEADS]
    kv3 = kvb.reshape(bp, lp, 768)
    kaug_f, qx_f = _cumsum_call(small3, kv3, tq)

    ones_rows = (jnp.arange(VT_ROWS - LANES) == 0).astype(BF16)[None, None, :, None]

    def dims_major(c):
        vt = kv3[:, :, c * LANES:(c + 1) * LANES].reshape(bp, lp // tk, tk, LANES).transpose(0, 1, 3, 2)
        return jnp.concatenate([vt, jnp.broadcast_to(ones_rows, (bp, lp // tk, VT_ROWS - LANES, tk))], axis=2)

    kx, cx = _position_lanes(lp)
    o_n = _nsa_prompt_call(qn.reshape(bp, lp, 512), kv3, kx, dims_major(1), dims_major(3),
                           cmpb.reshape(bp, lp // CMP_BLOCK, 256), cx, small3, tq, tk)
    o_f = _fox_prompt_call(qf.reshape(bp, lp, 512), qx_f, kaug_f, dims_major(5), tq, tk)
    x2 = _outproj_call(x1, mod_p, gn, o_n.reshape(bp * lp, 512), o_f.reshape(bp * lp, 512),
                       w_merge, wa_pm, wb_pm, wo, pblk)
    y_prompt = _ffn_call(x2, mod_p, gn, *ffn_w[1], 2, pblk)
    nsa_p = nsa_new.reshape(1, bp, lp, 4, NSA_KV_HEADS, HEAD_DIM)
    fox_p = fox_new.reshape(1, bp, lp, 2, FOX_KV_HEADS, HEAD_DIM)
    win_p = win_new.reshape(bp, lp, 2, NSA_KV_HEADS, HEAD_DIM)[None, :, lp - min(WINDOW, lp):]

    sblk = (min(bs, 64), nq)
    pp = 8 if n_pages % 8 == 0 else 1
    x1s = _ffn_call(x_sample, mod_s, gn, *ffn_w[0], 0, sblk)
    qn_s, qf_s, nsa_s, win_s, fox_s, small_s, _ = _inproj_call(
        x1s, mod_s, gn, w1, gains, bf_pad, seg, wc, sblk, False)
    nrow = NSA_HEADS * nq
    qn_dec = _to_decode_rows(qn_s, bs, nq)
    qf_dec = _to_decode_rows(qf_s, bs, nq)
    rows = jnp.arange(nrow)
    rowc = jnp.zeros((nrow, LANES), F32).at[:, 0].set(jnp.array(SLOPES, F32)[rows // nq]).at[:, 1].set(
        (rows % nq).astype(F32))
    sm3 = small_s.reshape(bs, nq, LANES)
    gates = sm3[:, :, FOX_HEADS:FOX_HEADS + 3 * NSA_HEADS].reshape(bs, nq, 3, NSA_HEADS)
    gates = jnp.pad(gates.transpose(0, 3, 1, 2).reshape(bs, nrow, 3), ((0, 0), (0, 0), (0, LANES - 3)))
    newlf = _pad_rows(sm3[:, :, :FOX_HEADS], LANES).transpose(0, 2, 1)
    newn = _pad_rows(nsa_s.reshape(bs, nq, 512)[:, :, 256:], LANES)
    newf = _pad_rows(fox_s.reshape(bs, nq, 256), LANES)
    wnew = _pad_rows(win_s.reshape(bs, nq, 256), LANES)
    nsa_pool = cache_nsa_kv[0].reshape(n_pool, page, 512)
    fox_pool = cache_fox_kv[0].reshape(n_pool, page, 256)
    lf_pool = cache_fox_logf[0].astype(F32).transpose(0, 2, 1)
    wbuf = state_win_kv[0].reshape(bs, nbuf, 256)
    pt_flat = page_table.reshape(-1).astype(jnp.int32)
    o_c, sel, ftot = _dec_cmp_call(pt_flat, qn_dec, rowc, wc, nsa_pool, lf_pool, pp, n_pages, past_len)
    on_dec, of_dec, wst = _dec_main_call(pt_flat, qn_dec, qf_dec, sel, rowc, o_c, gates, ftot, newn, newf, newlf,
                                         wbuf, wnew, nsa_pool, fox_pool, lf_pool, pp, n_pages, past_len, nq)
    o_ns = _from_decode_rows(on_dec, bs, nq).astype(BF16)
    o_fs = _from_decode_rows(of_dec, bs, nq).astype(BF16)
    x2s = _outproj_call(x1s, mod_s, gn, o_ns, o_fs, w_merge, wa_nat, wb_nat, wo, sblk)
    y_sample = _ffn_call(x2s, mod_s, gn, *ffn_w[1], 2, sblk)
    nsa_o = nsa_s.reshape(1, bs, nq, 4, NSA_KV_HEADS, HEAD_DIM)
    fox_o = fox_s.reshape(1, bs, nq, 2, FOX_KV_HEADS, HEAD_DIM)
    win_o = wst.reshape(1, bs, nbuf, 2, NSA_KV_HEADS, HEAD_DIM)

    return (y_prompt, y_sample, nsa_p, nsa_o, fox_p, fox_o,
            logf_p[None], sm3[None, :, :, :FOX_HEADS], win_p, win_o)
```

```python
import functools

import jax
import jax.numpy as jnp
from jax import lax
from jax.experimental import pallas as pl
from jax.experimental.pallas import tpu as pltpu

HEAD_DIM = 64
NSA_HEADS = 8
NSA_KV_HEADS = 2
FOX_HEADS = 8
FOX_KV_HEADS = 2
HPG = 4
CMP_BLOCK = 64
CMP_SHIFT = 6
SLC_TOPK = 8
WINDOW = 512
N_SUB = 3
RMS_EPS = 1e-6
NEG_INF = -1e30
FORCE_BONUS = 1e4
LANES = 128
PAIR_PERM = (0, 4, 1, 5, 2, 6, 3, 7)
SLOPES = tuple(2.0 ** -(h + 1) for h in range(NSA_HEADS))
VMEM_LIMIT = 56 * 1024 * 1024

F32 = jnp.float32
BF16 = jnp.bfloat16


def _nt(a, b):
    return lax.dot_general(a, b, (((1,), (1,)), ((), ())), preferred_element_type=F32)


def _nn(a, b):
    return jnp.dot(a, b, preferred_element_type=F32)


def _split3(x):
    hi = x.astype(BF16)
    r1 = x - hi.astype(F32)
    mid = r1.astype(BF16)
    lo = (r1 - mid.astype(F32)).astype(BF16)
    return hi, mid, lo


def _params(sem):
    return pltpu.CompilerParams(dimension_semantics=sem, vmem_limit_bytes=VMEM_LIMIT)


def _const_spec(shape):
    nd = len(shape)
    return pl.BlockSpec(shape, lambda *_: (0,) * nd, pipeline_mode=pl.Buffered(1))


def _mod_kernel(c_ref, w_ref, b_ref, o_ref):
    c = c_ref[...]
    a = (c * jax.nn.sigmoid(c)).astype(BF16)
    o_ref[...] = _nn(a, w_ref[...].astype(BF16)) + b_ref[...]


def _mod_call(c_all, w_ada, b_ada):
    bc, d = c_all.shape
    n = w_ada.shape[1]
    tn = 1024 if n % 1024 == 0 else n
    return pl.pallas_call(
        _mod_kernel,
        grid=(n // tn,),
        in_specs=[pl.BlockSpec((bc, d), lambda i: (0, 0)),
                  pl.BlockSpec((d, tn), lambda i: (0, i)),
                  pl.BlockSpec((1, tn), lambda i: (0, i))],
        out_specs=pl.BlockSpec((bc, tn), lambda i: (0, i)),
        out_shape=jax.ShapeDtypeStruct((bc, n), F32),
        compiler_params=_params(("parallel",)),
        name="adaln_mod",
    )(c_all, w_ada, b_ada.reshape(1, n))


def _pre(x, mod_ref, g_ref, sub):
    ms = jnp.mean(x * x, axis=-1, keepdims=True)
    y = x * lax.rsqrt(ms + RMS_EPS) * g_ref[sub:sub + 1, :]
    return y * (1.0 + mod_ref[:, 3 * sub + 1:3 * sub + 2, :]) + mod_ref[:, 3 * sub:3 * sub + 1, :]


def _ffn_kernel(x_ref, mod_ref, g_ref, wa_ref, wb_ref, wd_ref, o_ref, act_ref, *, sub, fc):
    x = x_ref[...]
    nb, nt, d = x.shape
    h = _pre(x, mod_ref, g_ref, sub).reshape(nb * nt, d).astype(BF16)
    dff = wa_ref.shape[1]
    for c in range(dff // fc):
        a = _nn(h, wa_ref[:, c * fc:(c + 1) * fc])
        b = _nn(h, wb_ref[:, c * fc:(c + 1) * fc])
        act_ref[:, c * fc:(c + 1) * fc] = (a * jax.nn.sigmoid(a) * b).astype(BF16)
    y = _nn(act_ref[...], wd_ref[...]).reshape(nb, nt, d)
    o_ref[...] = x + 0.5 * mod_ref[:, 3 * sub + 2:3 * sub + 3, :] * y


def _ffn_call(x3, mod, g_norm, wa, wb, wd, sub, blk):
    bs, lt, d = x3.shape
    nb, nt = blk
    dff = wa.shape[1]
    fc = 256 if dff % 256 == 0 else dff
    grid = (bs // nb, lt // nt)
    return pl.pallas_call(
        functools.partial(_ffn_kernel, sub=sub, fc=fc),
        grid=grid,
        in_specs=[pl.BlockSpec((nb, nt, d), lambda i, j: (i, j, 0)),
                  pl.BlockSpec((nb, 3 * N_SUB, d), lambda i, j: (i, 0, 0)),
                  _const_spec(g_norm.shape), _const_spec(wa.shape), _const_spec(wb.shape),
                  _const_spec(wd.shape)],
        out_specs=pl.BlockSpec((nb, nt, d), lambda i, j: (i, j, 0)),
        out_shape=jax.ShapeDtypeStruct(x3.shape, F32),
        scratch_shapes=[pltpu.VMEM((nb * nt, dff), BF16)],
        compiler_params=_params(("parallel", "parallel")),
        name="ffn_sub%d" % sub,
    )(x3, mod, g_norm, wa, wb, wd)


N_NORMED = 12


def _inproj_kernel(x_ref, mod_ref, g_ref, w_ref, gain_ref, bf_ref, seg_ref, *rest, with_cmp):
    if with_cmp:
        wc_ref, qn_ref, qf_ref, nsa_ref, win_ref, fox_ref, sm_ref, kv_ref, cmp_ref = rest
    else:
        qn_ref, qf_ref, nsa_ref, win_ref, fox_ref, sm_ref, kv_ref = rest
    x = x_ref[...]
    nb, nt, d = x.shape
    h = _pre(x, mod_ref, g_ref, 1).reshape(nb * nt, d).astype(BF16)
    z = _nn(h, w_ref[...])
    seg = seg_ref[...]
    normed = []
    for c in range(N_NORMED):
        zc = z[:, c * LANES:(c + 1) * LANES]
        zz = zc * zc
        hi = zz.astype(BF16)
        lo = (zz - hi.astype(F32)).astype(BF16)
        ss = _nn(hi, seg) + _nn(lo, seg)
        normed.append(zc * lax.rsqrt(ss * (1.0 / HEAD_DIM) + RMS_EPS) * gain_ref[:, c * LANES:(c + 1) * LANES])
    raw = [z[:, (N_NORMED + c) * LANES:(N_NORMED + c + 1) * LANES] for c in range(4)]
    k_cmp, k_slc, k_win, k_fox = normed[8:12]
    v_cmp, v_slc, v_win, v_fox = raw
    for j in range(4):
        qn_ref[:, j * LANES:(j + 1) * LANES] = normed[j].astype(BF16)
        qf_ref[:, j * LANES:(j + 1) * LANES] = normed[4 + j].astype(BF16)
    def put(ref, c, t):
        if with_cmp:
            ref[c * LANES:(c + 1) * LANES, :] = t.T
        else:
            ref[:, c * LANES:(c + 1) * LANES] = t

    for c, t in enumerate((k_cmp, v_cmp, k_slc, v_slc)):
        put(nsa_ref, c, t)
    put(win_ref, 0, k_win)
    put(win_ref, 1, v_win)
    put(fox_ref, 0, k_fox)
    put(fox_ref, 1, v_fox)
    for c, t in enumerate((k_slc, v_slc, k_win, v_win, k_fox, v_fox)):
        kv_ref[:, c * LANES:(c + 1) * LANES] = t.astype(BF16)
    zs = z[:, 16 * LANES:17 * LANES]
    lane = lax.broadcasted_iota(jnp.int32, zs.shape, 1)
    zf = zs + bf_ref[...]
    logf = jnp.minimum(zf, 0.0) - jnp.log1p(jnp.exp(-jnp.abs(zf)))
    sm_ref[...] = jnp.where(lane < FOX_HEADS, logf, jax.nn.sigmoid(zs))
    if with_cmp:
        rows = nb * nt
        kvc = jnp.concatenate([k_cmp, v_cmp], axis=1).reshape(rows // CMP_BLOCK, CMP_BLOCK, 2 * LANES)
        cmp_ref[...] = jnp.sum(kvc * wc_ref[...][None], axis=1)


def _inproj_call(x3, mod, g_norm, w1, gains, bf_pad, seg, wc, blk, with_cmp):
    bs, lt, d = x3.shape
    nb, nt = blk
    rows = nb * nt
    t = bs * lt
    grid = (bs // nb, lt // nt)
    nj = lt // nt

    def tok(cols):
        return pl.BlockSpec((rows, cols), lambda i, j: (i * nj + j, 0))

    in_specs = [pl.BlockSpec((nb, nt, d), lambda i, j: (i, j, 0)),
                pl.BlockSpec((nb, 3 * N_SUB, d), lambda i, j: (i, 0, 0)),
                _const_spec(g_norm.shape), _const_spec(w1.shape), _const_spec(gains.shape),
                _const_spec(bf_pad.shape), _const_spec(seg.shape)]
    args = [x3, mod, g_norm, w1, gains, bf_pad, seg]
    out_specs = [tok(512), tok(512), tok(512), tok(256), tok(256), tok(128), tok(768)]
    out_shape = [jax.ShapeDtypeStruct((t, 512), BF16), jax.ShapeDtypeStruct((t, 512), BF16),
                 jax.ShapeDtypeStruct((t, 512), F32), jax.ShapeDtypeStruct((t, 256), F32),
                 jax.ShapeDtypeStruct((t, 256), F32), jax.ShapeDtypeStruct((t, 128), F32),
                 jax.ShapeDtypeStruct((t, 768), BF16)]
    if with_cmp:
        assert nb == 1
        for k, cols in ((2, 512), (3, 256), (4, 256)):
            out_specs[k] = pl.BlockSpec((None, cols, nt), lambda i, j: (i, 0, j))
            out_shape[k] = jax.ShapeDtypeStruct((bs, cols, lt), F32)
        in_specs.append(_const_spec(wc.shape))
        args.append(wc)
        out_specs.append(pl.BlockSpec((rows // CMP_BLOCK, 256), lambda i, j: (i * nj + j, 0)))
        out_shape.append(jax.ShapeDtypeStruct((t // CMP_BLOCK, 256), F32))
    return pl.pallas_call(
        functools.partial(_inproj_kernel, with_cmp=with_cmp),
        grid=grid, in_specs=in_specs, out_specs=out_specs, out_shape=out_shape,
        compiler_params=_params(("parallel", "parallel")),
        name="in_proj",
    )(*args)


FQ0 = 0
FK0 = 3 * FOX_HEADS


def _cumsum_kernel(tok_ref, k_ref, kaug_ref, qx_ref, carry_ref):
    @pl.when(pl.program_id(1) == 0)
    def _():
        carry_ref[...] = jnp.zeros_like(carry_ref)

    tc = tok_ref.shape[0]
    r = lax.broadcasted_iota(jnp.int32, (tc, tc), 0)
    c = lax.broadcasted_iota(jnp.int32, (tc, tc), 1)
    lower = jnp.where(r >= c, 1.0, 0.0).astype(BF16)
    cs = carry_ref[...] + sum(_nn(lower, p) for p in _split3(tok_ref[...]))
    carry_ref[...] = cs[tc - 1:tc, :]
    hrow = lax.broadcasted_iota(jnp.int32, (LANES, LANES), 0)
    col = lax.broadcasted_iota(jnp.int32, (LANES, LANES), 1)
    placed = jnp.zeros((tc, LANES), F32)
    for p, piece in enumerate(_split3(cs)):
        tgt = 3 * hrow + p
        place = jnp.where((hrow < FOX_HEADS) & ((col == FQ0 + tgt) | (col == FK0 + tgt)), 1.0, 0.0).astype(BF16)
        placed = placed + _nn(piece, place)
    lane = lax.broadcasted_iota(jnp.int32, (tc, LANES), 1)
    in_q = lane < FK0
    in_k = (lane >= FK0) & (lane < 2 * FK0)
    qx_ref[...] = jnp.where(in_q, placed, jnp.where(in_k, 1.0, 0.0)).astype(BF16)
    kaug_ref[:, :LANES] = k_ref[...]
    kaug_ref[:, LANES:] = jnp.where(in_q, 1.0, jnp.where(in_k, -placed, 0.0)).astype(BF16)


def _cumsum_call(small3, kv3, tc):
    b, l, _ = small3.shape
    return pl.pallas_call(
        _cumsum_kernel,
        grid=(b, l // tc),
        in_specs=[pl.BlockSpec((None, tc, LANES), lambda i, j: (i, j, 0)),
                  pl.BlockSpec((None, tc, LANES), lambda i, j: (i, j, 4))],
        out_specs=[pl.BlockSpec((None, tc, 2 * LANES), lambda i, j: (i, j, 0)),
                   pl.BlockSpec((None, tc, LANES), lambda i, j: (i, j, 0))],
        out_shape=[jax.ShapeDtypeStruct((b, l, 2 * LANES), BF16),
                   jax.ShapeDtypeStruct((b, l, LANES), BF16)],
        scratch_shapes=[pltpu.VMEM((1, LANES), F32)],
        compiler_params=_params(("parallel", "arbitrary")),
        name="fox_cumsum",
    )(small3, kv3)


def _flash_step(s, v, m, l, acc):
    m_new = jnp.maximum(m, jnp.max(s, axis=-1, keepdims=True))
    alpha = jnp.exp(m - m_new)
    p = jnp.exp(s - m_new)
    l = alpha * l + jnp.sum(p, axis=-1, keepdims=True)
    acc = alpha * acc + _nn(p.astype(BF16), v)
    return m_new, l, acc


def _flash_init(rows):
    return (jnp.full((rows, 1), NEG_INF, F32), jnp.zeros((rows, 1), F32), jnp.zeros((rows, LANES), F32))


def _group_masked(qp, g):
    lane = lax.broadcasted_iota(jnp.int32, qp.shape, 1)
    keep = (lane < HEAD_DIM) if g == 0 else (lane >= HEAD_DIM)
    return jnp.where(keep, qp, jnp.zeros_like(qp))


def _top_blocks(score, n_pick, axis=1):
    idx = lax.broadcasted_iota(jnp.int32, score.shape, axis).astype(F32)
    sel = jnp.zeros(score.shape, F32)
    for _ in range(n_pick):
        mx = jnp.max(score, axis=axis, keepdims=True)
        first = jnp.min(jnp.where(score == mx, idx, 1e9), axis=axis, keepdims=True)
        pick = idx == first
        sel = jnp.where(pick, 1.0, sel)
        score = jnp.where(pick, -jnp.inf, score)
    return sel


XL_ONE_A = CMP_BLOCK
XL_ONE_B = CMP_BLOCK + 1
XL_HI = CMP_BLOCK + 2
XL_LO = CMP_BLOCK + 3
MASK_BIG = 1e30


VT_ROWS = LANES + 8


def _col_max(s):
    k = s.shape[0]
    if k % 64 == 0:
        s = jnp.max(s.reshape(8, k // 8, s.shape[1]), axis=0)
    return jnp.max(s, axis=0, keepdims=True)


def _heads_t_step(k, vt, qa_ref, m_ref, acc_ref, ok):
    nh = qa_ref.shape[0]
    scores = []
    for h in range(nh):
        s = _nt(k, qa_ref[h])
        scores.append(s if ok is None else jnp.where(ok, s, NEG_INF))
    probs, alphas = [], []
    for h in range(nh):
        m_old = m_ref[h:h + 1, :]
        m_new = jnp.maximum(m_old, _col_max(scores[h]))
        m_ref[h:h + 1, :] = m_new
        probs.append(jnp.exp(scores[h] - m_new).astype(BF16))
        alphas.append(jnp.exp(m_old - m_new))
    for h in range(nh):
        acc_ref[h] = alphas[h] * acc_ref[h] + _nn(vt, probs[h])


def _attn_scratch(tq):
    return [pltpu.VMEM((NSA_HEADS, tq, 2 * LANES), BF16), pltpu.VMEM((NSA_HEADS, tq, 2 * LANES), BF16),
            pltpu.VMEM((NSA_HEADS, tq), F32), pltpu.VMEM((NSA_HEADS, VT_ROWS, tq), F32),
            pltpu.VMEM((NSA_HEADS, LANES, tq), F32)]


def _finish_t(acc):
    return acc[:LANES, :] / acc[LANES:LANES + 1, :]


def _nsa_prompt_kernel(q_ref, ks_ref, kw_ref, kx_ref, vst_ref, vwt_ref, cmp_ref, cx_ref, sm_ref, o_ref,
                       qas_ref, qaw_ref, m_ref, acc_ref, out_ref, *, tq, tk, nb):
    qi = pl.program_id(1)
    lane = lax.broadcasted_iota(jnp.int32, (tq, LANES), 1)
    t2 = qi * tq + lax.broadcasted_iota(jnp.int32, (tq, LANES), 0)
    t_hi = lax.shift_right_logical(t2, CMP_SHIFT).astype(F32)
    t_lo = (t2 & (CMP_BLOCK - 1)).astype(F32)
    alibi_q = jnp.where(lane == XL_ONE_A, -CMP_BLOCK * t_hi,
                        jnp.where(lane == XL_ONE_B, -t_lo,
                                  jnp.where(lane == XL_HI, float(CMP_BLOCK), jnp.where(lane == XL_LO, 1.0, 0.0))))
    krow = lax.broadcasted_iota(jnp.int32, (tk, tq), 0)
    qcol = lax.broadcasted_iota(jnp.int32, (tk, tq), 1)
    rows = lax.broadcasted_iota(jnp.int32, (LANES, tq), 0)

    cmpv = cmp_ref[...]
    pad = jnp.zeros((LANES - nb, LANES), F32)
    ck = jnp.concatenate([cmpv[:, :LANES], pad], axis=0).astype(BF16)
    cvt = jnp.concatenate([cmpv[:, LANES:], pad], axis=0).T.astype(BF16)
    ck_aug = jnp.concatenate([ck, cx_ref[...]], axis=1)
    tcol = qi * tq + lax.broadcasted_iota(jnp.int32, (LANES, tq), 1)
    valid_c = ((rows + 1) * CMP_BLOCK - 1 <= tcol) & (rows < nb)
    cur = lax.shift_right_logical(tcol, CMP_SHIFT)
    forced = (rows == 0) | (rows == cur) | (rows == cur - 1)
    started = rows * CMP_BLOCK <= tcol
    smt = sm_ref[...].T

    for g in range(NSA_KV_HEADS):
        imp = jnp.zeros((LANES, tq), F32)
        qx = []
        for j in range(HPG):
            h = g * HPG + j
            qx.append(SLOPES[h] * alibi_q)
            qaw_ref[h, :, :LANES] = _group_masked(q_ref[:, j * LANES:(j + 1) * LANES], g)
            qaw_ref[h, :, LANES:] = qx[j].astype(BF16)
            s = jnp.where(valid_c, _nt(ck_aug, qaw_ref[h]), NEG_INF)
            e = jnp.exp(s - jnp.max(s, axis=0, keepdims=True))
            p = jnp.where(valid_c, e / jnp.sum(e, axis=0, keepdims=True), 0.0)
            imp = imp + p
            c0 = FOX_HEADS + h
            out_ref[h] = smt[c0:c0 + 1, :] * _nn(cvt, p.astype(BF16))
        score = jnp.where(started, imp + jnp.where(forced, FORCE_BONUS, 0.0), NEG_INF)
        score = jnp.where(rows < nb, score, 2 * NEG_INF)
        sel = _top_blocks(score, min(SLC_TOPK, nb), axis=0).T
        sel_q = jnp.where(lane < CMP_BLOCK, (sel - 1.0) * MASK_BIG, 0.0)
        for j in range(HPG):
            h = g * HPG + j
            qas_ref[h, :, :LANES] = qaw_ref[h, :, :LANES]
            qas_ref[h, :, LANES:] = (qx[j] + sel_q).astype(BF16)

    kpq = tq // tk

    def run_branch(qa_ref, k_ref, vt_ref, lo, window, gate_row):
        m_ref[...] = jnp.full(m_ref.shape, NEG_INF, F32)
        acc_ref[...] = jnp.zeros_like(acc_ref)

        def tile(kj, masked):
            off = pl.multiple_of(kj * tk, tk)
            ka = jnp.concatenate([k_ref[pl.ds(off, tk), :], kx_ref[pl.ds(off, tk), :]], axis=1)
            vt = vt_ref[kj]
            if masked:
                dist = qi * tq - kj * tk + qcol - krow
                ok = dist >= 0
                if window:
                    ok = ok & (dist < WINDOW)
            _heads_t_step(ka, vt, qa_ref, m_ref, acc_ref, ok if masked else None)

        def body(kj, c):
            tile(kj, window)
            return c

        if window:
            lax.fori_loop(lo, (qi + 1) * kpq, body, 0)
        else:
            lax.fori_loop(lo, qi * kpq, body, 0)
            for d in range(kpq):
                tile(qi * kpq + d, True)
        for h in range(NSA_HEADS):
            r = gate_row + h
            out_ref[h] = out_ref[h] + smt[r:r + 1, :] * _finish_t(acc_ref[h])

    run_branch(qas_ref, ks_ref, vst_ref, 0, False, FOX_HEADS + NSA_HEADS)
    run_branch(qaw_ref, kw_ref, vwt_ref, jnp.maximum(qi * kpq - (WINDOW + tk - 1) // tk, 0), True,
               FOX_HEADS + 2 * NSA_HEADS)
    for j in range(HPG):
        pair_t = jnp.where(rows < HEAD_DIM, out_ref[j], out_ref[HPG + j])
        o_ref[:, j * LANES:(j + 1) * LANES] = pair_t.T.astype(BF16)


def _nsa_prompt_call(qn3, kv3, kx, vst, vwt, cmp3, cx, small3, tq, tk):
    b, l, _ = qn3.shape
    nb = cmp3.shape[1]
    assert nb <= CMP_BLOCK, "one-hot block lanes cover at most CMP_BLOCK blocks"

    def kspec(c):
        return pl.BlockSpec((None, l, LANES), lambda i, j: (i, 0, c))

    vspec = pl.BlockSpec((None, l // tk, VT_ROWS, tk), lambda i, j: (i, 0, 0, 0))
    return pl.pallas_call(
        functools.partial(_nsa_prompt_kernel, tq=tq, tk=tk, nb=nb),
        grid=(b, l // tq),
        in_specs=[pl.BlockSpec((None, tq, 512), lambda i, j: (i, j, 0)),
                  kspec(0), kspec(2), pl.BlockSpec((l, LANES), lambda i, j: (0, 0)), vspec, vspec,
                  pl.BlockSpec((None, nb, 256), lambda i, j: (i, 0, 0)),
                  pl.BlockSpec((LANES, LANES), lambda i, j: (0, 0)),
                  pl.BlockSpec((None, tq, LANES), lambda i, j: (i, j, 0))],
        out_specs=pl.BlockSpec((None, tq, 512), lambda i, j: (i, j, 0)),
        out_shape=jax.ShapeDtypeStruct((b, l, 512), BF16),
        scratch_shapes=_attn_scratch(tq),
        compiler_params=_params(("parallel", "arbitrary")),
        name="nsa_prompt",
    )(qn3, kv3, kv3, kx, vst, vwt, cmp3, cx, small3)


def _fox_prompt_kernel(q_ref, qx_ref, k_ref, vt_ref, o_ref, qa_ref, m_ref, acc_ref, *, tq, tk):
    qi = pl.program_id(1)
    lane = lax.broadcasted_iota(jnp.int32, (tq, LANES), 1)
    krow = lax.broadcasted_iota(jnp.int32, (tk, tq), 0)
    qcol = lax.broadcasted_iota(jnp.int32, (tk, tq), 1)
    rows = lax.broadcasted_iota(jnp.int32, (LANES, tq), 0)
    qx = qx_ref[...]
    for g in range(FOX_KV_HEADS):
        for j in range(HPG):
            h = g * HPG + j
            own = ((lane >= FQ0 + 3 * h) & (lane < FQ0 + 3 * h + 3)) | ((lane >= FK0 + 3 * h) & (lane < FK0 + 3 * h + 3))
            qa_ref[h, :, :LANES] = _group_masked(q_ref[:, j * LANES:(j + 1) * LANES], g)
            qa_ref[h, :, LANES:] = jnp.where(own, qx, jnp.zeros_like(qx))
    m_ref[...] = jnp.full(m_ref.shape, NEG_INF, F32)
    acc_ref[...] = jnp.zeros_like(acc_ref)
    kpq = tq // tk

    def tile(kj, masked):
        k = k_ref[pl.ds(pl.multiple_of(kj * tk, tk), tk), :]
        vt = vt_ref[kj]
        if masked:
            ok = qi * tq - kj * tk + qcol - krow >= 0
        _heads_t_step(k, vt, qa_ref, m_ref, acc_ref, ok if masked else None)

    def body(kj, c):
        tile(kj, False)
        return c

    lax.fori_loop(0, qi * kpq, body, 0)
    for d in range(kpq):
        tile(qi * kpq + d, True)
    for j in range(HPG):
        pair_t = jnp.where(rows < HEAD_DIM, _finish_t(acc_ref[j]), _finish_t(acc_ref[HPG + j]))
        o_ref[:, j * LANES:(j + 1) * LANES] = pair_t.T.astype(BF16)


def _fox_prompt_call(qf3, qx3, kaug3, vft, tq, tk):
    b, l, _ = qf3.shape
    return pl.pallas_call(
        functools.partial(_fox_prompt_kernel, tq=tq, tk=tk),
        grid=(b, l // tq),
        in_specs=[pl.BlockSpec((None, tq, 512), lambda i, j: (i, j, 0)),
                  pl.BlockSpec((None, tq, LANES), lambda i, j: (i, j, 0)),
                  pl.BlockSpec((None, l, 2 * LANES), lambda i, j: (i, 0, 0)),
                  pl.BlockSpec((None, l // tk, VT_ROWS, tk), lambda i, j: (i, 0, 0, 0))],
        out_specs=pl.BlockSpec((None, tq, 512), lambda i, j: (i, j, 0)),
        out_shape=jax.ShapeDtypeStruct((b, l, 512), BF16),
        scratch_shapes=_attn_scratch(tq)[1:4],
        compiler_params=_params(("parallel", "arbitrary")),
        name="fox_prompt",
    )(qf3, qx3, kaug3, vft)


def _lane_cumsum(x):
    n = x.shape[1]
    r = lax.broadcasted_iota(jnp.int32, (n, n), 0)
    c = lax.broadcasted_iota(jnp.int32, (n, n), 1)
    upper = jnp.where(r <= c, 1.0, 0.0).astype(BF16)
    return sum(_nn(p, upper) for p in _split3(x))


def _dec_cmp_kernel(pt_ref, q_ref, rowc_ref, wct_ref, bsum_ref, *rest, pp, nb, past_len):
    del pt_ref
    pages = rest[:pp]
    lfs = rest[pp:2 * pp]
    oc_ref, sel_ref, ftot_ref, cmp_scr, ftot_scr = rest[2 * pp:]
    c = pl.program_id(1)

    @pl.when(c == 0)
    def _():
        ftot_scr[...] = jnp.zeros_like(ftot_scr)

    nblk = bsum_ref.shape[0]
    bsum = bsum_ref[...]
    for plane in range(2):
        w = wct_ref[plane]
        for g in range(NSA_KV_HEADS):
            prod = jnp.concatenate([p[plane, g] * w for p in pages], axis=1)
            blk = sum(_nt(bsum, piece) for piece in _split3(prod))
            cmp_scr[2 * plane + g, pl.ds(pl.multiple_of(c * nblk, nblk), nblk), :] = blk
    tot = lfs[0][...]
    for p in lfs[1:]:
        tot = tot + p[...]
    ftot_scr[...] += jnp.broadcast_to(jnp.sum(tot, axis=-1, keepdims=True), ftot_scr.shape)

    @pl.when(c == pl.num_programs(1) - 1)
    def _():
        ftot_ref[...] = ftot_scr[...]
        ckb = jnp.concatenate([cmp_scr[0], cmp_scr[1]], axis=1).astype(BF16)
        cvb = jnp.concatenate([cmp_scr[2], cmp_scr[3]], axis=1).astype(BF16)
        nrow = q_ref.shape[0]
        rowc = rowc_ref[...]
        slope = rowc[:, 0:1]
        t = past_len + rowc[:, 1:2]
        nidx = lax.broadcasted_iota(jnp.int32, (nrow, nb), 1)
        cmid = nidx.astype(F32) * CMP_BLOCK + 0.5 * (CMP_BLOCK - 1)
        valid = cmid + 0.5 * (CMP_BLOCK - 1) <= t
        s = jnp.where(valid, _nt(q_ref[...], ckb) - slope * (t - cmid), NEG_INF)
        e = jnp.exp(s - jnp.max(s, axis=-1, keepdims=True))
        p = jnp.where(valid, e / jnp.sum(e, axis=-1, keepdims=True), 0.0)
        oc_ref[...] = _nn(p.astype(BF16), cvb)
        nq = nrow // NSA_HEADS
        n8 = lax.broadcasted_iota(jnp.int32, (nq, nb), 1)
        forced = (n8 == 0) | (n8 == nb - 1)
        sel_rows = []
        for g in range(NSA_KV_HEADS):
            imp = jnp.zeros((nq, nb), F32)
            for j in range(HPG):
                r0 = (g * HPG + j) * nq
                imp = imp + p[r0:r0 + nq, :]
            sel = _top_blocks(imp + jnp.where(forced, FORCE_BONUS, 0.0), SLC_TOPK - 1)
            sel_rows += [sel] * HPG
        sel_ref[...] = ((jnp.concatenate(sel_rows, axis=0) - 1.0) * MASK_BIG).astype(BF16)


def _page_specs(pp, n_pages, block, plane_pair):
    tail = (0,) * (len(block) - 2)

    def one(i):
        return pl.BlockSpec(block, lambda b, c, pt: (pt[b * n_pages + c * pp + i], plane_pair) + tail)
    return [one(i) for i in range(pp)]


def _dec_cmp_call(pt_flat, q_dec, rowc, wct, bsum, nsa_t, lf_pool, pp, n_pages, past_len):
    b, nrow, _ = q_dec.shape
    nb = past_len // CMP_BLOCK
    page = nsa_t.shape[-1]
    nc = n_pages // pp
    assert nb <= LANES, "block mask lanes cover at most 128 cached blocks"
    per = lambda shape: pl.BlockSpec((None,) + shape, lambda i, c, pt: (i, 0, 0))
    cst = lambda shape: pl.BlockSpec(shape, lambda i, c, pt: (0,) * len(shape))
    grid_spec = pltpu.PrefetchScalarGridSpec(
        num_scalar_prefetch=1,
        grid=(b, nc),
        in_specs=[per((nrow, LANES)), cst(rowc.shape), cst(wct.shape), cst(bsum.shape)]
        + _page_specs(pp, n_pages, (None, 2, NSA_KV_HEADS, HEAD_DIM, page), 0)
        + _page_specs(pp, n_pages, (None, FOX_HEADS, page), 0),
        out_specs=[per((nrow, LANES)), per((nrow, nb)), per((FOX_HEADS, LANES))],
        scratch_shapes=[pltpu.VMEM((4, nb, HEAD_DIM), F32), pltpu.VMEM((FOX_HEADS, LANES), F32)],
    )
    return pl.pallas_call(
        functools.partial(_dec_cmp_kernel, pp=pp, nb=nb, past_len=past_len),
        grid_spec=grid_spec,
        out_shape=[jax.ShapeDtypeStruct((b, nrow, LANES), F32),
                   jax.ShapeDtypeStruct((b, nrow, nb), BF16),
                   jax.ShapeDtypeStruct((b, FOX_HEADS, LANES), F32)],
        compiler_params=_params(("parallel", "arbitrary")),
        name="decode_cmp",
    )(pt_flat, q_dec, rowc, wct, bsum, *([nsa_t] * pp), *([lf_pool] * pp))


def _rep_rows(x, n):
    return jnp.concatenate([jnp.broadcast_to(x[h:h + 1, :], (n, x.shape[1])) for h in range(x.shape[0])], axis=0)


def _scr_flash_t(scores, vts, stats):
    staged = []
    for s, (m_ref, l_ref, _) in zip(scores, stats):
        m_old = m_ref[:, 0:1]
        m = jnp.maximum(m_old, jnp.max(s, axis=-1, keepdims=True))
        alpha = jnp.exp(m_old - m)
        p = jnp.exp(s - m)
        l = alpha * l_ref[:, 0:1] + jnp.sum(p, axis=-1, keepdims=True)
        m_ref[...] = jnp.broadcast_to(m, m_ref.shape)
        l_ref[...] = jnp.broadcast_to(l, l_ref.shape)
        staged.append((alpha, p.astype(BF16)))
    for (alpha, p), vt, (_, _, acc_ref) in zip(staged, vts, stats):
        acc_ref[...] = alpha * acc_ref[...] + _nt(p, vt)


def _pages_t(pages, plane):
    return jnp.concatenate(
        [jnp.concatenate([p[plane, g] for g in range(NSA_KV_HEADS)], axis=0) for p in pages], axis=1).astype(BF16)


def _dec_main_kernel(pt_ref, qn_ref, qf_ref, sel_ref, rowc_ref, oc_ref, gate_ref, ftot_ref,
                     newn_ref, newf_ref, newlf_ref, wbuf_ref, wnew_ref, hot_ref, *rest, pp, nb, past_len, nq):
    del pt_ref
    slc = rest[:pp]
    fox = rest[pp:2 * pp]
    lfs = rest[2 * pp:3 * pp]
    (on_ref, of_ref, wst_ref, ms_ref, ls_ref, as_ref, mf_ref, lf_ref, af_ref, fcar_ref, ftc_ref,
     qna_ref) = rest[3 * pp:]
    c = pl.program_id(1)
    nrow = qn_ref.shape[0]
    page = lfs[0].shape[1]
    kc = pp * page
    rowc = rowc_ref[...]
    slope = rowc[:, 0:1]
    t = past_len + rowc[:, 1:2]
    qn = qn_ref[...]
    qf = qf_ref[...]

    @pl.when(c == 0)
    def _():
        for r in (ms_ref, mf_ref):
            r[...] = jnp.full(r.shape, NEG_INF, F32)
        for r in (ls_ref, as_ref, lf_ref, af_ref, fcar_ref):
            r[...] = jnp.zeros_like(r)
        f_new = ftot_ref[...] + _lane_cumsum(newlf_ref[...])
        rep = _rep_rows(f_new, nq)
        lane = lax.broadcasted_iota(jnp.int32, rep.shape, 1)
        tok = lax.broadcasted_iota(jnp.int32, rep.shape, 0) & (nq - 1)
        ftc_ref[...] = jnp.broadcast_to(
            jnp.sum(jnp.where(lane == tok, rep, 0.0), axis=-1, keepdims=True), ftc_ref.shape)
        qna_ref[:, :LANES] = qn
        qna_ref[:, LANES:] = sel_ref[...]

    ft = ftc_ref[:, 0:1]

    kts = jnp.concatenate([_pages_t(slc, 0), hot_ref[...]], axis=0)
    pos = (c * kc + lax.broadcasted_iota(jnp.int32, (1, kc), 1)).astype(F32)
    s_slc = _nn(qna_ref[...], kts) - slope * (t - pos)

    s_fox = _nn(qf, _pages_t(fox, 0))
    cs = _lane_cumsum(jnp.concatenate([p[...] for p in lfs], axis=0))
    car = fcar_ref[:, 0:1]
    fs_parts = []
    for i in range(pp):
        cs_i = cs[i * FOX_HEADS:(i + 1) * FOX_HEADS, :]
        fs_parts.append(_rep_rows(car + cs_i, nq))
        car = car + cs_i[:, page - 1:page]
    fcar_ref[...] = jnp.broadcast_to(car, fcar_ref.shape)
    fs = jnp.concatenate(fs_parts, axis=1)
    _scr_flash_t([s_slc, s_fox + (ft - fs)], [_pages_t(slc, 1), _pages_t(fox, 1)],
                 [(ms_ref, ls_ref, as_ref), (mf_ref, lf_ref, af_ref)])

    @pl.when(c == pl.num_programs(1) - 1)
    def _():
        npad = newn_ref.shape[0]
        lane_i = lax.broadcasted_iota(jnp.int32, (1, npad), 1)
        posn = (past_len + lane_i).astype(F32)
        okn = (lane_i < nq) & (posn <= t)
        newn = newn_ref[...]
        sn = jnp.where(okn, _nt(qn, newn[:, :LANES].astype(BF16)) - slope * (t - posn), NEG_INF)
        m, l, acc = _flash_step(sn, newn[:, LANES:].astype(BF16), ms_ref[:, 0:1], ls_ref[:, 0:1], as_ref[...])
        o_s = acc / l
        newf = newf_ref[...]
        fsn = _rep_rows(fcar_ref[:, 0:1] + _lane_cumsum(newlf_ref[...]), nq)
        sf = jnp.where(okn, _nt(qf, newf[:, :LANES].astype(BF16)) + (ft - fsn), NEG_INF)
        m, l, acc = _flash_step(sf, newf[:, LANES:].astype(BF16), mf_ref[:, 0:1], lf_ref[:, 0:1], af_ref[...])
        of_ref[...] = acc / l
        wbuf = wbuf_ref[...]
        wnew = wnew_ref[...]
        nbuf = wbuf.shape[0]
        kw = jnp.concatenate([wbuf[:, :LANES], wnew[:, :LANES]], axis=0).astype(BF16)
        vw = jnp.concatenate([wbuf[:, LANES:], wnew[:, LANES:]], axis=0).astype(BF16)
        wl = lax.broadcasted_iota(jnp.int32, (1, nbuf + npad), 1)
        posw = jnp.where(wl < nbuf, past_len - nbuf + wl, past_len + wl - nbuf).astype(F32)
        dw = t - posw
        okw = (dw >= 0) & (dw < WINDOW) & ((wl < nbuf) | (wl < nbuf + nq))
        sw = jnp.where(okw, _nt(qn, kw) - slope * dw, NEG_INF)
        ew = jnp.exp(sw - jnp.max(sw, axis=-1, keepdims=True))
        o_w = _nn(ew.astype(BF16), vw) / jnp.sum(ew, axis=-1, keepdims=True)
        gate = gate_ref[...]
        on_ref[...] = gate[:, 0:1] * oc_ref[...] + gate[:, 1:2] * o_s + gate[:, 2:3] * o_w
        wst_ref[0:nbuf - nq, :] = wbuf[nq:, :]
        wst_ref[nbuf - nq:, :] = wnew[0:nq, :]


def _dec_main_call(pt_flat, qn_dec, qf_dec, sel, rowc, o_c, gates, ftot, newn, newf, newlf, wbuf, wnew, hot,
                   nsa_t, fox_t, lf_pool, pp, n_pages, past_len, nq):
    b, nrow, _ = qn_dec.shape
    nb = past_len // CMP_BLOCK
    page = nsa_t.shape[-1]
    nc = n_pages // pp
    nbuf = wbuf.shape[1]
    assert nb % 16 == 0 and nb <= LANES, "query-side block mask lanes"
    per = lambda shape: pl.BlockSpec((None,) + shape, lambda i, c, pt: (i, 0, 0))
    cst = lambda shape: pl.BlockSpec(shape, lambda i, c, pt: (0, 0))
    rl = (nrow, LANES)
    kv_block = (None, 2, NSA_KV_HEADS, HEAD_DIM, page)
    grid_spec = pltpu.PrefetchScalarGridSpec(
        num_scalar_prefetch=1,
        grid=(b, nc),
        in_specs=[per(rl), per(rl), per((nrow, nb)), cst(rowc.shape), per(rl), per(rl), per((FOX_HEADS, LANES)),
                  per(newn.shape[1:]), per(newf.shape[1:]), per(newlf.shape[1:]),
                  per((nbuf, 2 * LANES)), per(wnew.shape[1:]),
                  pl.BlockSpec((None,) + hot.shape[1:], lambda i, c, pt: (c, 0, 0))]
        + _page_specs(pp, n_pages, kv_block, 1)
        + _page_specs(pp, n_pages, kv_block, 0)
        + _page_specs(pp, n_pages, (None, FOX_HEADS, page), 0),
        out_specs=[per(rl), per(rl), per((nbuf, 2 * LANES))],
        scratch_shapes=[pltpu.VMEM(rl, F32)] * 6 + [pltpu.VMEM((FOX_HEADS, LANES), F32), pltpu.VMEM(rl, F32),
                                                    pltpu.VMEM((nrow, LANES + nb), BF16)],
    )
    return pl.pallas_call(
        functools.partial(_dec_main_kernel, pp=pp, nb=nb, past_len=past_len, nq=nq),
        grid_spec=grid_spec,
        out_shape=[jax.ShapeDtypeStruct((b,) + rl, F32), jax.ShapeDtypeStruct((b,) + rl, F32),
                   jax.ShapeDtypeStruct((b, nbuf, 2 * LANES), F32)],
        compiler_params=_params(("parallel", "arbitrary")),
        name="decode_main",
    )(pt_flat, qn_dec, qf_dec, sel, rowc, o_c, gates, ftot, newn, newf, newlf, wbuf, wnew, hot,
      *([nsa_t] * pp), *([fox_t] * pp), *([lf_pool] * pp))


def _outproj_kernel(x_ref, mod_ref, g_ref, on_ref, of_ref, wm_ref, wa_ref, wb_ref, wo_ref, o_ref):
    x = x_ref[...]
    nb, nt, d = x.shape
    h = _pre(x, mod_ref, g_ref, 1).reshape(nb * nt, d).astype(BF16)
    gm = jax.nn.sigmoid(_nn(h, wm_ref[...]))
    y = gm[:, :d] * _nn(on_ref[...], wa_ref[...]) + gm[:, d:] * _nn(of_ref[...], wb_ref[...])
    mix = _nn(y.astype(BF16), wo_ref[...]).reshape(nb, nt, d)
    o_ref[...] = x + mod_ref[:, 5:6, :] * mix


def _outproj_call(x3, mod, g_norm, o_n, o_f, wm, wa, wb, wo, blk):
    bs, lt, d = x3.shape
    nb, nt = blk
    rows = nb * nt
    nj = lt // nt
    tok = pl.BlockSpec((rows, o_n.shape[1]), lambda i, j: (i * nj + j, 0))
    return pl.pallas_call(
        _outproj_kernel,
        grid=(bs // nb, nj),
        in_specs=[pl.BlockSpec((nb, nt, d), lambda i, j: (i, j, 0)),
                  pl.BlockSpec((nb, 3 * N_SUB, d), lambda i, j: (i, 0, 0)),
                  _const_spec(g_norm.shape), tok, tok,
                  _const_spec(wm.shape), _const_spec(wa.shape), _const_spec(wb.shape), _const_spec(wo.shape)],
        out_specs=pl.BlockSpec((nb, nt, d), lambda i, j: (i, j, 0)),
        out_shape=jax.ShapeDtypeStruct(x3.shape, F32),
        compiler_params=_params(("parallel", "parallel")),
        name="out_proj",
    )(x3, mod, g_norm, o_n, o_f, wm, wa, wb, wo)


def _pair_major_cols(w):
    d = w.shape[0]
    return w.reshape(d, NSA_HEADS, HEAD_DIM)[:, jnp.array(PAIR_PERM), :].reshape(d, NSA_HEADS * HEAD_DIM)


def _pair_major_rows(w):
    d = w.shape[1]
    return w.reshape(NSA_HEADS, HEAD_DIM, d)[jnp.array(PAIR_PERM)].reshape(NSA_HEADS * HEAD_DIM, d)


def _prep_weights(w_in, b_fgt, g_q_nsa, g_k_nsa, g_q_fox, g_k_fox, w_cmp):
    d = w_in.shape[0]
    hq = NSA_HEADS * HEAD_DIM
    o = 0
    w_qn = w_in[:, o:o + hq]; o += hq
    w_kvn = w_in[:, o:o + 3 * 2 * LANES]; o += 3 * 2 * LANES
    w_gt = w_in[:, o:o + 3 * NSA_HEADS]; o += 3 * NSA_HEADS
    w_qf = w_in[:, o:o + hq]; o += hq
    w_kvf = w_in[:, o:o + 2 * LANES]; o += 2 * LANES
    w_ff = w_in[:, o:o + FOX_HEADS]; o += FOX_HEADS
    w_merge = w_in[:, o:]
    k_n = [w_kvn[:, br * 256:br * 256 + LANES] for br in range(3)]
    v_n = [w_kvn[:, br * 256 + LANES:(br + 1) * 256] for br in range(3)]
    small = jnp.concatenate([w_ff, w_gt, jnp.zeros((d, LANES - FOX_HEADS - 3 * NSA_HEADS), w_in.dtype)], axis=1)
    w1 = jnp.concatenate([_pair_major_cols(w_qn), _pair_major_cols(w_qf), k_n[0], k_n[1], k_n[2], w_kvf[:, :LANES],
                          v_n[0], v_n[1], v_n[2], w_kvf[:, LANES:], small], axis=1).astype(BF16)
    scale = HEAD_DIM ** -0.5
    gains = jnp.concatenate([jnp.tile(g_q_nsa * scale, NSA_HEADS), jnp.tile(g_q_fox * scale, FOX_HEADS),
                             jnp.tile(g_k_nsa[0], 2), jnp.tile(g_k_nsa[1], 2), jnp.tile(g_k_nsa[2], 2),
                             jnp.tile(g_k_fox, 2)]).reshape(1, N_NORMED * LANES)
    bf_pad = jnp.concatenate([b_fgt, jnp.zeros((LANES - FOX_HEADS,), F32)]).reshape(1, LANES)
    lane = jnp.arange(LANES)
    seg = (lane[:, None] // HEAD_DIM == lane[None, :] // HEAD_DIM).astype(BF16)
    wc = jnp.concatenate([w_cmp[0], w_cmp[0], w_cmp[1], w_cmp[1]], axis=1)
    return w1, gains, bf_pad, seg, wc, w_merge.astype(BF16)


def _position_lanes(l):
    pos = jnp.arange(l)
    lane = jnp.arange(LANES)[None, :]
    blk = (pos // CMP_BLOCK)[:, None]
    kx = jnp.where(lane < CMP_BLOCK, (lane == blk).astype(F32),
                   jnp.where((lane == XL_ONE_A) | (lane == XL_ONE_B), 1.0,
                             jnp.where(lane == XL_HI, blk.astype(F32),
                                       jnp.where(lane == XL_LO, (pos % CMP_BLOCK)[:, None].astype(F32), 0.0))))
    n = jnp.arange(LANES)[:, None]
    cx = jnp.where((lane == XL_ONE_A) | (lane == XL_ONE_B), 1.0,
                   jnp.where(lane == XL_HI, n.astype(F32), jnp.where(lane == XL_LO, 0.5 * (CMP_BLOCK - 1), 0.0)))
    return kx.astype(BF16), cx.astype(BF16)


def _to_decode_rows(q, b, nq):
    q5 = q.reshape(b, nq, HPG, 2, HEAD_DIM)
    eye = jnp.eye(2, dtype=q.dtype)
    return jnp.einsum('btjgd,gk->bgjtkd', q5, eye).reshape(b, NSA_HEADS * nq, LANES)


def _from_decode_rows(o, b, nq):
    o6 = o.reshape(b, 2, HPG, nq, 2, HEAD_DIM)
    own = jnp.stack([o6[:, 0, :, :, 0], o6[:, 1, :, :, 1]], axis=1)
    return own.transpose(0, 3, 1, 2, 4).reshape(b * nq, NSA_HEADS * HEAD_DIM)


def _pad_rows(x, n):
    return jnp.pad(x, ((0, 0), (0, n - x.shape[1]), (0, 0)))


def kernel(x_prompt, x_sample, cache_nsa_kv, cache_fox_kv, cache_fox_logf, state_win_kv, page_table,
           c_prompt, c_sample, w_ada, b_ada, g_norm, w_ffn_up, w_ffn_down, w_in, b_fgt,
           g_q_nsa, g_k_nsa, g_q_fox, g_k_fox, w_cmp, w_br_nsa, w_br_fox, w_out):
    assert w_ada.shape[0] == 1, "single-layer trunk"
    bp, lp, d = x_prompt.shape
    bs, nq, _ = x_sample.shape
    n_pool, page = cache_nsa_kv.shape[1:3]
    n_pages = page_table.shape[1]
    past_len = n_pages * page
    nbuf = state_win_kv.shape[2]
    dff = w_ffn_down.shape[2]
    assert nq % 8 == 0 and nq & (nq - 1) == 0 and nq <= CMP_BLOCK and past_len // CMP_BLOCK >= SLC_TOPK
    assert nbuf == WINDOW and lp % 512 == 0

    mod = _mod_call(jnp.concatenate([c_prompt, c_sample], axis=0), w_ada[0], b_ada[0]).reshape(bp + bs, 3 * N_SUB, d)
    mod_p, mod_s = mod[:bp], mod[bp:]
    gn = g_norm[0]
    up = w_ffn_up[0].astype(BF16)
    ffn_w = [(up[i, :, :dff], up[i, :, dff:], w_ffn_down[0, i].astype(BF16)) for i in range(2)]
    w1, gains, bf_pad, seg, wc, w_merge = _prep_weights(
        w_in[0], b_fgt[0], g_q_nsa[0], g_k_nsa[0], g_q_fox[0], g_k_fox[0], w_cmp[0])
    wo = w_out[0].astype(BF16)
    wa_nat, wb_nat = w_br_nsa[0].astype(BF16), w_br_fox[0].astype(BF16)
    wa_pm, wb_pm = _pair_major_rows(wa_nat), _pair_major_rows(wb_nat)

    pblk = (1, 512)
    tq, tk = 256, 256
    x1 = _ffn_call(x_prompt, mod_p, gn, *ffn_w[0], 0, pblk)
    qn, qf, nsa_new, win_new, fox_new, small, kvb, cmpb = _inproj_call(
        x1, mod_p, gn, w1, gains, bf_pad, seg, wc, pblk, True)
    small3 = small.reshape(bp, lp, LANES)
    logf_p = small3[:, :, :FOX_HEADS]
    kv3 = kvb.reshape(bp, lp, 768)
    kaug_f, qx_f = _cumsum_call(small3, kv3, tq)

    ones_rows = (jnp.arange(VT_ROWS - LANES) == 0).astype(BF16)[None, None, :, None]

    def dims_major(c):
        vt = kv3[:, :, c * LANES:(c + 1) * LANES].reshape(bp, lp // tk, tk, LANES).transpose(0, 1, 3, 2)
        return jnp.concatenate([vt, jnp.broadcast_to(ones_rows, (bp, lp // tk, VT_ROWS - LANES, tk))], axis=2)

    kx, cx = _position_lanes(lp)
    o_n = _nsa_prompt_call(qn.reshape(bp, lp, 512), kv3, kx, dims_major(1), dims_major(3),
                           cmpb.reshape(bp, lp // CMP_BLOCK, 256), cx, small3, tq, tk)
    o_f = _fox_prompt_call(qf.reshape(bp, lp, 512), qx_f, kaug_f, dims_major(5), tq, tk)
    x2 = _outproj_call(x1, mod_p, gn, o_n.reshape(bp * lp, 512), o_f.reshape(bp * lp, 512),
                       w_merge, wa_pm, wb_pm, wo, pblk)
    y_prompt = _ffn_call(x2, mod_p, gn, *ffn_w[1], 2, pblk)
    def rows_major(t, planes, rows):
        return t.reshape(bp, planes, NSA_KV_HEADS, HEAD_DIM, rows).transpose(0, 4, 1, 2, 3)[None]

    nwin = min(WINDOW, lp)
    nsa_p = rows_major(nsa_new, 4, lp)
    fox_p = rows_major(fox_new, 2, lp)
    win_p = rows_major(win_new[:, :, lp - nwin:], 2, nwin)

    sblk = (min(bs, 64), nq)
    pp = next(p for p in (16, 8, 4, 2, 1) if n_pages % p == 0)
    x1s = _ffn_call(x_sample, mod_s, gn, *ffn_w[0], 0, sblk)
    qn_s, qf_s, nsa_s, win_s, fox_s, small_s, _ = _inproj_call(
        x1s, mod_s, gn, w1, gains, bf_pad, seg, wc, sblk, False)
    nrow = NSA_HEADS * nq
    qn_dec = _to_decode_rows(qn_s, bs, nq)
    qf_dec = _to_decode_rows(qf_s, bs, nq)
    rows = jnp.arange(nrow)
    rowc = jnp.zeros((nrow, LANES), F32).at[:, 0].set(jnp.array(SLOPES, F32)[rows // nq]).at[:, 1].set(
        (rows % nq).astype(F32))
    sm3 = small_s.reshape(bs, nq, LANES)
    gates = sm3[:, :, FOX_HEADS:FOX_HEADS + 3 * NSA_HEADS].reshape(bs, nq, 3, NSA_HEADS)
    gates = jnp.pad(gates.transpose(0, 3, 1, 2).reshape(bs, nrow, 3), ((0, 0), (0, 0), (0, LANES - 3)))
    newlf = _pad_rows(sm3[:, :, :FOX_HEADS], LANES).transpose(0, 2, 1)
    newn = _pad_rows(nsa_s.reshape(bs, nq, 512)[:, :, 256:], LANES)
    newf = _pad_rows(fox_s.reshape(bs, nq, 256), LANES)
    wnew = _pad_rows(win_s.reshape(bs, nq, 256), LANES)
    nsa_t = cache_nsa_kv[0].transpose(0, 2, 3, 4, 1)
    fox_t = cache_fox_kv[0].transpose(0, 2, 3, 4, 1)
    wct = jnp.tile(w_cmp[0].transpose(0, 2, 1), (1, 1, page // CMP_BLOCK))
    kc = pp * page
    col_blk = jnp.arange(kc) // CMP_BLOCK
    bsum = (jnp.arange(kc // CMP_BLOCK)[:, None] == col_blk[None, :]).astype(BF16)
    nblocks = past_len // CMP_BLOCK
    hot = (jnp.arange(nblocks)[None, :, None]
           == (jnp.arange(n_pages // pp)[:, None, None] * (kc // CMP_BLOCK) + col_blk[None, None, :])).astype(BF16)
    lf_pool = cache_fox_logf[0].astype(F32).transpose(0, 2, 1)
    wbuf = state_win_kv[0].reshape(bs, nbuf, 256)
    pt_flat = page_table.reshape(-1).astype(jnp.int32)
    o_c, sel, ftot = _dec_cmp_call(pt_flat, qn_dec, rowc, wct, bsum, nsa_t, lf_pool, pp, n_pages, past_len)
    on_dec, of_dec, wst = _dec_main_call(pt_flat, qn_dec, qf_dec, sel, rowc, o_c, gates, ftot, newn, newf, newlf,
                                         wbuf, wnew, hot, nsa_t, fox_t, lf_pool, pp, n_pages, past_len, nq)
    o_ns = _from_decode_rows(on_dec, bs, nq).astype(BF16)
    o_fs = _from_decode_rows(of_dec, bs, nq).astype(BF16)
    x2s = _outproj_call(x1s, mod_s, gn, o_ns, o_fs, w_merge, wa_nat, wb_nat, wo, sblk)
    y_sample = _ffn_call(x2s, mod_s, gn, *ffn_w[1], 2, sblk)
    nsa_o = nsa_s.reshape(1, bs, nq, 4, NSA_KV_HEADS, HEAD_DIM)
    fox_o = fox_s.reshape(1, bs, nq, 2, FOX_KV_HEADS, HEAD_DIM)
    win_o = wst.reshape(1, bs, nbuf, 2, NSA_KV_HEADS, HEAD_DIM)

    return (y_prompt, y_sample, nsa_p, nsa_o, fox_p, fox_o,
            logf_p[None], sm3[None, :, :, :FOX_HEADS], win_p, win_o)
```

```python
import functools

import jax
import jax.numpy as jnp
from jax import lax
from jax.experimental import pallas as pl
from jax.experimental.pallas import tpu as pltpu

HEAD_DIM = 64
NSA_HEADS = 8
NSA_KV_HEADS = 2
FOX_HEADS = 8
FOX_KV_HEADS = 2
HPG = 4
CMP_BLOCK = 64
CMP_SHIFT = 6
SLC_TOPK = 8
WINDOW = 512
N_SUB = 3
RMS_EPS = 1e-6
NEG_INF = -1e30
FORCE_BONUS = 1e4
LANES = 128
PAIR_PERM = (0, 4, 1, 5, 2, 6, 3, 7)
SLOPES = tuple(2.0 ** -(h + 1) for h in range(NSA_HEADS))
VMEM_LIMIT = 56 * 1024 * 1024

F32 = jnp.float32
BF16 = jnp.bfloat16


def _nt(a, b):
    return lax.dot_general(a, b, (((1,), (1,)), ((), ())), preferred_element_type=F32)


def _nn(a, b):
    return jnp.dot(a, b, preferred_element_type=F32)


def _split3(x):
    hi = x.astype(BF16)
    r1 = x - hi.astype(F32)
    mid = r1.astype(BF16)
    lo = (r1 - mid.astype(F32)).astype(BF16)
    return hi, mid, lo


def _params(sem):
    return pltpu.CompilerParams(dimension_semantics=sem, vmem_limit_bytes=VMEM_LIMIT)


def _const_spec(shape):
    nd = len(shape)
    return pl.BlockSpec(shape, lambda *_: (0,) * nd, pipeline_mode=pl.Buffered(1))


def _mod_kernel(c_ref, w_ref, b_ref, o_ref):
    c = c_ref[...]
    a = (c * jax.nn.sigmoid(c)).astype(BF16)
    o_ref[...] = _nn(a, w_ref[...].astype(BF16)) + b_ref[...]


def _mod_call(c_all, w_ada, b_ada):
    bc, d = c_all.shape
    n = w_ada.shape[1]
    tn = 1024 if n % 1024 == 0 else n
    return pl.pallas_call(
        _mod_kernel,
        grid=(n // tn,),
        in_specs=[pl.BlockSpec((bc, d), lambda i: (0, 0)),
                  pl.BlockSpec((d, tn), lambda i: (0, i)),
                  pl.BlockSpec((1, tn), lambda i: (0, i))],
        out_specs=pl.BlockSpec((bc, tn), lambda i: (0, i)),
        out_shape=jax.ShapeDtypeStruct((bc, n), F32),
        compiler_params=_params(("parallel",)),
        name="adaln_mod",
    )(c_all, w_ada, b_ada.reshape(1, n))


def _pre(x, mod_ref, g_ref, sub):
    ms = jnp.mean(x * x, axis=-1, keepdims=True)
    y = x * lax.rsqrt(ms + RMS_EPS) * g_ref[sub:sub + 1, :]
    return y * (1.0 + mod_ref[:, 3 * sub + 1:3 * sub + 2, :]) + mod_ref[:, 3 * sub:3 * sub + 1, :]


def _ffn_kernel(x_ref, mod_ref, g_ref, wa_ref, wb_ref, wd_ref, o_ref, act_ref, *, sub, fc):
    x = x_ref[...]
    nb, nt, d = x.shape
    h = _pre(x, mod_ref, g_ref, sub).reshape(nb * nt, d).astype(BF16)
    dff = wa_ref.shape[1]
    for c in range(dff // fc):
        a = _nn(h, wa_ref[:, c * fc:(c + 1) * fc])
        b = _nn(h, wb_ref[:, c * fc:(c + 1) * fc])
        act_ref[:, c * fc:(c + 1) * fc] = (a * jax.nn.sigmoid(a) * b).astype(BF16)
    y = _nn(act_ref[...], wd_ref[...]).reshape(nb, nt, d)
    o_ref[...] = x + 0.5 * mod_ref[:, 3 * sub + 2:3 * sub + 3, :] * y


def _ffn_call(x3, mod, g_norm, wa, wb, wd, sub, blk):
    bs, lt, d = x3.shape
    nb, nt = blk
    dff = wa.shape[1]
    fc = 256 if dff % 256 == 0 else dff
    grid = (bs // nb, lt // nt)
    return pl.pallas_call(
        functools.partial(_ffn_kernel, sub=sub, fc=fc),
        grid=grid,
        in_specs=[pl.BlockSpec((nb, nt, d), lambda i, j: (i, j, 0)),
                  pl.BlockSpec((nb, 3 * N_SUB, d), lambda i, j: (i, 0, 0)),
                  _const_spec(g_norm.shape), _const_spec(wa.shape), _const_spec(wb.shape),
                  _const_spec(wd.shape)],
        out_specs=pl.BlockSpec((nb, nt, d), lambda i, j: (i, j, 0)),
        out_shape=jax.ShapeDtypeStruct(x3.shape, F32),
        scratch_shapes=[pltpu.VMEM((nb * nt, dff), BF16)],
        compiler_params=_params(("parallel", "parallel")),
        name="ffn_sub%d" % sub,
    )(x3, mod, g_norm, wa, wb, wd)


N_NORMED = 12


def _inproj_kernel(x_ref, mod_ref, g_ref, w_ref, gain_ref, bf_ref, seg_ref, *rest, with_cmp):
    if with_cmp:
        wc_ref, qn_ref, qf_ref, nsa_ref, win_ref, fox_ref, sm_ref, kv_ref, cmp_ref = rest
    else:
        qn_ref, qf_ref, nsa_ref, win_ref, fox_ref, sm_ref, kv_ref = rest
    x = x_ref[...]
    nb, nt, d = x.shape
    h = _pre(x, mod_ref, g_ref, 1).reshape(nb * nt, d).astype(BF16)
    z = _nn(h, w_ref[...])
    seg = seg_ref[...]
    normed = []
    for c in range(N_NORMED):
        zc = z[:, c * LANES:(c + 1) * LANES]
        zz = zc * zc
        hi = zz.astype(BF16)
        lo = (zz - hi.astype(F32)).astype(BF16)
        ss = _nn(hi, seg) + _nn(lo, seg)
        normed.append(zc * lax.rsqrt(ss * (1.0 / HEAD_DIM) + RMS_EPS) * gain_ref[:, c * LANES:(c + 1) * LANES])
    raw = [z[:, (N_NORMED + c) * LANES:(N_NORMED + c + 1) * LANES] for c in range(4)]
    k_cmp, k_slc, k_win, k_fox = normed[8:12]
    v_cmp, v_slc, v_win, v_fox = raw
    for j in range(4):
        qn_ref[:, j * LANES:(j + 1) * LANES] = normed[j].astype(BF16)
        qf_ref[:, j * LANES:(j + 1) * LANES] = normed[4 + j].astype(BF16)
    def put(ref, c, t):
        if with_cmp:
            ref[c * LANES:(c + 1) * LANES, :] = t.T
        else:
            ref[:, c * LANES:(c + 1) * LANES] = t

    for c, t in enumerate((k_cmp, v_cmp, k_slc, v_slc)):
        put(nsa_ref, c, t)
    put(win_ref, 0, k_win)
    put(win_ref, 1, v_win)
    put(fox_ref, 0, k_fox)
    put(fox_ref, 1, v_fox)
    for c, t in enumerate((k_slc, v_slc, k_win, v_win, k_fox, v_fox)):
        kv_ref[:, c * LANES:(c + 1) * LANES] = t.astype(BF16)
    zs = z[:, 16 * LANES:17 * LANES]
    lane = lax.broadcasted_iota(jnp.int32, zs.shape, 1)
    zf = zs + bf_ref[...]
    logf = jnp.minimum(zf, 0.0) - jnp.log1p(jnp.exp(-jnp.abs(zf)))
    sm_ref[...] = jnp.where(lane < FOX_HEADS, logf, jax.nn.sigmoid(zs))
    if with_cmp:
        rows = nb * nt
        kvc = jnp.concatenate([k_cmp, v_cmp], axis=1).reshape(rows // CMP_BLOCK, CMP_BLOCK, 2 * LANES)
        cmp_ref[...] = jnp.sum(kvc * wc_ref[...][None], axis=1)


def _inproj_call(x3, mod, g_norm, w1, gains, bf_pad, seg, wc, blk, with_cmp):
    bs, lt, d = x3.shape
    nb, nt = blk
    rows = nb * nt
    t = bs * lt
    grid = (bs // nb, lt // nt)
    nj = lt // nt

    def tok(cols):
        return pl.BlockSpec((rows, cols), lambda i, j: (i * nj + j, 0))

    in_specs = [pl.BlockSpec((nb, nt, d), lambda i, j: (i, j, 0)),
                pl.BlockSpec((nb, 3 * N_SUB, d), lambda i, j: (i, 0, 0)),
                _const_spec(g_norm.shape), _const_spec(w1.shape), _const_spec(gains.shape),
                _const_spec(bf_pad.shape), _const_spec(seg.shape)]
    args = [x3, mod, g_norm, w1, gains, bf_pad, seg]
    out_specs = [tok(512), tok(512), tok(512), tok(256), tok(256), tok(128), tok(768)]
    out_shape = [jax.ShapeDtypeStruct((t, 512), BF16), jax.ShapeDtypeStruct((t, 512), BF16),
                 jax.ShapeDtypeStruct((t, 512), F32), jax.ShapeDtypeStruct((t, 256), F32),
                 jax.ShapeDtypeStruct((t, 256), F32), jax.ShapeDtypeStruct((t, 128), F32),
                 jax.ShapeDtypeStruct((t, 768), BF16)]
    if with_cmp:
        assert nb == 1
        for k, cols in ((2, 512), (3, 256), (4, 256)):
            out_specs[k] = pl.BlockSpec((None, cols, nt), lambda i, j: (i, 0, j))
            out_shape[k] = jax.ShapeDtypeStruct((bs, cols, lt), F32)
        in_specs.append(_const_spec(wc.shape))
        args.append(wc)
        out_specs.append(pl.BlockSpec((rows // CMP_BLOCK, 256), lambda i, j: (i * nj + j, 0)))
        out_shape.append(jax.ShapeDtypeStruct((t // CMP_BLOCK, 256), F32))
    return pl.pallas_call(
        functools.partial(_inproj_kernel, with_cmp=with_cmp),
        grid=grid, in_specs=in_specs, out_specs=out_specs, out_shape=out_shape,
        compiler_params=_params(("parallel", "parallel")),
        name="in_proj",
    )(*args)


FQ0 = 0
FK0 = 3 * FOX_HEADS


def _cumsum_kernel(tok_ref, k_ref, kaug_ref, qx_ref, carry_ref):
    @pl.when(pl.program_id(1) == 0)
    def _():
        carry_ref[...] = jnp.zeros_like(carry_ref)

    tc = tok_ref.shape[0]
    r = lax.broadcasted_iota(jnp.int32, (tc, tc), 0)
    c = lax.broadcasted_iota(jnp.int32, (tc, tc), 1)
    lower = jnp.where(r >= c, 1.0, 0.0).astype(BF16)
    cs = carry_ref[...] + sum(_nn(lower, p) for p in _split3(tok_ref[...]))
    carry_ref[...] = cs[tc - 1:tc, :]
    hrow = lax.broadcasted_iota(jnp.int32, (LANES, LANES), 0)
    col = lax.broadcasted_iota(jnp.int32, (LANES, LANES), 1)
    placed = jnp.zeros((tc, LANES), F32)
    for p, piece in enumerate(_split3(cs)):
        tgt = 3 * hrow + p
        place = jnp.where((hrow < FOX_HEADS) & ((col == FQ0 + tgt) | (col == FK0 + tgt)), 1.0, 0.0).astype(BF16)
        placed = placed + _nn(piece, place)
    lane = lax.broadcasted_iota(jnp.int32, (tc, LANES), 1)
    in_q = lane < FK0
    in_k = (lane >= FK0) & (lane < 2 * FK0)
    qx_ref[...] = jnp.where(in_q, placed, jnp.where(in_k, 1.0, 0.0)).astype(BF16)
    kaug_ref[:, :LANES] = k_ref[...]
    kaug_ref[:, LANES:] = jnp.where(in_q, 1.0, jnp.where(in_k, -placed, 0.0)).astype(BF16)


def _cumsum_call(small3, kv3, tc):
    b, l, _ = small3.shape
    return pl.pallas_call(
        _cumsum_kernel,
        grid=(b, l // tc),
        in_specs=[pl.BlockSpec((None, tc, LANES), lambda i, j: (i, j, 0)),
                  pl.BlockSpec((None, tc, LANES), lambda i, j: (i, j, 4))],
        out_specs=[pl.BlockSpec((None, tc, 2 * LANES), lambda i, j: (i, j, 0)),
                   pl.BlockSpec((None, tc, LANES), lambda i, j: (i, j, 0))],
        out_shape=[jax.ShapeDtypeStruct((b, l, 2 * LANES), BF16),
                   jax.ShapeDtypeStruct((b, l, LANES), BF16)],
        scratch_shapes=[pltpu.VMEM((1, LANES), F32)],
        compiler_params=_params(("parallel", "arbitrary")),
        name="fox_cumsum",
    )(small3, kv3)


def _flash_step(s, v, m, l, acc):
    m_new = jnp.maximum(m, jnp.max(s, axis=-1, keepdims=True))
    alpha = jnp.exp(m - m_new)
    p = jnp.exp(s - m_new)
    l = alpha * l + jnp.sum(p, axis=-1, keepdims=True)
    acc = alpha * acc + _nn(p.astype(BF16), v)
    return m_new, l, acc


def _flash_init(rows):
    return (jnp.full((rows, 1), NEG_INF, F32), jnp.zeros((rows, 1), F32), jnp.zeros((rows, LANES), F32))


def _group_masked(qp, g):
    lane = lax.broadcasted_iota(jnp.int32, qp.shape, 1)
    keep = (lane < HEAD_DIM) if g == 0 else (lane >= HEAD_DIM)
    return jnp.where(keep, qp, jnp.zeros_like(qp))


def _top_blocks(score, n_pick, axis=1):
    idx = lax.broadcasted_iota(jnp.int32, score.shape, axis).astype(F32)
    sel = jnp.zeros(score.shape, F32)
    for _ in range(n_pick):
        mx = jnp.max(score, axis=axis, keepdims=True)
        first = jnp.min(jnp.where(score == mx, idx, 1e9), axis=axis, keepdims=True)
        pick = idx == first
        sel = jnp.where(pick, 1.0, sel)
        score = jnp.where(pick, -jnp.inf, score)
    return sel


XL_ONE_A = CMP_BLOCK
XL_ONE_B = CMP_BLOCK + 1
XL_HI = CMP_BLOCK + 2
XL_LO = CMP_BLOCK + 3
MASK_BIG = 1e30
N_FREE = SLC_TOPK - 3


VT_ROWS = LANES + 8


def _col_max(s):
    k = s.shape[0]
    if k % 64 == 0:
        s = jnp.max(s.reshape(8, k // 8, s.shape[1]), axis=0)
    return jnp.max(s, axis=0, keepdims=True)


def _heads_scores(k, qa_ref, ok):
    out = []
    for h in range(qa_ref.shape[0]):
        s = _nt(k, qa_ref[h])
        out.append(s if ok is None else jnp.where(ok, s, NEG_INF))
    return out


def _heads_stage(scores, s_ref):
    for h, s in enumerate(scores):
        s_ref[h] = s


def _heads_consume(vt, s_ref, m_ref, acc_ref):
    nh = s_ref.shape[0]
    probs, alphas = [], []
    for h in range(nh):
        s = s_ref[h]
        m_old = m_ref[h:h + 1, :]
        m_new = jnp.maximum(m_old, _col_max(s))
        m_ref[h:h + 1, :] = m_new
        alphas.append(jnp.exp(m_old - m_new))
        probs.append(jnp.exp(s - m_new).astype(BF16))
    for h in range(nh):
        acc_ref[h] = alphas[h] * acc_ref[h] + _nn(vt, probs[h])


def _heads_run(k_of, vt_ref, qa_ref, s_ref, m_ref, acc_ref, lo, qi, mask_of, mask_all):
    m_ref[...] = jnp.full(m_ref.shape, NEG_INF, F32)
    acc_ref[...] = jnp.zeros_like(acc_ref)
    _heads_stage(_heads_scores(k_of(qi), qa_ref, mask_of(qi)), s_ref)

    def body(kj, c):
        new = _heads_scores(k_of(kj), qa_ref, mask_of(kj) if mask_all else None)
        _heads_consume(vt_ref[jnp.where(kj == lo, qi, kj - 1)], s_ref, m_ref, acc_ref)
        _heads_stage(new, s_ref)
        return c

    lax.fori_loop(lo, qi, body, 0)
    _heads_consume(vt_ref[jnp.where(qi > lo, qi - 1, qi)], s_ref, m_ref, acc_ref)


def _attn_scratch(tq, tk):
    return [pltpu.VMEM((NSA_HEADS, tq, 2 * LANES), BF16), pltpu.VMEM((NSA_HEADS, tq, 2 * LANES), BF16),
            pltpu.VMEM((NSA_HEADS, tq), F32), pltpu.VMEM((NSA_HEADS, tk, tq), F32),
            pltpu.VMEM((NSA_HEADS, VT_ROWS, tq), F32), pltpu.VMEM((NSA_HEADS, LANES, tq), F32)]


def _finish_t(acc):
    return acc[:LANES, :] / acc[LANES:LANES + 1, :]


def _nsa_prompt_kernel(q_ref, ks_ref, kw_ref, kx_ref, vst_ref, vwt_ref, cmp_ref, cx_ref, sm_ref, o_ref,
                       qas_ref, qaw_ref, m_ref, s_ref, acc_ref, out_ref, *, tq, tk, nb):
    qi = pl.program_id(1)
    lane = lax.broadcasted_iota(jnp.int32, (tq, LANES), 1)
    t2 = qi * tq + lax.broadcasted_iota(jnp.int32, (tq, LANES), 0)
    t_hi = lax.shift_right_logical(t2, CMP_SHIFT).astype(F32)
    t_lo = (t2 & (CMP_BLOCK - 1)).astype(F32)
    alibi_q = jnp.where(lane == XL_ONE_A, -CMP_BLOCK * t_hi,
                        jnp.where(lane == XL_ONE_B, -t_lo,
                                  jnp.where(lane == XL_HI, float(CMP_BLOCK), jnp.where(lane == XL_LO, 1.0, 0.0))))
    krow = lax.broadcasted_iota(jnp.int32, (tk, tq), 0)
    qcol = lax.broadcasted_iota(jnp.int32, (tk, tq), 1)
    rows = lax.broadcasted_iota(jnp.int32, (LANES, tq), 0)

    cmpv = cmp_ref[...]
    ck_aug = jnp.concatenate([cmpv[:, :LANES].astype(BF16), cx_ref[...]], axis=1)
    cvt = jnp.concatenate([cmpv[:, LANES:], jnp.zeros((LANES - nb, LANES), F32)],
                          axis=0).T[:, :nb].astype(BF16)
    brow = lax.broadcasted_iota(jnp.int32, (nb, tq), 0)
    tcol = qi * tq + lax.broadcasted_iota(jnp.int32, (nb, tq), 1)
    valid_c = (brow + 1) * CMP_BLOCK - 1 <= tcol
    cur = lax.shift_right_logical(tcol, CMP_SHIFT)
    forced = (brow == 0) | (brow == cur) | (brow == cur - 1)
    free = (brow * CMP_BLOCK <= tcol) & jnp.logical_not(forced)
    smt = sm_ref[...].T

    for g in range(NSA_KV_HEADS):
        imp = jnp.zeros((nb, tq), F32)
        qx = []
        for j in range(HPG):
            h = g * HPG + j
            qx.append(SLOPES[h] * alibi_q)
            qaw_ref[h, :, :LANES] = _group_masked(q_ref[:, j * LANES:(j + 1) * LANES], g)
            qaw_ref[h, :, LANES:] = qx[j].astype(BF16)
            s = jnp.where(valid_c, _nt(ck_aug, qaw_ref[h]), NEG_INF)
            e = jnp.exp(s - jnp.max(s, axis=0, keepdims=True))
            p = jnp.where(valid_c, e / jnp.sum(e, axis=0, keepdims=True), 0.0)
            imp = imp + p
            c0 = FOX_HEADS + h
            out_ref[h] = smt[c0:c0 + 1, :] * _nn(cvt, p.astype(BF16))
        score = jnp.where(free, imp, jnp.where(forced, 2 * NEG_INF, NEG_INF))
        sel = jnp.where(forced, 1.0, _top_blocks(score, N_FREE, axis=0))
        sel = jnp.concatenate([sel, jnp.zeros((LANES - nb, tq), F32)], axis=0).T
        sel_q = jnp.where(lane < CMP_BLOCK, (sel - 1.0) * MASK_BIG, 0.0)
        for j in range(HPG):
            h = g * HPG + j
            qas_ref[h, :, :LANES] = qaw_ref[h, :, :LANES]
            qas_ref[h, :, LANES:] = (qx[j] + sel_q).astype(BF16)

    assert tq == tk, "the pipelined key loop walks one key tile per query tile"

    def run_branch(qa_ref, k_ref, vt_ref, lo, window, gate_row):
        def k_of(kj):
            off = pl.multiple_of(kj * tk, tk)
            return jnp.concatenate([k_ref[pl.ds(off, tk), :], kx_ref[pl.ds(off, tk), :]], axis=1)

        def mask_of(kj):
            dist = (qi - kj) * tq + qcol - krow
            return (dist >= 0) & (dist < WINDOW) if window else dist >= 0

        _heads_run(k_of, vt_ref, qa_ref, s_ref, m_ref, acc_ref, lo, qi, mask_of, window)
        for h in range(NSA_HEADS):
            r = gate_row + h
            out_ref[h] = out_ref[h] + smt[r:r + 1, :] * _finish_t(acc_ref[h])

    run_branch(qas_ref, ks_ref, vst_ref, 0, False, FOX_HEADS + NSA_HEADS)
    run_branch(qaw_ref, kw_ref, vwt_ref, jnp.maximum(qi - (WINDOW + tk - 1) // tk, 0), True,
               FOX_HEADS + 2 * NSA_HEADS)
    for j in range(HPG):
        pair_t = jnp.where(rows < HEAD_DIM, out_ref[j], out_ref[HPG + j])
        o_ref[:, j * LANES:(j + 1) * LANES] = pair_t.T.astype(BF16)


def _nsa_prompt_call(qn3, kv3, kx, vst, vwt, cmp3, cx, small3, tq, tk):
    b, l, _ = qn3.shape
    nb = cmp3.shape[1]
    assert nb <= CMP_BLOCK, "one-hot block lanes cover at most CMP_BLOCK blocks"

    def kspec(c):
        return pl.BlockSpec((None, l, LANES), lambda i, j: (i, 0, c))

    vspec = pl.BlockSpec((None, l // tk, VT_ROWS, tk), lambda i, j: (i, 0, 0, 0))
    return pl.pallas_call(
        functools.partial(_nsa_prompt_kernel, tq=tq, tk=tk, nb=nb),
        grid=(b, l // tq),
        in_specs=[pl.BlockSpec((None, tq, 512), lambda i, j: (i, j, 0)),
                  kspec(0), kspec(2), pl.BlockSpec((l, LANES), lambda i, j: (0, 0)), vspec, vspec,
                  pl.BlockSpec((None, nb, 256), lambda i, j: (i, 0, 0)),
                  pl.BlockSpec((nb, LANES), lambda i, j: (0, 0)),
                  pl.BlockSpec((None, tq, LANES), lambda i, j: (i, j, 0))],
        out_specs=pl.BlockSpec((None, tq, 512), lambda i, j: (i, j, 0)),
        out_shape=jax.ShapeDtypeStruct((b, l, 512), BF16),
        scratch_shapes=_attn_scratch(tq, tk),
        compiler_params=_params(("parallel", "arbitrary")),
        name="nsa_prompt",
    )(qn3, kv3, kv3, kx, vst, vwt, cmp3, cx, small3)


def _fox_prompt_kernel(q_ref, qx_ref, k_ref, vt_ref, o_ref, qa_ref, m_ref, s_ref, acc_ref, *, tq, tk):
    assert tq == tk, "the pipelined key loop walks one key tile per query tile"
    qi = pl.program_id(1)
    lane = lax.broadcasted_iota(jnp.int32, (tq, LANES), 1)
    krow = lax.broadcasted_iota(jnp.int32, (tk, tq), 0)
    qcol = lax.broadcasted_iota(jnp.int32, (tk, tq), 1)
    rows = lax.broadcasted_iota(jnp.int32, (LANES, tq), 0)
    qx = qx_ref[...]
    for g in range(FOX_KV_HEADS):
        for j in range(HPG):
            h = g * HPG + j
            own = ((lane >= FQ0 + 3 * h) & (lane < FQ0 + 3 * h + 3)) | ((lane >= FK0 + 3 * h) & (lane < FK0 + 3 * h + 3))
            qa_ref[h, :, :LANES] = _group_masked(q_ref[:, j * LANES:(j + 1) * LANES], g)
            qa_ref[h, :, LANES:] = jnp.where(own, qx, jnp.zeros_like(qx))
    _heads_run(lambda kj: k_ref[pl.ds(pl.multiple_of(kj * tk, tk), tk), :], vt_ref, qa_ref, s_ref, m_ref, acc_ref,
               0, qi, lambda kj: (qi - kj) * tq + qcol - krow >= 0, False)
    for j in range(HPG):
        pair_t = jnp.where(rows < HEAD_DIM, _finish_t(acc_ref[j]), _finish_t(acc_ref[HPG + j]))
        o_ref[:, j * LANES:(j + 1) * LANES] = pair_t.T.astype(BF16)


def _fox_prompt_call(qf3, qx3, kaug3, vft, tq, tk):
    b, l, _ = qf3.shape
    return pl.pallas_call(
        functools.partial(_fox_prompt_kernel, tq=tq, tk=tk),
        grid=(b, l // tq),
        in_specs=[pl.BlockSpec((None, tq, 512), lambda i, j: (i, j, 0)),
                  pl.BlockSpec((None, tq, LANES), lambda i, j: (i, j, 0)),
                  pl.BlockSpec((None, l, 2 * LANES), lambda i, j: (i, 0, 0)),
                  pl.BlockSpec((None, l // tk, VT_ROWS, tk), lambda i, j: (i, 0, 0, 0))],
        out_specs=pl.BlockSpec((None, tq, 512), lambda i, j: (i, j, 0)),
        out_shape=jax.ShapeDtypeStruct((b, l, 512), BF16),
        scratch_shapes=_attn_scratch(tq, tk)[1:5],
        compiler_params=_params(("parallel", "arbitrary")),
        name="fox_prompt",
    )(qf3, qx3, kaug3, vft)


def _lane_cumsum(x):
    n = x.shape[1]
    r = lax.broadcasted_iota(jnp.int32, (n, n), 0)
    c = lax.broadcasted_iota(jnp.int32, (n, n), 1)
    upper = jnp.where(r <= c, 1.0, 0.0).astype(BF16)
    return sum(_nn(p, upper) for p in _split3(x))


def _dec_cmp_kernel(pt_ref, q_ref, rowc_ref, wct_ref, bsum_ref, *rest, pp, nb, past_len):
    del pt_ref
    pages = rest[:pp]
    lfs = rest[pp:2 * pp]
    oc_ref, sel_ref, ftot_ref, cmp_scr, ftot_scr = rest[2 * pp:]
    c = pl.program_id(1)

    @pl.when(c == 0)
    def _():
        ftot_scr[...] = jnp.zeros_like(ftot_scr)

    nblk = bsum_ref.shape[0]
    bsum = bsum_ref[...]
    for plane in range(2):
        w = wct_ref[plane]
        for g in range(NSA_KV_HEADS):
            prod = jnp.concatenate([p[plane, g] * w for p in pages], axis=1)
            blk = sum(_nt(bsum, piece) for piece in _split3(prod))
            cmp_scr[2 * plane + g, pl.ds(pl.multiple_of(c * nblk, nblk), nblk), :] = blk
    tot = lfs[0][...]
    for p in lfs[1:]:
        tot = tot + p[...]
    ftot_scr[...] += jnp.broadcast_to(jnp.sum(tot, axis=-1, keepdims=True), ftot_scr.shape)

    @pl.when(c == pl.num_programs(1) - 1)
    def _():
        ftot_ref[...] = ftot_scr[...]
        ckb = jnp.concatenate([cmp_scr[0], cmp_scr[1]], axis=1).astype(BF16)
        cvb = jnp.concatenate([cmp_scr[2], cmp_scr[3]], axis=1).astype(BF16)
        nrow = q_ref.shape[0]
        rowc = rowc_ref[...]
        slope = rowc[:, 0:1]
        t = past_len + rowc[:, 1:2]
        nidx = lax.broadcasted_iota(jnp.int32, (nrow, nb), 1)
        cmid = nidx.astype(F32) * CMP_BLOCK + 0.5 * (CMP_BLOCK - 1)
        valid = cmid + 0.5 * (CMP_BLOCK - 1) <= t
        s = jnp.where(valid, _nt(q_ref[...], ckb) - slope * (t - cmid), NEG_INF)
        e = jnp.exp(s - jnp.max(s, axis=-1, keepdims=True))
        p = jnp.where(valid, e / jnp.sum(e, axis=-1, keepdims=True), 0.0)
        oc_ref[...] = _nn(p.astype(BF16), cvb)
        nq = nrow // NSA_HEADS
        n8 = lax.broadcasted_iota(jnp.int32, (nq, nb), 1)
        forced = (n8 == 0) | (n8 == nb - 1)
        sel_rows = []
        for g in range(NSA_KV_HEADS):
            imp = jnp.zeros((nq, nb), F32)
            for j in range(HPG):
                r0 = (g * HPG + j) * nq
                imp = imp + p[r0:r0 + nq, :]
            sel = _top_blocks(imp + jnp.where(forced, FORCE_BONUS, 0.0), SLC_TOPK - 1)
            sel_rows += [sel] * HPG
        sel_ref[...] = ((jnp.concatenate(sel_rows, axis=0) - 1.0) * MASK_BIG).astype(BF16)


def _page_specs(pp, n_pages, block, plane_pair):
    tail = (0,) * (len(block) - 2)

    def one(i):
        return pl.BlockSpec(block, lambda b, c, pt: (pt[b * n_pages + c * pp + i], plane_pair) + tail)
    return [one(i) for i in range(pp)]


def _dec_cmp_call(pt_flat, q_dec, rowc, wct, bsum, nsa_t, lf_pool, pp, n_pages, past_len):
    b, nrow, _ = q_dec.shape
    nb = past_len // CMP_BLOCK
    page = nsa_t.shape[-1]
    nc = n_pages // pp
    assert nb <= LANES, "block mask lanes cover at most 128 cached blocks"
    per = lambda shape: pl.BlockSpec((None,) + shape, lambda i, c, pt: (i, 0, 0))
    cst = lambda shape: pl.BlockSpec(shape, lambda i, c, pt: (0,) * len(shape))
    grid_spec = pltpu.PrefetchScalarGridSpec(
        num_scalar_prefetch=1,
        grid=(b, nc),
        in_specs=[per((nrow, LANES)), cst(rowc.shape), cst(wct.shape), cst(bsum.shape)]
        + _page_specs(pp, n_pages, (None, 2, NSA_KV_HEADS, HEAD_DIM, page), 0)
        + _page_specs(pp, n_pages, (None, FOX_HEADS, page), 0),
        out_specs=[per((nrow, LANES)), per((nrow, nb)), per((FOX_HEADS, LANES))],
        scratch_shapes=[pltpu.VMEM((4, nb, HEAD_DIM), F32), pltpu.VMEM((FOX_HEADS, LANES), F32)],
    )
    return pl.pallas_call(
        functools.partial(_dec_cmp_kernel, pp=pp, nb=nb, past_len=past_len),
        grid_spec=grid_spec,
        out_shape=[jax.ShapeDtypeStruct((b, nrow, LANES), F32),
                   jax.ShapeDtypeStruct((b, nrow, nb), BF16),
                   jax.ShapeDtypeStruct((b, FOX_HEADS, LANES), F32)],
        compiler_params=_params(("parallel", "arbitrary")),
        name="decode_cmp",
    )(pt_flat, q_dec, rowc, wct, bsum, *([nsa_t] * pp), *([lf_pool] * pp))


def _rep_rows(x, n):
    return jnp.concatenate([jnp.broadcast_to(x[h:h + 1, :], (n, x.shape[1])) for h in range(x.shape[0])], axis=0)


def _scr_flash_t(scores, vts, stats):
    staged = []
    for s, (m_ref, l_ref, _) in zip(scores, stats):
        m_old = m_ref[:, 0:1]
        m = jnp.maximum(m_old, jnp.max(s, axis=-1, keepdims=True))
        alpha = jnp.exp(m_old - m)
        p = jnp.exp(s - m)
        l = alpha * l_ref[:, 0:1] + jnp.sum(p, axis=-1, keepdims=True)
        m_ref[...] = jnp.broadcast_to(m, m_ref.shape)
        l_ref[...] = jnp.broadcast_to(l, l_ref.shape)
        staged.append((alpha, p.astype(BF16)))
    for (alpha, p), vt, (_, _, acc_ref) in zip(staged, vts, stats):
        acc_ref[...] = alpha * acc_ref[...] + _nt(p, vt)


def _pages_t(pages, plane):
    return jnp.concatenate(
        [jnp.concatenate([p[plane, g] for g in range(NSA_KV_HEADS)], axis=0) for p in pages], axis=1).astype(BF16)


def _dec_main_kernel(pt_ref, qn_ref, qf_ref, sel_ref, rowc_ref, oc_ref, gate_ref, ftot_ref,
                     newn_ref, newf_ref, newlf_ref, wbuf_ref, wnew_ref, hot_ref, *rest, pp, nb, past_len, nq):
    del pt_ref
    slc = rest[:pp]
    fox = rest[pp:2 * pp]
    lfs = rest[2 * pp:3 * pp]
    (on_ref, of_ref, wst_ref, ms_ref, ls_ref, as_ref, mf_ref, lf_ref, af_ref, fcar_ref, ftc_ref,
     qna_ref) = rest[3 * pp:]
    c = pl.program_id(1)
    nrow = qn_ref.shape[0]
    page = lfs[0].shape[1]
    kc = pp * page
    rowc = rowc_ref[...]
    slope = rowc[:, 0:1]
    t = past_len + rowc[:, 1:2]
    qn = qn_ref[...]
    qf = qf_ref[...]

    @pl.when(c == 0)
    def _():
        for r in (ms_ref, mf_ref):
            r[...] = jnp.full(r.shape, NEG_INF, F32)
        for r in (ls_ref, as_ref, lf_ref, af_ref, fcar_ref):
            r[...] = jnp.zeros_like(r)
        f_new = ftot_ref[...] + _lane_cumsum(newlf_ref[...])
        rep = _rep_rows(f_new, nq)
        lane = lax.broadcasted_iota(jnp.int32, rep.shape, 1)
        tok = lax.broadcasted_iota(jnp.int32, rep.shape, 0) & (nq - 1)
        ftc_ref[...] = jnp.broadcast_to(
            jnp.sum(jnp.where(lane == tok, rep, 0.0), axis=-1, keepdims=True), ftc_ref.shape)
        qna_ref[:, :LANES] = qn
        qna_ref[:, LANES:] = sel_ref[...]

    ft = ftc_ref[:, 0:1]

    kts = jnp.concatenate([_pages_t(slc, 0), hot_ref[...]], axis=0)
    pos = (c * kc + lax.broadcasted_iota(jnp.int32, (1, kc), 1)).astype(F32)
    s_slc = _nn(qna_ref[...], kts) - slope * (t - pos)

    s_fox = _nn(qf, _pages_t(fox, 0))
    cs = _lane_cumsum(jnp.concatenate([p[...] for p in lfs], axis=0))
    car = fcar_ref[:, 0:1]
    fs_parts = []
    for i in range(pp):
        cs_i = cs[i * FOX_HEADS:(i + 1) * FOX_HEADS, :]
        fs_parts.append(_rep_rows(car + cs_i, nq))
        car = car + cs_i[:, page - 1:page]
    fcar_ref[...] = jnp.broadcast_to(car, fcar_ref.shape)
    fs = jnp.concatenate(fs_parts, axis=1)
    _scr_flash_t([s_slc, s_fox + (ft - fs)], [_pages_t(slc, 1), _pages_t(fox, 1)],
                 [(ms_ref, ls_ref, as_ref), (mf_ref, lf_ref, af_ref)])

    @pl.when(c == pl.num_programs(1) - 1)
    def _():
        npad = newn_ref.shape[0]
        lane_i = lax.broadcasted_iota(jnp.int32, (1, npad), 1)
        posn = (past_len + lane_i).astype(F32)
        okn = (lane_i < nq) & (posn <= t)
        newn = newn_ref[...]
        sn = jnp.where(okn, _nt(qn, newn[:, :LANES].astype(BF16)) - slope * (t - posn), NEG_INF)
        m, l, acc = _flash_step(sn, newn[:, LANES:].astype(BF16), ms_ref[:, 0:1], ls_ref[:, 0:1], as_ref[...])
        o_s = acc / l
        newf = newf_ref[...]
        fsn = _rep_rows(fcar_ref[:, 0:1] + _lane_cumsum(newlf_ref[...]), nq)
        sf = jnp.where(okn, _nt(qf, newf[:, :LANES].astype(BF16)) + (ft - fsn), NEG_INF)
        m, l, acc = _flash_step(sf, newf[:, LANES:].astype(BF16), mf_ref[:, 0:1], lf_ref[:, 0:1], af_ref[...])
        of_ref[...] = acc / l
        wbuf = wbuf_ref[...]
        wnew = wnew_ref[...]
        nbuf = wbuf.shape[0]
        kw = jnp.concatenate([wbuf[:, :LANES], wnew[:, :LANES]], axis=0).astype(BF16)
        vw = jnp.concatenate([wbuf[:, LANES:], wnew[:, LANES:]], axis=0).astype(BF16)
        wl = lax.broadcasted_iota(jnp.int32, (1, nbuf + npad), 1)
        posw = jnp.where(wl < nbuf, past_len - nbuf + wl, past_len + wl - nbuf).astype(F32)
        dw = t - posw
        okw = (dw >= 0) & (dw < WINDOW) & ((wl < nbuf) | (wl < nbuf + nq))
        sw = jnp.where(okw, _nt(qn, kw) - slope * dw, NEG_INF)
        ew = jnp.exp(sw - jnp.max(sw, axis=-1, keepdims=True))
        o_w = _nn(ew.astype(BF16), vw) / jnp.sum(ew, axis=-1, keepdims=True)
        gate = gate_ref[...]
        on_ref[...] = gate[:, 0:1] * oc_ref[...] + gate[:, 1:2] * o_s + gate[:, 2:3] * o_w
        wst_ref[0:nbuf - nq, :] = wbuf[nq:, :]
        wst_ref[nbuf - nq:, :] = wnew[0:nq, :]


def _dec_main_call(pt_flat, qn_dec, qf_dec, sel, rowc, o_c, gates, ftot, newn, newf, newlf, wbuf, wnew, hot,
                   nsa_t, fox_t, lf_pool, pp, n_pages, past_len, nq):
    b, nrow, _ = qn_dec.shape
    nb = past_len // CMP_BLOCK
    page = nsa_t.shape[-1]
    nc = n_pages // pp
    nbuf = wbuf.shape[1]
    assert nb % 16 == 0 and nb <= LANES, "query-side block mask lanes"
    per = lambda shape: pl.BlockSpec((None,) + shape, lambda i, c, pt: (i, 0, 0))
    cst = lambda shape: pl.BlockSpec(shape, lambda i, c, pt: (0, 0))
    rl = (nrow, LANES)
    kv_block = (None, 2, NSA_KV_HEADS, HEAD_DIM, page)
    grid_spec = pltpu.PrefetchScalarGridSpec(
        num_scalar_prefetch=1,
        grid=(b, nc),
        in_specs=[per(rl), per(rl), per((nrow, nb)), cst(rowc.shape), per(rl), per(rl), per((FOX_HEADS, LANES)),
                  per(newn.shape[1:]), per(newf.shape[1:]), per(newlf.shape[1:]),
                  per((nbuf, 2 * LANES)), per(wnew.shape[1:]),
                  pl.BlockSpec((None,) + hot.shape[1:], lambda i, c, pt: (c, 0, 0))]
        + _page_specs(pp, n_pages, kv_block, 1)
        + _page_specs(pp, n_pages, kv_block, 0)
        + _page_specs(pp, n_pages, (None, FOX_HEADS, page), 0),
        out_specs=[per(rl), per(rl), per((nbuf, 2 * LANES))],
        scratch_shapes=[pltpu.VMEM(rl, F32)] * 6 + [pltpu.VMEM((FOX_HEADS, LANES), F32), pltpu.VMEM(rl, F32),
                                                    pltpu.VMEM((nrow, LANES + nb), BF16)],
    )
    return pl.pallas_call(
        functools.partial(_dec_main_kernel, pp=pp, nb=nb, past_len=past_len, nq=nq),
        grid_spec=grid_spec,
        out_shape=[jax.ShapeDtypeStruct((b,) + rl, F32), jax.ShapeDtypeStruct((b,) + rl, F32),
                   jax.ShapeDtypeStruct((b, nbuf, 2 * LANES), F32)],
        compiler_params=_params(("parallel", "arbitrary")),
        name="decode_main",
    )(pt_flat, qn_dec, qf_dec, sel, rowc, o_c, gates, ftot, newn, newf, newlf, wbuf, wnew, hot,
      *([nsa_t] * pp), *([fox_t] * pp), *([lf_pool] * pp))


def _outproj_kernel(x_ref, mod_ref, g_ref, on_ref, of_ref, wm_ref, wa_ref, wb_ref, wo_ref, o_ref):
    x = x_ref[...]
    nb, nt, d = x.shape
    h = _pre(x, mod_ref, g_ref, 1).reshape(nb * nt, d).astype(BF16)
    gm = jax.nn.sigmoid(_nn(h, wm_ref[...]))
    y = gm[:, :d] * _nn(on_ref[...], wa_ref[...]) + gm[:, d:] * _nn(of_ref[...], wb_ref[...])
    mix = _nn(y.astype(BF16), wo_ref[...]).reshape(nb, nt, d)
    o_ref[...] = x + mod_ref[:, 5:6, :] * mix


def _outproj_call(x3, mod, g_norm, o_n, o_f, wm, wa, wb, wo, blk):
    bs, lt, d = x3.shape
    nb, nt = blk
    rows = nb * nt
    nj = lt // nt
    tok = pl.BlockSpec((rows, o_n.shape[1]), lambda i, j: (i * nj + j, 0))
    return pl.pallas_call(
        _outproj_kernel,
        grid=(bs // nb, nj),
        in_specs=[pl.BlockSpec((nb, nt, d), lambda i, j: (i, j, 0)),
                  pl.BlockSpec((nb, 3 * N_SUB, d), lambda i, j: (i, 0, 0)),
                  _const_spec(g_norm.shape), tok, tok,
                  _const_spec(wm.shape), _const_spec(wa.shape), _const_spec(wb.shape), _const_spec(wo.shape)],
        out_specs=pl.BlockSpec((nb, nt, d), lambda i, j: (i, j, 0)),
        out_shape=jax.ShapeDtypeStruct(x3.shape, F32),
        compiler_params=_params(("parallel", "parallel")),
        name="out_proj",
    )(x3, mod, g_norm, o_n, o_f, wm, wa, wb, wo)


def _pair_major_cols(w):
    d = w.shape[0]
    return w.reshape(d, NSA_HEADS, HEAD_DIM)[:, jnp.array(PAIR_PERM), :].reshape(d, NSA_HEADS * HEAD_DIM)


def _pair_major_rows(w):
    d = w.shape[1]
    return w.reshape(NSA_HEADS, HEAD_DIM, d)[jnp.array(PAIR_PERM)].reshape(NSA_HEADS * HEAD_DIM, d)


def _prep_weights(w_in, b_fgt, g_q_nsa, g_k_nsa, g_q_fox, g_k_fox, w_cmp):
    d = w_in.shape[0]
    hq = NSA_HEADS * HEAD_DIM
    o = 0
    w_qn = w_in[:, o:o + hq]; o += hq
    w_kvn = w_in[:, o:o + 3 * 2 * LANES]; o += 3 * 2 * LANES
    w_gt = w_in[:, o:o + 3 * NSA_HEADS]; o += 3 * NSA_HEADS
    w_qf = w_in[:, o:o + hq]; o += hq
    w_kvf = w_in[:, o:o + 2 * LANES]; o += 2 * LANES
    w_ff = w_in[:, o:o + FOX_HEADS]; o += FOX_HEADS
    w_merge = w_in[:, o:]
    k_n = [w_kvn[:, br * 256:br * 256 + LANES] for br in range(3)]
    v_n = [w_kvn[:, br * 256 + LANES:(br + 1) * 256] for br in range(3)]
    small = jnp.concatenate([w_ff, w_gt, jnp.zeros((d, LANES - FOX_HEADS - 3 * NSA_HEADS), w_in.dtype)], axis=1)
    w1 = jnp.concatenate([_pair_major_cols(w_qn), _pair_major_cols(w_qf), k_n[0], k_n[1], k_n[2], w_kvf[:, :LANES],
                          v_n[0], v_n[1], v_n[2], w_kvf[:, LANES:], small], axis=1).astype(BF16)
    scale = HEAD_DIM ** -0.5
    gains = jnp.concatenate([jnp.tile(g_q_nsa * scale, NSA_HEADS), jnp.tile(g_q_fox * scale, FOX_HEADS),
                             jnp.tile(g_k_nsa[0], 2), jnp.tile(g_k_nsa[1], 2), jnp.tile(g_k_nsa[2], 2),
                             jnp.tile(g_k_fox, 2)]).reshape(1, N_NORMED * LANES)
    bf_pad = jnp.concatenate([b_fgt, jnp.zeros((LANES - FOX_HEADS,), F32)]).reshape(1, LANES)
    lane = jnp.arange(LANES)
    seg = (lane[:, None] // HEAD_DIM == lane[None, :] // HEAD_DIM).astype(BF16)
    wc = jnp.concatenate([w_cmp[0], w_cmp[0], w_cmp[1], w_cmp[1]], axis=1)
    return w1, gains, bf_pad, seg, wc, w_merge.astype(BF16)


def _position_lanes(l):
    pos = jnp.arange(l)
    lane = jnp.arange(LANES)[None, :]
    blk = (pos // CMP_BLOCK)[:, None]
    kx = jnp.where(lane < CMP_BLOCK, (lane == blk).astype(F32),
                   jnp.where((lane == XL_ONE_A) | (lane == XL_ONE_B), 1.0,
                             jnp.where(lane == XL_HI, blk.astype(F32),
                                       jnp.where(lane == XL_LO, (pos % CMP_BLOCK)[:, None].astype(F32), 0.0))))
    n = jnp.arange(LANES)[:, None]
    cx = jnp.where((lane == XL_ONE_A) | (lane == XL_ONE_B), 1.0,
                   jnp.where(lane == XL_HI, n.astype(F32), jnp.where(lane == XL_LO, 0.5 * (CMP_BLOCK - 1), 0.0)))
    return kx.astype(BF16), cx.astype(BF16)


def _to_decode_rows(q, b, nq):
    q5 = q.reshape(b, nq, HPG, 2, HEAD_DIM)
    eye = jnp.eye(2, dtype=q.dtype)
    return jnp.einsum('btjgd,gk->bgjtkd', q5, eye).reshape(b, NSA_HEADS * nq, LANES)


def _from_decode_rows(o, b, nq):
    o6 = o.reshape(b, 2, HPG, nq, 2, HEAD_DIM)
    own = jnp.stack([o6[:, 0, :, :, 0], o6[:, 1, :, :, 1]], axis=1)
    return own.transpose(0, 3, 1, 2, 4).reshape(b * nq, NSA_HEADS * HEAD_DIM)


def _pad_rows(x, n):
    return jnp.pad(x, ((0, 0), (0, n - x.shape[1]), (0, 0)))


def kernel(x_prompt, x_sample, cache_nsa_kv, cache_fox_kv, cache_fox_logf, state_win_kv, page_table,
           c_prompt, c_sample, w_ada, b_ada, g_norm, w_ffn_up, w_ffn_down, w_in, b_fgt,
           g_q_nsa, g_k_nsa, g_q_fox, g_k_fox, w_cmp, w_br_nsa, w_br_fox, w_out):
    assert w_ada.shape[0] == 1, "single-layer trunk"
    bp, lp, d = x_prompt.shape
    bs, nq, _ = x_sample.shape
    n_pool, page = cache_nsa_kv.shape[1:3]
    n_pages = page_table.shape[1]
    past_len = n_pages * page
    nbuf = state_win_kv.shape[2]
    dff = w_ffn_down.shape[2]
    assert nq % 8 == 0 and nq & (nq - 1) == 0 and nq <= CMP_BLOCK and past_len // CMP_BLOCK >= SLC_TOPK
    assert nbuf == WINDOW and lp % 512 == 0

    mod = _mod_call(jnp.concatenate([c_prompt, c_sample], axis=0), w_ada[0], b_ada[0]).reshape(bp + bs, 3 * N_SUB, d)
    mod_p, mod_s = mod[:bp], mod[bp:]
    gn = g_norm[0]
    up = w_ffn_up[0].astype(BF16)
    ffn_w = [(up[i, :, :dff], up[i, :, dff:], w_ffn_down[0, i].astype(BF16)) for i in range(2)]
    w1, gains, bf_pad, seg, wc, w_merge = _prep_weights(
        w_in[0], b_fgt[0], g_q_nsa[0], g_k_nsa[0], g_q_fox[0], g_k_fox[0], w_cmp[0])
    wo = w_out[0].astype(BF16)
    wa_nat, wb_nat = w_br_nsa[0].astype(BF16), w_br_fox[0].astype(BF16)
    wa_pm, wb_pm = _pair_major_rows(wa_nat), _pair_major_rows(wb_nat)

    pblk = (1, 512)
    tq, tk = 256, 256
    x1 = _ffn_call(x_prompt, mod_p, gn, *ffn_w[0], 0, pblk)
    qn, qf, nsa_new, win_new, fox_new, small, kvb, cmpb = _inproj_call(
        x1, mod_p, gn, w1, gains, bf_pad, seg, wc, pblk, True)
    small3 = small.reshape(bp, lp, LANES)
    logf_p = small3[:, :, :FOX_HEADS]
    kv3 = kvb.reshape(bp, lp, 768)
    kaug_f, qx_f = _cumsum_call(small3, kv3, tq)

    ones_rows = (jnp.arange(VT_ROWS - LANES) == 0).astype(BF16)[None, None, :, None]

    def dims_major(c):
        vt = kv3[:, :, c * LANES:(c + 1) * LANES].reshape(bp, lp // tk, tk, LANES).transpose(0, 1, 3, 2)
        return jnp.concatenate([vt, jnp.broadcast_to(ones_rows, (bp, lp // tk, VT_ROWS - LANES, tk))], axis=2)

    kx, cx = _position_lanes(lp)
    o_n = _nsa_prompt_call(qn.reshape(bp, lp, 512), kv3, kx, dims_major(1), dims_major(3),
                           cmpb.reshape(bp, lp // CMP_BLOCK, 256), cx[:lp // CMP_BLOCK], small3, tq, tk)
    o_f = _fox_prompt_call(qf.reshape(bp, lp, 512), qx_f, kaug_f, dims_major(5), tq, tk)
    x2 = _outproj_call(x1, mod_p, gn, o_n.reshape(bp * lp, 512), o_f.reshape(bp * lp, 512),
                       w_merge, wa_pm, wb_pm, wo, pblk)
    y_prompt = _ffn_call(x2, mod_p, gn, *ffn_w[1], 2, pblk)
    def rows_major(t, planes, rows):
        return t.reshape(bp, planes, NSA_KV_HEADS, HEAD_DIM, rows).transpose(0, 4, 1, 2, 3)[None]

    nwin = min(WINDOW, lp)
    nsa_p = rows_major(nsa_new, 4, lp)
    fox_p = rows_major(fox_new, 2, lp)
    win_p = rows_major(win_new[:, :, lp - nwin:], 2, nwin)

    sblk = (min(bs, 64), nq)
    pp = next(p for p in (16, 8, 4, 2, 1) if n_pages % p == 0)
    x1s = _ffn_call(x_sample, mod_s, gn, *ffn_w[0], 0, sblk)
    qn_s, qf_s, nsa_s, win_s, fox_s, small_s, _ = _inproj_call(
        x1s, mod_s, gn, w1, gains, bf_pad, seg, wc, sblk, False)
    nrow = NSA_HEADS * nq
    qn_dec = _to_decode_rows(qn_s, bs, nq)
    qf_dec = _to_decode_rows(qf_s, bs, nq)
    rows = jnp.arange(nrow)
    rowc = jnp.zeros((nrow, LANES), F32).at[:, 0].set(jnp.array(SLOPES, F32)[rows // nq]).at[:, 1].set(
        (rows % nq).astype(F32))
    sm3 = small_s.reshape(bs, nq, LANES)
    gates = sm3[:, :, FOX_HEADS:FOX_HEADS + 3 * NSA_HEADS].reshape(bs, nq, 3, NSA_HEADS)
    gates = jnp.pad(gates.transpose(0, 3, 1, 2).reshape(bs, nrow, 3), ((0, 0), (0, 0), (0, LANES - 3)))
    newlf = _pad_rows(sm3[:, :, :FOX_HEADS], LANES).transpose(0, 2, 1)
    newn = _pad_rows(nsa_s.reshape(bs, nq, 512)[:, :, 256:], LANES)
    newf = _pad_rows(fox_s.reshape(bs, nq, 256), LANES)
    wnew = _pad_rows(win_s.reshape(bs, nq, 256), LANES)
    nsa_t = cache_nsa_kv[0].transpose(0, 2, 3, 4, 1)
    fox_t = cache_fox_kv[0].transpose(0, 2, 3, 4, 1)
    wct = jnp.tile(w_cmp[0].transpose(0, 2, 1), (1, 1, page // CMP_BLOCK))
    kc = pp * page
    col_blk = jnp.arange(kc) // CMP_BLOCK
    bsum = (jnp.arange(kc // CMP_BLOCK)[:, None] == col_blk[None, :]).astype(BF16)
    nblocks = past_len // CMP_BLOCK
    hot = (jnp.arange(nblocks)[None, :, None]
           == (jnp.arange(n_pages // pp)[:, None, None] * (kc // CMP_BLOCK) + col_blk[None, None, :])).astype(BF16)
    lf_pool = cache_fox_logf[0].astype(F32).transpose(0, 2, 1)
    wbuf = state_win_kv[0].reshape(bs, nbuf, 256)
    pt_flat = page_table.reshape(-1).astype(jnp.int32)
    o_c, sel, ftot = _dec_cmp_call(pt_flat, qn_dec, rowc, wct, bsum, nsa_t, lf_pool, pp, n_pages, past_len)
    on_dec, of_dec, wst = _dec_main_call(pt_flat, qn_dec, qf_dec, sel, rowc, o_c, gates, ftot, newn, newf, newlf,
                                         wbuf, wnew, hot, nsa_t, fox_t, lf_pool, pp, n_pages, past_len, nq)
    o_ns = _from_decode_rows(on_dec, bs, nq).astype(BF16)
    o_fs = _from_decode_rows(of_dec, bs, nq).astype(BF16)
    x2s = _outproj_call(x1s, mod_s, gn, o_ns, o_fs, w_merge, wa_nat, wb_nat, wo, sblk)
    y_sample = _ffn_call(x2s, mod_s, gn, *ffn_w[1], 2, sblk)
    nsa_o = nsa_s.reshape(1, bs, nq, 4, NSA_KV_HEADS, HEAD_DIM)
    fox_o = fox_s.reshape(1, bs, nq, 2, FOX_KV_HEADS, HEAD_DIM)
    win_o = wst.reshape(1, bs, nbuf, 2, NSA_KV_HEADS, HEAD_DIM)

    return (y_prompt, y_sample, nsa_p, nsa_o, fox_p, fox_o,
            logf_p[None], sm3[None, :, :, :FOX_HEADS], win_p, win_o)
```

```python
import functools

import jax
import jax.numpy as jnp
from jax import lax
from jax.experimental import pallas as pl
from jax.experimental.pallas import tpu as pltpu

HEAD_DIM = 64
NSA_HEADS = 8
NSA_KV_HEADS = 2
FOX_HEADS = 8
FOX_KV_HEADS = 2
HPG = 4
CMP_BLOCK = 64
CMP_SHIFT = 6
SLC_TOPK = 8
WINDOW = 512
N_SUB = 3
RMS_EPS = 1e-6
NEG_INF = -1e30
FORCE_BONUS = 1e4
LANES = 128
PAIR_PERM = (0, 4, 1, 5, 2, 6, 3, 7)
SLOPES = tuple(2.0 ** -(h + 1) for h in range(NSA_HEADS))
VMEM_LIMIT = 56 * 1024 * 1024

F32 = jnp.float32
BF16 = jnp.bfloat16


def _nt(a, b):
    return lax.dot_general(a, b, (((1,), (1,)), ((), ())), preferred_element_type=F32)


def _nn(a, b):
    return jnp.dot(a, b, preferred_element_type=F32)


def _split3(x):
    hi = x.astype(BF16)
    r1 = x - hi.astype(F32)
    mid = r1.astype(BF16)
    lo = (r1 - mid.astype(F32)).astype(BF16)
    return hi, mid, lo


def _params(sem):
    return pltpu.CompilerParams(dimension_semantics=sem, vmem_limit_bytes=VMEM_LIMIT)


def _const_spec(shape):
    nd = len(shape)
    return pl.BlockSpec(shape, lambda *_: (0,) * nd, pipeline_mode=pl.Buffered(1))


def _mod_kernel(c_ref, w_ref, b_ref, o_ref):
    c = c_ref[...]
    a = (c * jax.nn.sigmoid(c)).astype(BF16)
    o_ref[...] = _nn(a, w_ref[...].astype(BF16)) + b_ref[...]


def _mod_call(c_all, w_ada, b_ada):
    bc, d = c_all.shape
    n = w_ada.shape[1]
    tn = 1024 if n % 1024 == 0 else n
    return pl.pallas_call(
        _mod_kernel,
        grid=(n // tn,),
        in_specs=[pl.BlockSpec((bc, d), lambda i: (0, 0)),
                  pl.BlockSpec((d, tn), lambda i: (0, i)),
                  pl.BlockSpec((1, tn), lambda i: (0, i))],
        out_specs=pl.BlockSpec((bc, tn), lambda i: (0, i)),
        out_shape=jax.ShapeDtypeStruct((bc, n), F32),
        compiler_params=_params(("parallel",)),
        name="adaln_mod",
    )(c_all, w_ada, b_ada.reshape(1, n))


def _pre(x, mod_ref, g_ref, sub):
    ms = jnp.mean(x * x, axis=-1, keepdims=True)
    y = x * lax.rsqrt(ms + RMS_EPS) * g_ref[sub:sub + 1, :]
    return y * (1.0 + mod_ref[:, 3 * sub + 1:3 * sub + 2, :]) + mod_ref[:, 3 * sub:3 * sub + 1, :]


def _ffn_kernel(x_ref, mod_ref, g_ref, wa_ref, wb_ref, wd_ref, o_ref, act_ref, *, sub, fc):
    x = x_ref[...]
    nb, nt, d = x.shape
    h = _pre(x, mod_ref, g_ref, sub).reshape(nb * nt, d).astype(BF16)
    dff = wa_ref.shape[1]
    for c in range(dff // fc):
        a = _nn(h, wa_ref[:, c * fc:(c + 1) * fc])
        b = _nn(h, wb_ref[:, c * fc:(c + 1) * fc])
        act_ref[:, c * fc:(c + 1) * fc] = (a * jax.nn.sigmoid(a) * b).astype(BF16)
    y = _nn(act_ref[...], wd_ref[...]).reshape(nb, nt, d)
    o_ref[...] = x + 0.5 * mod_ref[:, 3 * sub + 2:3 * sub + 3, :] * y


def _ffn_call(x3, mod, g_norm, wa, wb, wd, sub, blk):
    bs, lt, d = x3.shape
    nb, nt = blk
    dff = wa.shape[1]
    fc = 256 if dff % 256 == 0 else dff
    grid = (bs // nb, lt // nt)
    return pl.pallas_call(
        functools.partial(_ffn_kernel, sub=sub, fc=fc),
        grid=grid,
        in_specs=[pl.BlockSpec((nb, nt, d), lambda i, j: (i, j, 0)),
                  pl.BlockSpec((nb, 3 * N_SUB, d), lambda i, j: (i, 0, 0)),
                  _const_spec(g_norm.shape), _const_spec(wa.shape), _const_spec(wb.shape),
                  _const_spec(wd.shape)],
        out_specs=pl.BlockSpec((nb, nt, d), lambda i, j: (i, j, 0)),
        out_shape=jax.ShapeDtypeStruct(x3.shape, F32),
        scratch_shapes=[pltpu.VMEM((nb * nt, dff), BF16)],
        compiler_params=_params(("parallel", "parallel")),
        name="ffn_sub%d" % sub,
    )(x3, mod, g_norm, wa, wb, wd)


N_NORMED = 12


def _inproj_kernel(x_ref, mod_ref, g_ref, w_ref, gain_ref, bf_ref, seg_ref, *rest, with_cmp):
    if with_cmp:
        wc_ref, qn_ref, qf_ref, nsa_ref, win_ref, fox_ref, sm_ref, kv_ref, cmp_ref = rest
    else:
        qn_ref, qf_ref, nsa_ref, win_ref, fox_ref, sm_ref, kv_ref = rest
    x = x_ref[...]
    nb, nt, d = x.shape
    h = _pre(x, mod_ref, g_ref, 1).reshape(nb * nt, d).astype(BF16)
    z = _nn(h, w_ref[...])
    seg = seg_ref[...]
    normed = []
    for c in range(N_NORMED):
        zc = z[:, c * LANES:(c + 1) * LANES]
        zz = zc * zc
        hi = zz.astype(BF16)
        lo = (zz - hi.astype(F32)).astype(BF16)
        ss = _nn(hi, seg) + _nn(lo, seg)
        normed.append(zc * lax.rsqrt(ss * (1.0 / HEAD_DIM) + RMS_EPS) * gain_ref[:, c * LANES:(c + 1) * LANES])
    raw = [z[:, (N_NORMED + c) * LANES:(N_NORMED + c + 1) * LANES] for c in range(4)]
    k_cmp, k_slc, k_win, k_fox = normed[8:12]
    v_cmp, v_slc, v_win, v_fox = raw
    for j in range(4):
        qn_ref[:, j * LANES:(j + 1) * LANES] = normed[j].astype(BF16)
        qf_ref[:, j * LANES:(j + 1) * LANES] = normed[4 + j].astype(BF16)
    def put(ref, c, t):
        if with_cmp:
            ref[c * LANES:(c + 1) * LANES, :] = t.T
        else:
            ref[:, c * LANES:(c + 1) * LANES] = t

    for c, t in enumerate((k_cmp, v_cmp, k_slc, v_slc)):
        put(nsa_ref, c, t)
    put(win_ref, 0, k_win)
    put(win_ref, 1, v_win)
    put(fox_ref, 0, k_fox)
    put(fox_ref, 1, v_fox)
    for c, t in enumerate((k_slc, v_slc, k_win, v_win, k_fox, v_fox)):
        kv_ref[:, c * LANES:(c + 1) * LANES] = t.astype(BF16)
    zs = z[:, 16 * LANES:17 * LANES]
    lane = lax.broadcasted_iota(jnp.int32, zs.shape, 1)
    zf = zs + bf_ref[...]
    logf = jnp.minimum(zf, 0.0) - jnp.log1p(jnp.exp(-jnp.abs(zf)))
    sm_ref[...] = jnp.where(lane < FOX_HEADS, logf, jax.nn.sigmoid(zs))
    if with_cmp:
        rows = nb * nt
        kvc = jnp.concatenate([k_cmp, v_cmp], axis=1).reshape(rows // CMP_BLOCK, CMP_BLOCK, 2 * LANES)
        cmp_ref[...] = jnp.sum(kvc * wc_ref[...][None], axis=1)


def _inproj_call(x3, mod, g_norm, w1, gains, bf_pad, seg, wc, blk, with_cmp):
    bs, lt, d = x3.shape
    nb, nt = blk
    rows = nb * nt
    t = bs * lt
    grid = (bs // nb, lt // nt)
    nj = lt // nt

    def tok(cols):
        return pl.BlockSpec((rows, cols), lambda i, j: (i * nj + j, 0))

    in_specs = [pl.BlockSpec((nb, nt, d), lambda i, j: (i, j, 0)),
                pl.BlockSpec((nb, 3 * N_SUB, d), lambda i, j: (i, 0, 0)),
                _const_spec(g_norm.shape), _const_spec(w1.shape), _const_spec(gains.shape),
                _const_spec(bf_pad.shape), _const_spec(seg.shape)]
    args = [x3, mod, g_norm, w1, gains, bf_pad, seg]
    out_specs = [tok(512), tok(512), tok(512), tok(256), tok(256), tok(128), tok(768)]
    out_shape = [jax.ShapeDtypeStruct((t, 512), BF16), jax.ShapeDtypeStruct((t, 512), BF16),
                 jax.ShapeDtypeStruct((t, 512), F32), jax.ShapeDtypeStruct((t, 256), F32),
                 jax.ShapeDtypeStruct((t, 256), F32), jax.ShapeDtypeStruct((t, 128), F32),
                 jax.ShapeDtypeStruct((t, 768), BF16)]
    if with_cmp:
        assert nb == 1
        for k, cols in ((2, 512), (3, 256), (4, 256)):
            out_specs[k] = pl.BlockSpec((None, cols, nt), lambda i, j: (i, 0, j))
            out_shape[k] = jax.ShapeDtypeStruct((bs, cols, lt), F32)
        in_specs.append(_const_spec(wc.shape))
        args.append(wc)
        out_specs.append(pl.BlockSpec((rows // CMP_BLOCK, 256), lambda i, j: (i * nj + j, 0)))
        out_shape.append(jax.ShapeDtypeStruct((t // CMP_BLOCK, 256), F32))
    return pl.pallas_call(
        functools.partial(_inproj_kernel, with_cmp=with_cmp),
        grid=grid, in_specs=in_specs, out_specs=out_specs, out_shape=out_shape,
        compiler_params=_params(("parallel", "parallel")),
        name="in_proj",
    )(*args)


FQ0 = 0
FK0 = 3 * FOX_HEADS


def _cumsum_kernel(tok_ref, k_ref, kaug_ref, qx_ref, carry_ref):
    @pl.when(pl.program_id(1) == 0)
    def _():
        carry_ref[...] = jnp.zeros_like(carry_ref)

    tc = tok_ref.shape[0]
    r = lax.broadcasted_iota(jnp.int32, (tc, tc), 0)
    c = lax.broadcasted_iota(jnp.int32, (tc, tc), 1)
    lower = jnp.where(r >= c, 1.0, 0.0).astype(BF16)
    cs = carry_ref[...] + sum(_nn(lower, p) for p in _split3(tok_ref[...]))
    carry_ref[...] = cs[tc - 1:tc, :]
    hrow = lax.broadcasted_iota(jnp.int32, (LANES, LANES), 0)
    col = lax.broadcasted_iota(jnp.int32, (LANES, LANES), 1)
    placed = jnp.zeros((tc, LANES), F32)
    for p, piece in enumerate(_split3(cs)):
        tgt = 3 * hrow + p
        place = jnp.where((hrow < FOX_HEADS) & ((col == FQ0 + tgt) | (col == FK0 + tgt)), 1.0, 0.0).astype(BF16)
        placed = placed + _nn(piece, place)
    lane = lax.broadcasted_iota(jnp.int32, (tc, LANES), 1)
    in_q = lane < FK0
    in_k = (lane >= FK0) & (lane < 2 * FK0)
    qx_ref[...] = jnp.where(in_q, placed, jnp.where(in_k, 1.0, 0.0)).astype(BF16)
    kaug_ref[:, :LANES] = k_ref[...]
    kaug_ref[:, LANES:] = jnp.where(in_q, 1.0, jnp.where(in_k, -placed, 0.0)).astype(BF16)


def _cumsum_call(small3, kv3, tc):
    b, l, _ = small3.shape
    return pl.pallas_call(
        _cumsum_kernel,
        grid=(b, l // tc),
        in_specs=[pl.BlockSpec((None, tc, LANES), lambda i, j: (i, j, 0)),
                  pl.BlockSpec((None, tc, LANES), lambda i, j: (i, j, 4))],
        out_specs=[pl.BlockSpec((None, tc, 2 * LANES), lambda i, j: (i, j, 0)),
                   pl.BlockSpec((None, tc, LANES), lambda i, j: (i, j, 0))],
        out_shape=[jax.ShapeDtypeStruct((b, l, 2 * LANES), BF16),
                   jax.ShapeDtypeStruct((b, l, LANES), BF16)],
        scratch_shapes=[pltpu.VMEM((1, LANES), F32)],
        compiler_params=_params(("parallel", "arbitrary")),
        name="fox_cumsum",
    )(small3, kv3)


def _flash_step(s, v, m, l, acc):
    m_new = jnp.maximum(m, jnp.max(s, axis=-1, keepdims=True))
    alpha = jnp.exp(m - m_new)
    p = jnp.exp(s - m_new)
    l = alpha * l + jnp.sum(p, axis=-1, keepdims=True)
    acc = alpha * acc + _nn(p.astype(BF16), v)
    return m_new, l, acc


def _flash_init(rows):
    return (jnp.full((rows, 1), NEG_INF, F32), jnp.zeros((rows, 1), F32), jnp.zeros((rows, LANES), F32))


def _group_masked(qp, g):
    lane = lax.broadcasted_iota(jnp.int32, qp.shape, 1)
    keep = (lane < HEAD_DIM) if g == 0 else (lane >= HEAD_DIM)
    return jnp.where(keep, qp, jnp.zeros_like(qp))


def _top_blocks(score, n_pick, axis=1):
    idx = lax.broadcasted_iota(jnp.int32, score.shape, axis).astype(F32)
    sel = jnp.zeros(score.shape, F32)
    for _ in range(n_pick):
        mx = jnp.max(score, axis=axis, keepdims=True)
        first = jnp.min(jnp.where(score == mx, idx, 1e9), axis=axis, keepdims=True)
        pick = idx == first
        sel = jnp.where(pick, 1.0, sel)
        score = jnp.where(pick, -jnp.inf, score)
    return sel


XL_ONE_A = CMP_BLOCK
XL_ONE_B = CMP_BLOCK + 1
XL_HI = CMP_BLOCK + 2
XL_LO = CMP_BLOCK + 3
MASK_BIG = 1e30
N_FREE = SLC_TOPK - 3


VT_ROWS = LANES + 8


def _col_max(s):
    k = s.shape[0]
    if k % 64 == 0:
        s = jnp.max(s.reshape(8, k // 8, s.shape[1]), axis=0)
    return jnp.max(s, axis=0, keepdims=True)


def _heads_scores(k, qa_ref, ok):
    out = []
    for h in range(qa_ref.shape[0]):
        s = _nt(k, qa_ref[h])
        out.append(s if ok is None else jnp.where(ok, s, NEG_INF))
    return out


def _heads_stage(scores, s_ref):
    for h, s in enumerate(scores):
        s_ref[h] = s


def _heads_consume(vt, s_ref, m_ref, acc_ref):
    nh = s_ref.shape[0]
    probs, alphas = [], []
    for h in range(nh):
        s = s_ref[h]
        m_old = m_ref[h:h + 1, :]
        m_new = jnp.maximum(m_old, _col_max(s))
        m_ref[h:h + 1, :] = m_new
        alphas.append(jnp.exp(m_old - m_new))
        probs.append(jnp.exp(s - m_new).astype(BF16))
    for h in range(nh):
        acc_ref[h] = alphas[h] * acc_ref[h] + _nn(vt, probs[h])


def _heads_run(k_of, vt_ref, qa_ref, s_ref, m_ref, acc_ref, lo, qi, mask_of, mask_all, need_ref=None):
    m_ref[...] = jnp.full(m_ref.shape, NEG_INF, F32)
    acc_ref[...] = jnp.zeros_like(acc_ref)
    _heads_stage(_heads_scores(k_of(qi), qa_ref, mask_of(qi)), s_ref)

    def step(kj, staged):
        new = _heads_scores(k_of(kj), qa_ref, mask_of(kj) if mask_all else None)
        _heads_consume(vt_ref[staged], s_ref, m_ref, acc_ref)
        _heads_stage(new, s_ref)
        return kj

    def body(kj, staged):
        if need_ref is None:
            return step(kj, staged)
        return lax.cond(need_ref[kj] > 0, lambda st: step(kj, st), lambda st: st, staged)

    staged = lax.fori_loop(lo, qi, body, qi)
    _heads_consume(vt_ref[staged], s_ref, m_ref, acc_ref)


def _attn_scratch(tq, tk):
    return [pltpu.VMEM((NSA_HEADS, tq, 2 * LANES), BF16), pltpu.VMEM((NSA_HEADS, tq, 2 * LANES), BF16),
            pltpu.VMEM((NSA_HEADS, tq), F32), pltpu.VMEM((NSA_HEADS, tk, tq), F32),
            pltpu.VMEM((NSA_HEADS, VT_ROWS, tq), F32), pltpu.VMEM((NSA_HEADS, LANES, tq), F32)]


def _finish_t(acc):
    return acc[:LANES, :] / acc[LANES:LANES + 1, :]


def _nsa_prompt_kernel(q_ref, ks_ref, kw_ref, kx_ref, vst_ref, vwt_ref, cmp_ref, cx_ref, sm_ref, o_ref,
                       qas_ref, qaw_ref, m_ref, s_ref, acc_ref, out_ref, need_ref, *, tq, tk, nb):
    qi = pl.program_id(1)
    lane = lax.broadcasted_iota(jnp.int32, (tq, LANES), 1)
    t2 = qi * tq + lax.broadcasted_iota(jnp.int32, (tq, LANES), 0)
    t_hi = lax.shift_right_logical(t2, CMP_SHIFT).astype(F32)
    t_lo = (t2 & (CMP_BLOCK - 1)).astype(F32)
    alibi_q = jnp.where(lane == XL_ONE_A, -CMP_BLOCK * t_hi,
                        jnp.where(lane == XL_ONE_B, -t_lo,
                                  jnp.where(lane == XL_HI, float(CMP_BLOCK), jnp.where(lane == XL_LO, 1.0, 0.0))))
    krow = lax.broadcasted_iota(jnp.int32, (tk, tq), 0)
    qcol = lax.broadcasted_iota(jnp.int32, (tk, tq), 1)
    rows = lax.broadcasted_iota(jnp.int32, (LANES, tq), 0)

    cmpv = cmp_ref[...]
    ck_aug = jnp.concatenate([cmpv[:, :LANES].astype(BF16), cx_ref[...]], axis=1)
    cvt = jnp.concatenate([cmpv[:, LANES:], jnp.zeros((LANES - nb, LANES), F32)],
                          axis=0).T[:, :nb].astype(BF16)
    brow = lax.broadcasted_iota(jnp.int32, (nb, tq), 0)
    tcol = qi * tq + lax.broadcasted_iota(jnp.int32, (nb, tq), 1)
    valid_c = (brow + 1) * CMP_BLOCK - 1 <= tcol
    cur = lax.shift_right_logical(tcol, CMP_SHIFT)
    forced = (brow == 0) | (brow == cur) | (brow == cur - 1)
    free = (brow * CMP_BLOCK <= tcol) & jnp.logical_not(forced)
    smt = sm_ref[...].T

    picked = jnp.zeros((1, LANES), F32)
    for g in range(NSA_KV_HEADS):
        imp = jnp.zeros((nb, tq), F32)
        qx = []
        for j in range(HPG):
            h = g * HPG + j
            qx.append(SLOPES[h] * alibi_q)
            qaw_ref[h, :, :LANES] = _group_masked(q_ref[:, j * LANES:(j + 1) * LANES], g)
            qaw_ref[h, :, LANES:] = qx[j].astype(BF16)
            s = jnp.where(valid_c, _nt(ck_aug, qaw_ref[h]), NEG_INF)
            e = jnp.exp(s - jnp.max(s, axis=0, keepdims=True))
            p = jnp.where(valid_c, e / jnp.sum(e, axis=0, keepdims=True), 0.0)
            imp = imp + p
            c0 = FOX_HEADS + h
            out_ref[h] = smt[c0:c0 + 1, :] * _nn(cvt, p.astype(BF16))
        score = jnp.where(free, imp, jnp.where(forced, 2 * NEG_INF, NEG_INF))
        sel = jnp.where(forced, 1.0, _top_blocks(score, N_FREE, axis=0))
        sel = jnp.concatenate([sel, jnp.zeros((LANES - nb, tq), F32)], axis=0).T
        sel_q = jnp.where(lane < CMP_BLOCK, (sel - 1.0) * MASK_BIG, 0.0)
        picked = jnp.maximum(picked, jnp.max(sel, axis=0, keepdims=True))
        for j in range(HPG):
            h = g * HPG + j
            qas_ref[h, :, :LANES] = qaw_ref[h, :, :LANES]
            qas_ref[h, :, LANES:] = (qx[j] + sel_q).astype(BF16)

    assert tq == tk, "the pipelined key loop walks one key tile per query tile"
    blane = lax.broadcasted_iota(jnp.int32, (1, LANES), 1)
    bpt = tk // CMP_BLOCK
    for kt in range(need_ref.shape[0]):
        hit = jnp.max(jnp.where((blane >= kt * bpt) & (blane < (kt + 1) * bpt), picked, 0.0), axis=1, keepdims=True)
        need_ref[kt] = (hit[0, 0] > 0.0).astype(jnp.int32)

    def run_branch(qa_ref, k_ref, vt_ref, lo, window, gate_row, need=None):
        def k_of(kj):
            off = pl.multiple_of(kj * tk, tk)
            return jnp.concatenate([k_ref[pl.ds(off, tk), :], kx_ref[pl.ds(off, tk), :]], axis=1)

        def mask_of(kj):
            dist = (qi - kj) * tq + qcol - krow
            return (dist >= 0) & (dist < WINDOW) if window else dist >= 0

        _heads_run(k_of, vt_ref, qa_ref, s_ref, m_ref, acc_ref, lo, qi, mask_of, window, need)
        for h in range(NSA_HEADS):
            r = gate_row + h
            out_ref[h] = out_ref[h] + smt[r:r + 1, :] * _finish_t(acc_ref[h])

    run_branch(qas_ref, ks_ref, vst_ref, 0, False, FOX_HEADS + NSA_HEADS, need_ref)
    run_branch(qaw_ref, kw_ref, vwt_ref, jnp.maximum(qi - (WINDOW + tk - 1) // tk, 0), True,
               FOX_HEADS + 2 * NSA_HEADS)
    for j in range(HPG):
        pair_t = jnp.where(rows < HEAD_DIM, out_ref[j], out_ref[HPG + j])
        o_ref[:, j * LANES:(j + 1) * LANES] = pair_t.T.astype(BF16)


def _nsa_prompt_call(qn3, kv3, kx, vst, vwt, cmp3, cx, small3, tq, tk):
    b, l, _ = qn3.shape
    nb = cmp3.shape[1]
    assert nb <= CMP_BLOCK, "one-hot block lanes cover at most CMP_BLOCK blocks"

    def kspec(c):
        return pl.BlockSpec((None, l, LANES), lambda i, j: (i, 0, c))

    vspec = pl.BlockSpec((None, l // tk, VT_ROWS, tk), lambda i, j: (i, 0, 0, 0))
    return pl.pallas_call(
        functools.partial(_nsa_prompt_kernel, tq=tq, tk=tk, nb=nb),
        grid=(b, l // tq),
        in_specs=[pl.BlockSpec((None, tq, 512), lambda i, j: (i, j, 0)),
                  kspec(0), kspec(2), pl.BlockSpec((l, LANES), lambda i, j: (0, 0)), vspec, vspec,
                  pl.BlockSpec((None, nb, 256), lambda i, j: (i, 0, 0)),
                  pl.BlockSpec((nb, LANES), lambda i, j: (0, 0)),
                  pl.BlockSpec((None, tq, LANES), lambda i, j: (i, j, 0))],
        out_specs=pl.BlockSpec((None, tq, 512), lambda i, j: (i, j, 0)),
        out_shape=jax.ShapeDtypeStruct((b, l, 512), BF16),
        scratch_shapes=_attn_scratch(tq, tk) + [pltpu.SMEM((l // tk,), jnp.int32)],
        compiler_params=_params(("parallel", "arbitrary")),
        name="nsa_prompt",
    )(qn3, kv3, kv3, kx, vst, vwt, cmp3, cx, small3)


def _fox_prompt_kernel(q_ref, qx_ref, k_ref, vt_ref, o_ref, qa_ref, m_ref, s_ref, acc_ref, *, tq, tk):
    assert tq == tk, "the pipelined key loop walks one key tile per query tile"
    qi = pl.program_id(1)
    lane = lax.broadcasted_iota(jnp.int32, (tq, LANES), 1)
    krow = lax.broadcasted_iota(jnp.int32, (tk, tq), 0)
    qcol = lax.broadcasted_iota(jnp.int32, (tk, tq), 1)
    rows = lax.broadcasted_iota(jnp.int32, (LANES, tq), 0)
    qx = qx_ref[...]
    for g in range(FOX_KV_HEADS):
        for j in range(HPG):
            h = g * HPG + j
            own = ((lane >= FQ0 + 3 * h) & (lane < FQ0 + 3 * h + 3)) | ((lane >= FK0 + 3 * h) & (lane < FK0 + 3 * h + 3))
            qa_ref[h, :, :LANES] = _group_masked(q_ref[:, j * LANES:(j + 1) * LANES], g)
            qa_ref[h, :, LANES:] = jnp.where(own, qx, jnp.zeros_like(qx))
    _heads_run(lambda kj: k_ref[pl.ds(pl.multiple_of(kj * tk, tk), tk), :], vt_ref, qa_ref, s_ref, m_ref, acc_ref,
               0, qi, lambda kj: (qi - kj) * tq + qcol - krow >= 0, False)
    for j in range(HPG):
        pair_t = jnp.where(rows < HEAD_DIM, _finish_t(acc_ref[j]), _finish_t(acc_ref[HPG + j]))
        o_ref[:, j * LANES:(j + 1) * LANES] = pair_t.T.astype(BF16)


def _fox_prompt_call(qf3, qx3, kaug3, vft, tq, tk):
    b, l, _ = qf3.shape
    return pl.pallas_call(
        functools.partial(_fox_prompt_kernel, tq=tq, tk=tk),
        grid=(b, l // tq),
        in_specs=[pl.BlockSpec((None, tq, 512), lambda i, j: (i, j, 0)),
                  pl.BlockSpec((None, tq, LANES), lambda i, j: (i, j, 0)),
                  pl.BlockSpec((None, l, 2 * LANES), lambda i, j: (i, 0, 0)),
                  pl.BlockSpec((None, l // tk, VT_ROWS, tk), lambda i, j: (i, 0, 0, 0))],
        out_specs=pl.BlockSpec((None, tq, 512), lambda i, j: (i, j, 0)),
        out_shape=jax.ShapeDtypeStruct((b, l, 512), BF16),
        scratch_shapes=_attn_scratch(tq, tk)[1:5],
        compiler_params=_params(("parallel", "arbitrary")),
        name="fox_prompt",
    )(qf3, qx3, kaug3, vft)


def _lane_cumsum(x):
    n = x.shape[1]
    r = lax.broadcasted_iota(jnp.int32, (n, n), 0)
    c = lax.broadcasted_iota(jnp.int32, (n, n), 1)
    upper = jnp.where(r <= c, 1.0, 0.0).astype(BF16)
    return sum(_nn(p, upper) for p in _split3(x))


def _dec_cmp_kernel(pt_ref, q_ref, rowc_ref, wct_ref, bsum_ref, *rest, pp, nb, past_len):
    del pt_ref
    pages = rest[:pp]
    lfs = rest[pp:2 * pp]
    oc_ref, sel_ref, ftot_ref, cmp_scr, ftot_scr = rest[2 * pp:]
    c = pl.program_id(1)

    @pl.when(c == 0)
    def _():
        ftot_scr[...] = jnp.zeros_like(ftot_scr)

    nblk = bsum_ref.shape[0]
    bsum = bsum_ref[...]
    for plane in range(2):
        w = wct_ref[plane]
        for g in range(NSA_KV_HEADS):
            prod = jnp.concatenate([(p[plane, g] * w).astype(BF16) for p in pages], axis=1)
            blk = _nt(bsum, prod)
            cmp_scr[2 * plane + g, pl.ds(pl.multiple_of(c * nblk, nblk), nblk), :] = blk
    tot = lfs[0][...]
    for p in lfs[1:]:
        tot = tot + p[...]
    ftot_scr[...] += jnp.broadcast_to(jnp.sum(tot, axis=-1, keepdims=True), ftot_scr.shape)

    @pl.when(c == pl.num_programs(1) - 1)
    def _():
        ftot_ref[...] = ftot_scr[...]
        ckb = jnp.concatenate([cmp_scr[0], cmp_scr[1]], axis=1).astype(BF16)
        cvb = jnp.concatenate([cmp_scr[2], cmp_scr[3]], axis=1).astype(BF16)
        nrow = q_ref.shape[0]
        rowc = rowc_ref[...]
        slope = rowc[:, 0:1]
        t = past_len + rowc[:, 1:2]
        nidx = lax.broadcasted_iota(jnp.int32, (nrow, nb), 1)
        cmid = nidx.astype(F32) * CMP_BLOCK + 0.5 * (CMP_BLOCK - 1)
        valid = cmid + 0.5 * (CMP_BLOCK - 1) <= t
        s = jnp.where(valid, _nt(q_ref[...], ckb) - slope * (t - cmid), NEG_INF)
        e = jnp.exp(s - jnp.max(s, axis=-1, keepdims=True))
        p = jnp.where(valid, e / jnp.sum(e, axis=-1, keepdims=True), 0.0)
        oc_ref[...] = _nn(p.astype(BF16), cvb)
        nq = nrow // NSA_HEADS
        n8 = lax.broadcasted_iota(jnp.int32, (nq, nb), 1)
        forced = (n8 == 0) | (n8 == nb - 1)
        sel_rows = []
        for g in range(NSA_KV_HEADS):
            imp = jnp.zeros((nq, nb), F32)
            for j in range(HPG):
                r0 = (g * HPG + j) * nq
                imp = imp + p[r0:r0 + nq, :]
            sel = _top_blocks(imp + jnp.where(forced, FORCE_BONUS, 0.0), SLC_TOPK - 1)
            sel_rows += [sel] * HPG
        sel_ref[...] = ((jnp.concatenate(sel_rows, axis=0) - 1.0) * MASK_BIG).astype(BF16)


def _page_specs(pp, n_pages, block, plane_pair):
    tail = (0,) * (len(block) - 2)

    def one(i):
        return pl.BlockSpec(block, lambda b, c, pt: (pt[b * n_pages + c * pp + i], plane_pair) + tail)
    return [one(i) for i in range(pp)]


def _dec_cmp_call(pt_flat, q_dec, rowc, wct, bsum, nsa_t, lf_pool, pp, n_pages, past_len):
    b, nrow, _ = q_dec.shape
    nb = past_len // CMP_BLOCK
    page = nsa_t.shape[-1]
    nc = n_pages // pp
    assert nb <= LANES, "block mask lanes cover at most 128 cached blocks"
    per = lambda shape: pl.BlockSpec((None,) + shape, lambda i, c, pt: (i, 0, 0))
    cst = lambda shape: pl.BlockSpec(shape, lambda i, c, pt: (0,) * len(shape))
    grid_spec = pltpu.PrefetchScalarGridSpec(
        num_scalar_prefetch=1,
        grid=(b, nc),
        in_specs=[per((nrow, LANES)), cst(rowc.shape), cst(wct.shape), cst(bsum.shape)]
        + _page_specs(pp, n_pages, (None, 2, NSA_KV_HEADS, HEAD_DIM, page), 0)
        + _page_specs(pp, n_pages, (None, FOX_HEADS, page), 0),
        out_specs=[per((nrow, LANES)), per((nrow, nb)), per((FOX_HEADS, LANES))],
        scratch_shapes=[pltpu.VMEM((4, nb, HEAD_DIM), F32), pltpu.VMEM((FOX_HEADS, LANES), F32)],
    )
    return pl.pallas_call(
        functools.partial(_dec_cmp_kernel, pp=pp, nb=nb, past_len=past_len),
        grid_spec=grid_spec,
        out_shape=[jax.ShapeDtypeStruct((b, nrow, LANES), F32),
                   jax.ShapeDtypeStruct((b, nrow, nb), BF16),
                   jax.ShapeDtypeStruct((b, FOX_HEADS, LANES), F32)],
        compiler_params=_params(("parallel", "arbitrary")),
        name="decode_cmp",
    )(pt_flat, q_dec, rowc, wct, bsum, *([nsa_t] * pp), *([lf_pool] * pp))


def _rep_rows(x, n):
    return jnp.concatenate([jnp.broadcast_to(x[h:h + 1, :], (n, x.shape[1])) for h in range(x.shape[0])], axis=0)


def _scr_flash_t(scores, vts, stats):
    staged = []
    for s, (m_ref, l_ref, _) in zip(scores, stats):
        m_old = m_ref[:, 0:1]
        m = jnp.maximum(m_old, jnp.max(s, axis=-1, keepdims=True))
        alpha = jnp.exp(m_old - m)
        p = jnp.exp(s - m)
        l = alpha * l_ref[:, 0:1] + jnp.sum(p, axis=-1, keepdims=True)
        m_ref[...] = jnp.broadcast_to(m, m_ref.shape)
        l_ref[...] = jnp.broadcast_to(l, l_ref.shape)
        staged.append((alpha, p.astype(BF16)))
    for (alpha, p), vt, (_, _, acc_ref) in zip(staged, vts, stats):
        acc_ref[...] = alpha * acc_ref[...] + _nt(p, vt)


def _pages_t(pages, plane):
    return jnp.concatenate(
        [jnp.concatenate([p[plane, g] for g in range(NSA_KV_HEADS)], axis=0) for p in pages], axis=1).astype(BF16)


def _dec_main_kernel(pt_ref, qn_ref, qf_ref, sel_ref, rowc_ref, oc_ref, gate_ref, ftot_ref,
                     newn_ref, newf_ref, newlf_ref, wbuf_ref, wnew_ref, hot_ref, *rest, pp, nb, past_len, nq):
    del pt_ref
    slc = rest[:pp]
    fox = rest[pp:2 * pp]
    lfs = rest[2 * pp:3 * pp]
    (on_ref, of_ref, wst_ref, ms_ref, ls_ref, as_ref, mf_ref, lf_ref, af_ref, fcar_ref, ftc_ref,
     qna_ref) = rest[3 * pp:]
    c = pl.program_id(1)
    nrow = qn_ref.shape[0]
    page = lfs[0].shape[1]
    kc = pp * page
    rowc = rowc_ref[...]
    slope = rowc[:, 0:1]
    t = past_len + rowc[:, 1:2]
    qn = qn_ref[...]
    qf = qf_ref[...]

    @pl.when(c == 0)
    def _():
        for r in (ms_ref, mf_ref):
            r[...] = jnp.full(r.shape, NEG_INF, F32)
        for r in (ls_ref, as_ref, lf_ref, af_ref, fcar_ref):
            r[...] = jnp.zeros_like(r)
        f_new = ftot_ref[...] + _lane_cumsum(newlf_ref[...])
        rep = _rep_rows(f_new, nq)
        lane = lax.broadcasted_iota(jnp.int32, rep.shape, 1)
        tok = lax.broadcasted_iota(jnp.int32, rep.shape, 0) & (nq - 1)
        ftc_ref[...] = jnp.broadcast_to(
            jnp.sum(jnp.where(lane == tok, rep, 0.0), axis=-1, keepdims=True), ftc_ref.shape)
        qna_ref[:, :LANES] = qn
        qna_ref[:, LANES:] = sel_ref[...]

    ft = ftc_ref[:, 0:1]

    kts = jnp.concatenate([_pages_t(slc, 0), hot_ref[...]], axis=0)
    pos = (c * kc + lax.broadcasted_iota(jnp.int32, (1, kc), 1)).astype(F32)
    s_slc = _nn(qna_ref[...], kts) - slope * (t - pos)

    s_fox = _nn(qf, _pages_t(fox, 0))
    cs = _lane_cumsum(jnp.concatenate([p[...] for p in lfs], axis=0))
    car = fcar_ref[:, 0:1]
    fs_parts = []
    for i in range(pp):
        cs_i = cs[i * FOX_HEADS:(i + 1) * FOX_HEADS, :]
        fs_parts.append(_rep_rows(car + cs_i, nq))
        car = car + cs_i[:, page - 1:page]
    fcar_ref[...] = jnp.broadcast_to(car, fcar_ref.shape)
    fs = jnp.concatenate(fs_parts, axis=1)
    _scr_flash_t([s_slc, s_fox + (ft - fs)], [_pages_t(slc, 1), _pages_t(fox, 1)],
                 [(ms_ref, ls_ref, as_ref), (mf_ref, lf_ref, af_ref)])

    @pl.when(c == pl.num_programs(1) - 1)
    def _():
        npad = newn_ref.shape[0]
        lane_i = lax.broadcasted_iota(jnp.int32, (1, npad), 1)
        posn = (past_len + lane_i).astype(F32)
        okn = (lane_i < nq) & (posn <= t)
        newn = newn_ref[...]
        sn = jnp.where(okn, _nt(qn, newn[:, :LANES].astype(BF16)) - slope * (t - posn), NEG_INF)
        m, l, acc = _flash_step(sn, newn[:, LANES:].astype(BF16), ms_ref[:, 0:1], ls_ref[:, 0:1], as_ref[...])
        o_s = acc / l
        newf = newf_ref[...]
        fsn = _rep_rows(fcar_ref[:, 0:1] + _lane_cumsum(newlf_ref[...]), nq)
        sf = jnp.where(okn, _nt(qf, newf[:, :LANES].astype(BF16)) + (ft - fsn), NEG_INF)
        m, l, acc = _flash_step(sf, newf[:, LANES:].astype(BF16), mf_ref[:, 0:1], lf_ref[:, 0:1], af_ref[...])
        of_ref[...] = acc / l
        wbuf = wbuf_ref[...]
        wnew = wnew_ref[...]
        nbuf = wbuf.shape[0]
        kw = jnp.concatenate([wbuf[:, :LANES], wnew[:, :LANES]], axis=0).astype(BF16)
        vw = jnp.concatenate([wbuf[:, LANES:], wnew[:, LANES:]], axis=0).astype(BF16)
        wl = lax.broadcasted_iota(jnp.int32, (1, nbuf + npad), 1)
        posw = jnp.where(wl < nbuf, past_len - nbuf + wl, past_len + wl - nbuf).astype(F32)
        dw = t - posw
        okw = (dw >= 0) & (dw < WINDOW) & ((wl < nbuf) | (wl < nbuf + nq))
        sw = jnp.where(okw, _nt(qn, kw) - slope * dw, NEG_INF)
        ew = jnp.exp(sw - jnp.max(sw, axis=-1, keepdims=True))
        o_w = _nn(ew.astype(BF16), vw) / jnp.sum(ew, axis=-1, keepdims=True)
        gate = gate_ref[...]
        on_ref[...] = gate[:, 0:1] * oc_ref[...] + gate[:, 1:2] * o_s + gate[:, 2:3] * o_w
        wst_ref[0:nbuf - nq, :] = wbuf[nq:, :]
        wst_ref[nbuf - nq:, :] = wnew[0:nq, :]


def _dec_main_call(pt_flat, qn_dec, qf_dec, sel, rowc, o_c, gates, ftot, newn, newf, newlf, wbuf, wnew, hot,
                   nsa_t, fox_t, lf_pool, pp, n_pages, past_len, nq):
    b, nrow, _ = qn_dec.shape
    nb = past_len // CMP_BLOCK
    page = nsa_t.shape[-1]
    nc = n_pages // pp
    nbuf = wbuf.shape[1]
    assert nb % 16 == 0 and nb <= LANES, "query-side block mask lanes"
    per = lambda shape: pl.BlockSpec((None,) + shape, lambda i, c, pt: (i, 0, 0))
    cst = lambda shape: pl.BlockSpec(shape, lambda i, c, pt: (0, 0))
    rl = (nrow, LANES)
    kv_block = (None, 2, NSA_KV_HEADS, HEAD_DIM, page)
    grid_spec = pltpu.PrefetchScalarGridSpec(
        num_scalar_prefetch=1,
        grid=(b, nc),
        in_specs=[per(rl), per(rl), per((nrow, nb)), cst(rowc.shape), per(rl), per(rl), per((FOX_HEADS, LANES)),
                  per(newn.shape[1:]), per(newf.shape[1:]), per(newlf.shape[1:]),
                  per((nbuf, 2 * LANES)), per(wnew.shape[1:]),
                  pl.BlockSpec((None,) + hot.shape[1:], lambda i, c, pt: (c, 0, 0))]
        + _page_specs(pp, n_pages, kv_block, 1)
        + _page_specs(pp, n_pages, kv_block, 0)
        + _page_specs(pp, n_pages, (None, FOX_HEADS, page), 0),
        out_specs=[per(rl), per(rl), per((nbuf, 2 * LANES))],
        scratch_shapes=[pltpu.VMEM(rl, F32)] * 6 + [pltpu.VMEM((FOX_HEADS, LANES), F32), pltpu.VMEM(rl, F32),
                                                    pltpu.VMEM((nrow, LANES + nb), BF16)],
    )
    return pl.pallas_call(
        functools.partial(_dec_main_kernel, pp=pp, nb=nb, past_len=past_len, nq=nq),
        grid_spec=grid_spec,
        out_shape=[jax.ShapeDtypeStruct((b,) + rl, F32), jax.ShapeDtypeStruct((b,) + rl, F32),
                   jax.ShapeDtypeStruct((b, nbuf, 2 * LANES), F32)],
        compiler_params=_params(("parallel", "arbitrary")),
        name="decode_main",
    )(pt_flat, qn_dec, qf_dec, sel, rowc, o_c, gates, ftot, newn, newf, newlf, wbuf, wnew, hot,
      *([nsa_t] * pp), *([fox_t] * pp), *([lf_pool] * pp))


def _outproj_kernel(x_ref, mod_ref, g_ref, on_ref, of_ref, wm_ref, wa_ref, wb_ref, wo_ref, o_ref):
    x = x_ref[...]
    nb, nt, d = x.shape
    h = _pre(x, mod_ref, g_ref, 1).reshape(nb * nt, d).astype(BF16)
    gm = jax.nn.sigmoid(_nn(h, wm_ref[...]))
    y = gm[:, :d] * _nn(on_ref[...], wa_ref[...]) + gm[:, d:] * _nn(of_ref[...], wb_ref[...])
    mix = _nn(y.astype(BF16), wo_ref[...]).reshape(nb, nt, d)
    o_ref[...] = x + mod_ref[:, 5:6, :] * mix


def _outproj_call(x3, mod, g_norm, o_n, o_f, wm, wa, wb, wo, blk):
    bs, lt, d = x3.shape
    nb, nt = blk
    rows = nb * nt
    nj = lt // nt
    tok = pl.BlockSpec((rows, o_n.shape[1]), lambda i, j: (i * nj + j, 0))
    return pl.pallas_call(
        _outproj_kernel,
        grid=(bs // nb, nj),
        in_specs=[pl.BlockSpec((nb, nt, d), lambda i, j: (i, j, 0)),
                  pl.BlockSpec((nb, 3 * N_SUB, d), lambda i, j: (i, 0, 0)),
                  _const_spec(g_norm.shape), tok, tok,
                  _const_spec(wm.shape), _const_spec(wa.shape), _const_spec(wb.shape), _const_spec(wo.shape)],
        out_specs=pl.BlockSpec((nb, nt, d), lambda i, j: (i, j, 0)),
        out_shape=jax.ShapeDtypeStruct(x3.shape, F32),
        compiler_params=_params(("parallel", "parallel")),
        name="out_proj",
    )(x3, mod, g_norm, o_n, o_f, wm, wa, wb, wo)


def _pair_major_cols(w):
    d = w.shape[0]
    return w.reshape(d, NSA_HEADS, HEAD_DIM)[:, jnp.array(PAIR_PERM), :].reshape(d, NSA_HEADS * HEAD_DIM)


def _pair_major_rows(w):
    d = w.shape[1]
    return w.reshape(NSA_HEADS, HEAD_DIM, d)[jnp.array(PAIR_PERM)].reshape(NSA_HEADS * HEAD_DIM, d)


def _prep_weights(w_in, b_fgt, g_q_nsa, g_k_nsa, g_q_fox, g_k_fox, w_cmp):
    d = w_in.shape[0]
    hq = NSA_HEADS * HEAD_DIM
    o = 0
    w_qn = w_in[:, o:o + hq]; o += hq
    w_kvn = w_in[:, o:o + 3 * 2 * LANES]; o += 3 * 2 * LANES
    w_gt = w_in[:, o:o + 3 * NSA_HEADS]; o += 3 * NSA_HEADS
    w_qf = w_in[:, o:o + hq]; o += hq
    w_kvf = w_in[:, o:o + 2 * LANES]; o += 2 * LANES
    w_ff = w_in[:, o:o + FOX_HEADS]; o += FOX_HEADS
    w_merge = w_in[:, o:]
    k_n = [w_kvn[:, br * 256:br * 256 + LANES] for br in range(3)]
    v_n = [w_kvn[:, br * 256 + LANES:(br + 1) * 256] for br in range(3)]
    small = jnp.concatenate([w_ff, w_gt, jnp.zeros((d, LANES - FOX_HEADS - 3 * NSA_HEADS), w_in.dtype)], axis=1)
    w1 = jnp.concatenate([_pair_major_cols(w_qn), _pair_major_cols(w_qf), k_n[0], k_n[1], k_n[2], w_kvf[:, :LANES],
                          v_n[0], v_n[1], v_n[2], w_kvf[:, LANES:], small], axis=1).astype(BF16)
    scale = HEAD_DIM ** -0.5
    gains = jnp.concatenate([jnp.tile(g_q_nsa * scale, NSA_HEADS), jnp.tile(g_q_fox * scale, FOX_HEADS),
                             jnp.tile(g_k_nsa[0], 2), jnp.tile(g_k_nsa[1], 2), jnp.tile(g_k_nsa[2], 2),
                             jnp.tile(g_k_fox, 2)]).reshape(1, N_NORMED * LANES)
    bf_pad = jnp.concatenate([b_fgt, jnp.zeros((LANES - FOX_HEADS,), F32)]).reshape(1, LANES)
    lane = jnp.arange(LANES)
    seg = (lane[:, None] // HEAD_DIM == lane[None, :] // HEAD_DIM).astype(BF16)
    wc = jnp.concatenate([w_cmp[0], w_cmp[0], w_cmp[1], w_cmp[1]], axis=1)
    return w1, gains, bf_pad, seg, wc, w_merge.astype(BF16)


def _position_lanes(l):
    pos = jnp.arange(l)
    lane = jnp.arange(LANES)[None, :]
    blk = (pos // CMP_BLOCK)[:, None]
    kx = jnp.where(lane < CMP_BLOCK, (lane == blk).astype(F32),
                   jnp.where((lane == XL_ONE_A) | (lane == XL_ONE_B), 1.0,
                             jnp.where(lane == XL_HI, blk.astype(F32),
                                       jnp.where(lane == XL_LO, (pos % CMP_BLOCK)[:, None].astype(F32), 0.0))))
    n = jnp.arange(LANES)[:, None]
    cx = jnp.where((lane == XL_ONE_A) | (lane == XL_ONE_B), 1.0,
                   jnp.where(lane == XL_HI, n.astype(F32), jnp.where(lane == XL_LO, 0.5 * (CMP_BLOCK - 1), 0.0)))
    return kx.astype(BF16), cx.astype(BF16)


def _to_decode_rows(q, b, nq):
    q5 = q.reshape(b, nq, HPG, 2, HEAD_DIM)
    eye = jnp.eye(2, dtype=q.dtype)
    return jnp.einsum('btjgd,gk->bgjtkd', q5, eye).reshape(b, NSA_HEADS * nq, LANES)


def _from_decode_rows(o, b, nq):
    o6 = o.reshape(b, 2, HPG, nq, 2, HEAD_DIM)
    own = jnp.stack([o6[:, 0, :, :, 0], o6[:, 1, :, :, 1]], axis=1)
    return own.transpose(0, 3, 1, 2, 4).reshape(b * nq, NSA_HEADS * HEAD_DIM)


def _pad_rows(x, n):
    return jnp.pad(x, ((0, 0), (0, n - x.shape[1]), (0, 0)))


def kernel(x_prompt, x_sample, cache_nsa_kv, cache_fox_kv, cache_fox_logf, state_win_kv, page_table,
           c_prompt, c_sample, w_ada, b_ada, g_norm, w_ffn_up, w_ffn_down, w_in, b_fgt,
           g_q_nsa, g_k_nsa, g_q_fox, g_k_fox, w_cmp, w_br_nsa, w_br_fox, w_out):
    assert w_ada.shape[0] == 1, "single-layer trunk"
    bp, lp, d = x_prompt.shape
    bs, nq, _ = x_sample.shape
    n_pool, page = cache_nsa_kv.shape[1:3]
    n_pages = page_table.shape[1]
    past_len = n_pages * page
    nbuf = state_win_kv.shape[2]
    dff = w_ffn_down.shape[2]
    assert nq % 8 == 0 and nq & (nq - 1) == 0 and nq <= CMP_BLOCK and past_len // CMP_BLOCK >= SLC_TOPK
    assert nbuf == WINDOW and lp % 512 == 0

    mod = _mod_call(jnp.concatenate([c_prompt, c_sample], axis=0), w_ada[0], b_ada[0]).reshape(bp + bs, 3 * N_SUB, d)
    mod_p, mod_s = mod[:bp], mod[bp:]
    gn = g_norm[0]
    up = w_ffn_up[0].astype(BF16)
    ffn_w = [(up[i, :, :dff], up[i, :, dff:], w_ffn_down[0, i].astype(BF16)) for i in range(2)]
    w1, gains, bf_pad, seg, wc, w_merge = _prep_weights(
        w_in[0], b_fgt[0], g_q_nsa[0], g_k_nsa[0], g_q_fox[0], g_k_fox[0], w_cmp[0])
    wo = w_out[0].astype(BF16)
    wa_nat, wb_nat = w_br_nsa[0].astype(BF16), w_br_fox[0].astype(BF16)
    wa_pm, wb_pm = _pair_major_rows(wa_nat), _pair_major_rows(wb_nat)

    pblk = (1, 512)
    tq, tk = 256, 256
    x1 = _ffn_call(x_prompt, mod_p, gn, *ffn_w[0], 0, pblk)
    qn, qf, nsa_new, win_new, fox_new, small, kvb, cmpb = _inproj_call(
        x1, mod_p, gn, w1, gains, bf_pad, seg, wc, pblk, True)
    small3 = small.reshape(bp, lp, LANES)
    logf_p = small3[:, :, :FOX_HEADS]
    kv3 = kvb.reshape(bp, lp, 768)
    kaug_f, qx_f = _cumsum_call(small3, kv3, tq)

    ones_rows = (jnp.arange(VT_ROWS - LANES) == 0).astype(BF16)[None, None, :, None]

    def dims_major(c):
        vt = kv3[:, :, c * LANES:(c + 1) * LANES].reshape(bp, lp // tk, tk, LANES).transpose(0, 1, 3, 2)
        return jnp.concatenate([vt, jnp.broadcast_to(ones_rows, (bp, lp // tk, VT_ROWS - LANES, tk))], axis=2)

    kx, cx = _position_lanes(lp)
    o_n = _nsa_prompt_call(qn.reshape(bp, lp, 512), kv3, kx, dims_major(1), dims_major(3),
                           cmpb.reshape(bp, lp // CMP_BLOCK, 256), cx[:lp // CMP_BLOCK], small3, tq, tk)
    o_f = _fox_prompt_call(qf.reshape(bp, lp, 512), qx_f, kaug_f, dims_major(5), tq, tk)
    x2 = _outproj_call(x1, mod_p, gn, o_n.reshape(bp * lp, 512), o_f.reshape(bp * lp, 512),
                       w_merge, wa_pm, wb_pm, wo, pblk)
    y_prompt = _ffn_call(x2, mod_p, gn, *ffn_w[1], 2, pblk)
    def rows_major(t, planes, rows):
        return t.reshape(bp, planes, NSA_KV_HEADS, HEAD_DIM, rows).transpose(0, 4, 1, 2, 3)[None]

    nwin = min(WINDOW, lp)
    nsa_p = rows_major(nsa_new, 4, lp)
    fox_p = rows_major(fox_new, 2, lp)
    win_p = rows_major(win_new[:, :, lp - nwin:], 2, nwin)

    sblk = (min(bs, 64), nq)
    pp = next(p for p in (16, 8, 4, 2, 1) if n_pages % p == 0)
    x1s = _ffn_call(x_sample, mod_s, gn, *ffn_w[0], 0, sblk)
    qn_s, qf_s, nsa_s, win_s, fox_s, small_s, _ = _inproj_call(
        x1s, mod_s, gn, w1, gains, bf_pad, seg, wc, sblk, False)
    nrow = NSA_HEADS * nq
    qn_dec = _to_decode_rows(qn_s, bs, nq)
    qf_dec = _to_decode_rows(qf_s, bs, nq)
    rows = jnp.arange(nrow)
    rowc = jnp.zeros((nrow, LANES), F32).at[:, 0].set(jnp.array(SLOPES, F32)[rows // nq]).at[:, 1].set(
        (rows % nq).astype(F32))
    sm3 = small_s.reshape(bs, nq, LANES)
    gates = sm3[:, :, FOX_HEADS:FOX_HEADS + 3 * NSA_HEADS].reshape(bs, nq, 3, NSA_HEADS)
    gates = jnp.pad(gates.transpose(0, 3, 1, 2).reshape(bs, nrow, 3), ((0, 0), (0, 0), (0, LANES - 3)))
    newlf = _pad_rows(sm3[:, :, :FOX_HEADS], LANES).transpose(0, 2, 1)
    newn = _pad_rows(nsa_s.reshape(bs, nq, 512)[:, :, 256:], LANES)
    newf = _pad_rows(fox_s.reshape(bs, nq, 256), LANES)
    wnew = _pad_rows(win_s.reshape(bs, nq, 256), LANES)
    nsa_t = cache_nsa_kv[0].transpose(0, 2, 3, 4, 1)
    fox_t = cache_fox_kv[0].transpose(0, 2, 3, 4, 1)
    wct = jnp.tile(w_cmp[0].transpose(0, 2, 1), (1, 1, page // CMP_BLOCK))
    kc = pp * page
    col_blk = jnp.arange(kc) // CMP_BLOCK
    bsum = (jnp.arange(kc // CMP_BLOCK)[:, None] == col_blk[None, :]).astype(BF16)
    nblocks = past_len // CMP_BLOCK
    hot = (jnp.arange(nblocks)[None, :, None]
           == (jnp.arange(n_pages // pp)[:, None, None] * (kc // CMP_BLOCK) + col_blk[None, None, :])).astype(BF16)
    lf_pool = cache_fox_logf[0].astype(F32).transpose(0, 2, 1)
    wbuf = state_win_kv[0].reshape(bs, nbuf, 256)
    pt_flat = page_table.reshape(-1).astype(jnp.int32)
    o_c, sel, ftot = _dec_cmp_call(pt_flat, qn_dec, rowc, wct, bsum, nsa_t, lf_pool, pp, n_pages, past_len)
    on_dec, of_dec, wst = _dec_main_call(pt_flat, qn_dec, qf_dec, sel, rowc, o_c, gates, ftot, newn, newf, newlf,
                                         wbuf, wnew, hot, nsa_t, fox_t, lf_pool, pp, n_pages, past_len, nq)
    o_ns = _from_decode_rows(on_dec, bs, nq).astype(BF16)
    o_fs = _from_decode_rows(of_dec, bs, nq).astype(BF16)
    x2s = _outproj_call(x1s, mod_s, gn, o_ns, o_fs, w_merge, wa_nat, wb_nat, wo, sblk)
    y_sample = _ffn_call(x2s, mod_s, gn, *ffn_w[1], 2, sblk)
    nsa_o = nsa_s.reshape(1, bs, nq, 4, NSA_KV_HEADS, HEAD_DIM)
    fox_o = fox_s.reshape(1, bs, nq, 2, FOX_KV_HEADS, HEAD_DIM)
    win_o = wst.reshape(1, bs, nbuf, 2, NSA_KV_HEADS, HEAD_DIM)

    return (y_prompt, y_sample, nsa_p, nsa_o, fox_p, fox_o,
            logf_p[None], sm3[None, :, :, :FOX_HEADS], win_p, win_o)
```

```python
import functools

import jax
import jax.numpy as jnp
from jax import lax
from jax.experimental import pallas as pl
from jax.experimental.pallas import tpu as pltpu

HEAD_DIM = 64
NSA_HEADS = 8
NSA_KV_HEADS = 2
FOX_HEADS = 8
FOX_KV_HEADS = 2
HPG = 4
CMP_BLOCK = 64
CMP_SHIFT = 6
SLC_TOPK = 8
WINDOW = 512
N_SUB = 3
RMS_EPS = 1e-6
NEG_INF = -1e30
FORCE_BONUS = 1e4
LANES = 128
PAIR_PERM = (0, 4, 1, 5, 2, 6, 3, 7)
SLOPES = tuple(2.0 ** -(h + 1) for h in range(NSA_HEADS))
VMEM_LIMIT = 56 * 1024 * 1024

F32 = jnp.float32
BF16 = jnp.bfloat16


def _nt(a, b):
    return lax.dot_general(a, b, (((1,), (1,)), ((), ())), preferred_element_type=F32)


def _nn(a, b):
    return jnp.dot(a, b, preferred_element_type=F32)


def _split3(x):
    hi = x.astype(BF16)
    r1 = x - hi.astype(F32)
    mid = r1.astype(BF16)
    lo = (r1 - mid.astype(F32)).astype(BF16)
    return hi, mid, lo


def _params(sem):
    return pltpu.CompilerParams(dimension_semantics=sem, vmem_limit_bytes=VMEM_LIMIT)


def _const_spec(shape):
    nd = len(shape)
    return pl.BlockSpec(shape, lambda *_: (0,) * nd, pipeline_mode=pl.Buffered(1))


def _mod_kernel(c_ref, w_ref, b_ref, o_ref):
    c = c_ref[...]
    a = (c * jax.nn.sigmoid(c)).astype(BF16)
    o_ref[...] = _nn(a, w_ref[...].astype(BF16)) + b_ref[...]


def _mod_call(c_all, w_ada, b_ada):
    bc, d = c_all.shape
    n = w_ada.shape[1]
    tn = 1024 if n % 1024 == 0 else n
    return pl.pallas_call(
        _mod_kernel,
        grid=(n // tn,),
        in_specs=[pl.BlockSpec((bc, d), lambda i: (0, 0)),
                  pl.BlockSpec((d, tn), lambda i: (0, i)),
                  pl.BlockSpec((1, tn), lambda i: (0, i))],
        out_specs=pl.BlockSpec((bc, tn), lambda i: (0, i)),
        out_shape=jax.ShapeDtypeStruct((bc, n), F32),
        compiler_params=_params(("parallel",)),
        name="adaln_mod",
    )(c_all, w_ada, b_ada.reshape(1, n))


def _pre(x, mod_ref, g_ref, sub):
    ms = jnp.mean(x * x, axis=-1, keepdims=True)
    y = x * lax.rsqrt(ms + RMS_EPS) * g_ref[sub:sub + 1, :]
    return y * (1.0 + mod_ref[:, 3 * sub + 1:3 * sub + 2, :]) + mod_ref[:, 3 * sub:3 * sub + 1, :]


def _ffn_kernel(x_ref, mod_ref, g_ref, wa_ref, wb_ref, wd_ref, o_ref, act_ref, *, sub, fc):
    x = x_ref[...]
    nb, nt, d = x.shape
    h = _pre(x, mod_ref, g_ref, sub).reshape(nb * nt, d).astype(BF16)
    dff = wa_ref.shape[1]
    for c in range(dff // fc):
        a = _nn(h, wa_ref[:, c * fc:(c + 1) * fc])
        b = _nn(h, wb_ref[:, c * fc:(c + 1) * fc])
        act_ref[:, c * fc:(c + 1) * fc] = (a * jax.nn.sigmoid(a) * b).astype(BF16)
    y = _nn(act_ref[...], wd_ref[...]).reshape(nb, nt, d)
    o_ref[...] = x + 0.5 * mod_ref[:, 3 * sub + 2:3 * sub + 3, :] * y


def _ffn_call(x3, mod, g_norm, wa, wb, wd, sub, blk):
    bs, lt, d = x3.shape
    nb, nt = blk
    dff = wa.shape[1]
    fc = 256 if dff % 256 == 0 else dff
    grid = (bs // nb, lt // nt)
    return pl.pallas_call(
        functools.partial(_ffn_kernel, sub=sub, fc=fc),
        grid=grid,
        in_specs=[pl.BlockSpec((nb, nt, d), lambda i, j: (i, j, 0)),
                  pl.BlockSpec((nb, 3 * N_SUB, d), lambda i, j: (i, 0, 0)),
                  _const_spec(g_norm.shape), _const_spec(wa.shape), _const_spec(wb.shape),
                  _const_spec(wd.shape)],
        out_specs=pl.BlockSpec((nb, nt, d), lambda i, j: (i, j, 0)),
        out_shape=jax.ShapeDtypeStruct(x3.shape, F32),
        scratch_shapes=[pltpu.VMEM((nb * nt, dff), BF16)],
        compiler_params=_params(("parallel", "parallel")),
        name="ffn_sub%d" % sub,
    )(x3, mod, g_norm, wa, wb, wd)


N_NORMED = 12


def _inproj_kernel(x_ref, mod_ref, g_ref, w_ref, gain_ref, bf_ref, seg_ref, *rest, with_cmp):
    if with_cmp:
        wc_ref, qn_ref, qf_ref, nsa_ref, win_ref, fox_ref, sm_ref, kv_ref, cmp_ref = rest
    else:
        qn_ref, qf_ref, nsa_ref, win_ref, fox_ref, sm_ref, kv_ref = rest
    x = x_ref[...]
    nb, nt, d = x.shape
    h = _pre(x, mod_ref, g_ref, 1).reshape(nb * nt, d).astype(BF16)
    z = _nn(h, w_ref[...])
    seg = seg_ref[...]
    normed = []
    for c in range(N_NORMED):
        zc = z[:, c * LANES:(c + 1) * LANES]
        zz = zc * zc
        hi = zz.astype(BF16)
        lo = (zz - hi.astype(F32)).astype(BF16)
        ss = _nn(hi, seg) + _nn(lo, seg)
        normed.append(zc * lax.rsqrt(ss * (1.0 / HEAD_DIM) + RMS_EPS) * gain_ref[:, c * LANES:(c + 1) * LANES])
    raw = [z[:, (N_NORMED + c) * LANES:(N_NORMED + c + 1) * LANES] for c in range(4)]
    k_cmp, k_slc, k_win, k_fox = normed[8:12]
    v_cmp, v_slc, v_win, v_fox = raw
    for j in range(4):
        qn_ref[:, j * LANES:(j + 1) * LANES] = normed[j].astype(BF16)
        qf_ref[:, j * LANES:(j + 1) * LANES] = normed[4 + j].astype(BF16)
    def put(ref, c, t):
        if with_cmp:
            ref[c * LANES:(c + 1) * LANES, :] = t.T
        else:
            ref[:, c * LANES:(c + 1) * LANES] = t

    for c, t in enumerate((k_cmp, v_cmp, k_slc, v_slc)):
        put(nsa_ref, c, t)
    put(win_ref, 0, k_win)
    put(win_ref, 1, v_win)
    put(fox_ref, 0, k_fox)
    put(fox_ref, 1, v_fox)
    for c, t in enumerate((k_slc, v_slc, k_win, v_win, k_fox, v_fox)):
        kv_ref[:, c * LANES:(c + 1) * LANES] = t.astype(BF16)
    zs = z[:, 16 * LANES:17 * LANES]
    lane = lax.broadcasted_iota(jnp.int32, zs.shape, 1)
    zf = zs + bf_ref[...]
    logf = jnp.minimum(zf, 0.0) - jnp.log1p(jnp.exp(-jnp.abs(zf)))
    sm_ref[...] = jnp.where(lane < FOX_HEADS, logf, jax.nn.sigmoid(zs))
    if with_cmp:
        rows = nb * nt
        kvc = jnp.concatenate([k_cmp, v_cmp], axis=1).reshape(rows // CMP_BLOCK, CMP_BLOCK, 2 * LANES)
        cmp_ref[...] = jnp.sum(kvc * wc_ref[...][None], axis=1)


def _inproj_call(x3, mod, g_norm, w1, gains, bf_pad, seg, wc, blk, with_cmp):
    bs, lt, d = x3.shape
    nb, nt = blk
    rows = nb * nt
    t = bs * lt
    grid = (bs // nb, lt // nt)
    nj = lt // nt

    def tok(cols):
        return pl.BlockSpec((rows, cols), lambda i, j: (i * nj + j, 0))

    in_specs = [pl.BlockSpec((nb, nt, d), lambda i, j: (i, j, 0)),
                pl.BlockSpec((nb, 3 * N_SUB, d), lambda i, j: (i, 0, 0)),
                _const_spec(g_norm.shape), _const_spec(w1.shape), _const_spec(gains.shape),
                _const_spec(bf_pad.shape), _const_spec(seg.shape)]
    args = [x3, mod, g_norm, w1, gains, bf_pad, seg]
    out_specs = [tok(512), tok(512), tok(512), tok(256), tok(256), tok(128), tok(768)]
    out_shape = [jax.ShapeDtypeStruct((t, 512), BF16), jax.ShapeDtypeStruct((t, 512), BF16),
                 jax.ShapeDtypeStruct((t, 512), F32), jax.ShapeDtypeStruct((t, 256), F32),
                 jax.ShapeDtypeStruct((t, 256), F32), jax.ShapeDtypeStruct((t, 128), F32),
                 jax.ShapeDtypeStruct((t, 768), BF16)]
    if with_cmp:
        assert nb == 1
        for k, cols in ((2, 512), (3, 256), (4, 256)):
            out_specs[k] = pl.BlockSpec((None, cols, nt), lambda i, j: (i, 0, j))
            out_shape[k] = jax.ShapeDtypeStruct((bs, cols, lt), F32)
        in_specs.append(_const_spec(wc.shape))
        args.append(wc)
        out_specs.append(pl.BlockSpec((rows // CMP_BLOCK, 256), lambda i, j: (i * nj + j, 0)))
        out_shape.append(jax.ShapeDtypeStruct((t // CMP_BLOCK, 256), F32))
    return pl.pallas_call(
        functools.partial(_inproj_kernel, with_cmp=with_cmp),
        grid=grid, in_specs=in_specs, out_specs=out_specs, out_shape=out_shape,
        compiler_params=_params(("parallel", "parallel")),
        name="in_proj",
    )(*args)


FQ0 = 0
FK0 = 3 * FOX_HEADS


def _cumsum_kernel(tok_ref, k_ref, kaug_ref, qx_ref, carry_ref):
    @pl.when(pl.program_id(1) == 0)
    def _():
        carry_ref[...] = jnp.zeros_like(carry_ref)

    tc = tok_ref.shape[0]
    r = lax.broadcasted_iota(jnp.int32, (tc, tc), 0)
    c = lax.broadcasted_iota(jnp.int32, (tc, tc), 1)
    lower = jnp.where(r >= c, 1.0, 0.0).astype(BF16)
    cs = carry_ref[...] + sum(_nn(lower, p) for p in _split3(tok_ref[...]))
    carry_ref[...] = cs[tc - 1:tc, :]
    hrow = lax.broadcasted_iota(jnp.int32, (LANES, LANES), 0)
    col = lax.broadcasted_iota(jnp.int32, (LANES, LANES), 1)
    placed = jnp.zeros((tc, LANES), F32)
    for p, piece in enumerate(_split3(cs)):
        tgt = 3 * hrow + p
        place = jnp.where((hrow < FOX_HEADS) & ((col == FQ0 + tgt) | (col == FK0 + tgt)), 1.0, 0.0).astype(BF16)
        placed = placed + _nn(piece, place)
    lane = lax.broadcasted_iota(jnp.int32, (tc, LANES), 1)
    in_q = lane < FK0
    in_k = (lane >= FK0) & (lane < 2 * FK0)
    qx_ref[...] = jnp.where(in_q, placed, jnp.where(in_k, 1.0, 0.0)).astype(BF16)
    kaug_ref[:, :LANES] = k_ref[...]
    kaug_ref[:, LANES:] = jnp.where(in_q, 1.0, jnp.where(in_k, -placed, 0.0)).astype(BF16)


def _cumsum_call(small3, kv3, tc):
    b, l, _ = small3.shape
    return pl.pallas_call(
        _cumsum_kernel,
        grid=(b, l // tc),
        in_specs=[pl.BlockSpec((None, tc, LANES), lambda i, j: (i, j, 0)),
                  pl.BlockSpec((None, tc, LANES), lambda i, j: (i, j, 4))],
        out_specs=[pl.BlockSpec((None, tc, 2 * LANES), lambda i, j: (i, j, 0)),
                   pl.BlockSpec((None, tc, LANES), lambda i, j: (i, j, 0))],
        out_shape=[jax.ShapeDtypeStruct((b, l, 2 * LANES), BF16),
                   jax.ShapeDtypeStruct((b, l, LANES), BF16)],
        scratch_shapes=[pltpu.VMEM((1, LANES), F32)],
        compiler_params=_params(("parallel", "arbitrary")),
        name="fox_cumsum",
    )(small3, kv3)


def _flash_step(s, v, m, l, acc):
    m_new = jnp.maximum(m, jnp.max(s, axis=-1, keepdims=True))
    alpha = jnp.exp(m - m_new)
    p = jnp.exp(s - m_new)
    l = alpha * l + jnp.sum(p, axis=-1, keepdims=True)
    acc = alpha * acc + _nn(p.astype(BF16), v)
    return m_new, l, acc


def _flash_init(rows):
    return (jnp.full((rows, 1), NEG_INF, F32), jnp.zeros((rows, 1), F32), jnp.zeros((rows, LANES), F32))


def _group_masked(qp, g):
    lane = lax.broadcasted_iota(jnp.int32, qp.shape, 1)
    keep = (lane < HEAD_DIM) if g == 0 else (lane >= HEAD_DIM)
    return jnp.where(keep, qp, jnp.zeros_like(qp))


def _top_blocks(score, n_pick, axis=1):
    idx = lax.broadcasted_iota(jnp.int32, score.shape, axis).astype(F32)
    sel = jnp.zeros(score.shape, F32)
    for _ in range(n_pick):
        mx = jnp.max(score, axis=axis, keepdims=True)
        first = jnp.min(jnp.where(score == mx, idx, 1e9), axis=axis, keepdims=True)
        pick = idx == first
        sel = jnp.where(pick, 1.0, sel)
        score = jnp.where(pick, -jnp.inf, score)
    return sel


XL_ONE_A = CMP_BLOCK
XL_ONE_B = CMP_BLOCK + 1
XL_HI = CMP_BLOCK + 2
XL_LO = CMP_BLOCK + 3
MASK_BIG = 1e30
N_FREE = SLC_TOPK - 3


VT_ROWS = HEAD_DIM + 16


def _col_max(s):
    k = s.shape[0]
    if k % 64 == 0:
        s = jnp.max(s.reshape(8, k // 8, s.shape[1]), axis=0)
    return jnp.max(s, axis=0, keepdims=True)


def _heads_scores(k, qa_ref, ok):
    out = []
    for h in range(qa_ref.shape[0]):
        s = _nt(k, qa_ref[h])
        out.append(s if ok is None else jnp.where(ok, s, NEG_INF))
    return out


def _heads_stage(scores, s_ref):
    for h, s in enumerate(scores):
        s_ref[h] = s


def _heads_consume(vt, s_ref, m_ref, acc_ref):
    nh = s_ref.shape[0]
    probs, alphas = [], []
    for h in range(nh):
        s = s_ref[h]
        m_old = m_ref[h:h + 1, :]
        m_new = jnp.maximum(m_old, _col_max(s))
        m_ref[h:h + 1, :] = m_new
        alphas.append(jnp.exp(m_old - m_new))
        probs.append(jnp.exp(s - m_new).astype(BF16))
    for h in range(nh):
        acc_ref[h] = alphas[h] * acc_ref[h] + _nn(vt[h // HPG], probs[h])


def _heads_run(k_of, vt_ref, qa_ref, s_ref, m_ref, acc_ref, lo, qi, mask_of, mask_all, need_ref=None):
    m_ref[...] = jnp.full(m_ref.shape, NEG_INF, F32)
    acc_ref[...] = jnp.zeros_like(acc_ref)
    _heads_stage(_heads_scores(k_of(qi), qa_ref, mask_of(qi)), s_ref)

    def step(kj, staged):
        new = _heads_scores(k_of(kj), qa_ref, mask_of(kj) if mask_all else None)
        _heads_consume(vt_ref[staged], s_ref, m_ref, acc_ref)
        _heads_stage(new, s_ref)
        return kj

    def body(kj, staged):
        if need_ref is None:
            return step(kj, staged)
        wanted = lax.shift_right_logical(need_ref[0], kj) & 1
        return lax.cond(wanted > 0, lambda st: step(kj, st), lambda st: st, staged)

    staged = lax.fori_loop(lo, qi, body, qi)
    _heads_consume(vt_ref[staged], s_ref, m_ref, acc_ref)


def _attn_scratch(tq, tk):
    return [pltpu.VMEM((NSA_HEADS, tq, 2 * LANES), BF16), pltpu.VMEM((NSA_HEADS, tq, 2 * LANES), BF16),
            pltpu.VMEM((NSA_HEADS, tq), F32), pltpu.VMEM((NSA_HEADS, tk, tq), F32),
            pltpu.VMEM((NSA_HEADS, VT_ROWS, tq), F32), pltpu.VMEM((NSA_HEADS, HEAD_DIM, tq), F32)]


def _finish_t(acc):
    return acc[:HEAD_DIM, :] / acc[HEAD_DIM:HEAD_DIM + 1, :]


def _nsa_prompt_kernel(q_ref, ks_ref, kw_ref, kx_ref, vst_ref, vwt_ref, cmp_ref, cx_ref, sm_ref, o_ref,
                       qas_ref, qaw_ref, m_ref, s_ref, acc_ref, out_ref, need_ref, *, tq, tk, nb):
    qi = pl.program_id(1)
    lane = lax.broadcasted_iota(jnp.int32, (tq, LANES), 1)
    t2 = qi * tq + lax.broadcasted_iota(jnp.int32, (tq, LANES), 0)
    t_hi = lax.shift_right_logical(t2, CMP_SHIFT).astype(F32)
    t_lo = (t2 & (CMP_BLOCK - 1)).astype(F32)
    alibi_q = jnp.where(lane == XL_ONE_A, -CMP_BLOCK * t_hi,
                        jnp.where(lane == XL_ONE_B, -t_lo,
                                  jnp.where(lane == XL_HI, float(CMP_BLOCK), jnp.where(lane == XL_LO, 1.0, 0.0))))
    krow = lax.broadcasted_iota(jnp.int32, (tk, tq), 0)
    qcol = lax.broadcasted_iota(jnp.int32, (tk, tq), 1)

    cmpv = cmp_ref[...]
    ck_aug = jnp.concatenate([cmpv[:, :LANES].astype(BF16), cx_ref[...]], axis=1)
    cvt = jnp.concatenate([cmpv[:, LANES:], jnp.zeros((LANES - nb, LANES), F32)],
                          axis=0).T[:, :nb].astype(BF16)
    brow = lax.broadcasted_iota(jnp.int32, (nb, tq), 0)
    tcol = qi * tq + lax.broadcasted_iota(jnp.int32, (nb, tq), 1)
    valid_c = (brow + 1) * CMP_BLOCK - 1 <= tcol
    cur = lax.shift_right_logical(tcol, CMP_SHIFT)
    forced = (brow == 0) | (brow == cur) | (brow == cur - 1)
    free = (brow * CMP_BLOCK <= tcol) & jnp.logical_not(forced)
    smt = sm_ref[...].T

    qx = [SLOPES[h] * alibi_q for h in range(NSA_HEADS)]
    for g in range(NSA_KV_HEADS):
        for j in range(HPG):
            h = g * HPG + j
            qaw_ref[h, :, :LANES] = _group_masked(q_ref[:, j * LANES:(j + 1) * LANES], g)
            qaw_ref[h, :, LANES:] = qx[h].astype(BF16)
    s_c = [jnp.where(valid_c, _nt(ck_aug, qaw_ref[h]), NEG_INF) for h in range(NSA_HEADS)]
    p_c = []
    for s in s_c:
        e = jnp.exp(s - jnp.max(s, axis=0, keepdims=True))
        p_c.append(jnp.where(valid_c, e / jnp.sum(e, axis=0, keepdims=True), 0.0))
    for h in range(NSA_HEADS):
        g = h // HPG
        out_ref[h] = smt[FOX_HEADS + h:FOX_HEADS + h + 1, :] * _nn(cvt[g * HEAD_DIM:(g + 1) * HEAD_DIM, :],
                                                                 p_c[h].astype(BF16))
    imp = jnp.stack([sum(p_c[g * HPG:(g + 1) * HPG]) for g in range(NSA_KV_HEADS)])
    score = jnp.where(free[None], imp, jnp.where(forced[None], 2 * NEG_INF, NEG_INF))
    sel_all = jnp.where(forced[None], 1.0, _top_blocks(score, N_FREE, axis=1))
    picked = jnp.zeros((8, LANES), F32)
    for g in range(NSA_KV_HEADS):
        sel = jnp.concatenate([sel_all[g], jnp.zeros((LANES - nb, tq), F32)], axis=0).T
        sel_q = jnp.where(lane < CMP_BLOCK, (sel - 1.0) * MASK_BIG, 0.0)
        picked = jnp.maximum(picked, jnp.max(sel, axis=0, keepdims=True))
        for j in range(HPG):
            h = g * HPG + j
            qas_ref[h, :, :LANES] = qaw_ref[h, :, :LANES]
            qas_ref[h, :, LANES:] = (qx[h] + sel_q).astype(BF16)

    assert tq == tk, "the pipelined key loop walks one key tile per query tile"
    bpt = tk // CMP_BLOCK
    assert bpt & (bpt - 1) == 0 and nb // bpt <= 24, "tile bits must be exact in f32"
    sh = 1
    while sh < bpt:
        picked = jnp.maximum(picked, pltpu.roll(picked, sh, axis=1))
        sh *= 2
    blane = lax.broadcasted_iota(jnp.int32, (8, LANES), 1)
    weight = jnp.where(((blane & (bpt - 1)) == bpt - 1) & (blane < nb),
                       lax.shift_left(jnp.ones_like(blane), lax.shift_right_logical(blane, bpt.bit_length() - 1)),
                       0).astype(F32)
    bits = jnp.sum(picked[0:1, :] * weight[0:1, :], axis=1, keepdims=True)
    need_ref[0] = bits[0, 0].astype(jnp.int32)

    def run_branch(qa_ref, k_ref, vt_ref, lo, window, gate_row, need=None):
        def k_of(kj):
            off = pl.multiple_of(kj * tk, tk)
            return jnp.concatenate([k_ref[pl.ds(off, tk), :], kx_ref[pl.ds(off, tk), :]], axis=1)

        def mask_of(kj):
            dist = (qi - kj) * tq + qcol - krow
            return (dist >= 0) & (dist < WINDOW) if window else dist >= 0

        _heads_run(k_of, vt_ref, qa_ref, s_ref, m_ref, acc_ref, lo, qi, mask_of, window, need)
        for h in range(NSA_HEADS):
            r = gate_row + h
            out_ref[h] = out_ref[h] + smt[r:r + 1, :] * _finish_t(acc_ref[h])

    run_branch(qas_ref, ks_ref, vst_ref, 0, False, FOX_HEADS + NSA_HEADS, need_ref)
    run_branch(qaw_ref, kw_ref, vwt_ref, jnp.maximum(qi - (WINDOW + tk - 1) // tk, 0), True,
               FOX_HEADS + 2 * NSA_HEADS)
    for j in range(HPG):
        pair_t = jnp.concatenate([out_ref[j], out_ref[HPG + j]], axis=0)
        o_ref[:, j * LANES:(j + 1) * LANES] = pair_t.T.astype(BF16)


def _nsa_prompt_call(qn3, kv3, kx, vst, vwt, cmp3, cx, small3, tq, tk):
    b, l, _ = qn3.shape
    nb = cmp3.shape[1]
    assert nb <= CMP_BLOCK, "one-hot block lanes cover at most CMP_BLOCK blocks"

    def kspec(c):
        return pl.BlockSpec((None, l, LANES), lambda i, j: (i, 0, c))

    vspec = pl.BlockSpec((None, l // tk, NSA_KV_HEADS, VT_ROWS, tk), lambda i, j: (i, 0, 0, 0, 0))
    return pl.pallas_call(
        functools.partial(_nsa_prompt_kernel, tq=tq, tk=tk, nb=nb),
        grid=(b, l // tq),
        in_specs=[pl.BlockSpec((None, tq, 512), lambda i, j: (i, j, 0)),
                  kspec(0), kspec(2), pl.BlockSpec((l, LANES), lambda i, j: (0, 0)), vspec, vspec,
                  pl.BlockSpec((None, nb, 256), lambda i, j: (i, 0, 0)),
                  pl.BlockSpec((nb, LANES), lambda i, j: (0, 0)),
                  pl.BlockSpec((None, tq, LANES), lambda i, j: (i, j, 0))],
        out_specs=pl.BlockSpec((None, tq, 512), lambda i, j: (i, j, 0)),
        out_shape=jax.ShapeDtypeStruct((b, l, 512), BF16),
        scratch_shapes=_attn_scratch(tq, tk) + [pltpu.SMEM((1,), jnp.int32)],
        compiler_params=_params(("parallel", "arbitrary")),
        name="nsa_prompt",
    )(qn3, kv3, kv3, kx, vst, vwt, cmp3, cx, small3)


def _fox_prompt_kernel(q_ref, qx_ref, k_ref, vt_ref, o_ref, qa_ref, m_ref, s_ref, acc_ref, *, tq, tk):
    assert tq == tk, "the pipelined key loop walks one key tile per query tile"
    qi = pl.program_id(1)
    lane = lax.broadcasted_iota(jnp.int32, (tq, LANES), 1)
    krow = lax.broadcasted_iota(jnp.int32, (tk, tq), 0)
    qcol = lax.broadcasted_iota(jnp.int32, (tk, tq), 1)
    qx = qx_ref[...]
    for g in range(FOX_KV_HEADS):
        for j in range(HPG):
            h = g * HPG + j
            own = ((lane >= FQ0 + 3 * h) & (lane < FQ0 + 3 * h + 3)) | ((lane >= FK0 + 3 * h) & (lane < FK0 + 3 * h + 3))
            qa_ref[h, :, :LANES] = _group_masked(q_ref[:, j * LANES:(j + 1) * LANES], g)
            qa_ref[h, :, LANES:] = jnp.where(own, qx, jnp.zeros_like(qx))
    _heads_run(lambda kj: k_ref[pl.ds(pl.multiple_of(kj * tk, tk), tk), :], vt_ref, qa_ref, s_ref, m_ref, acc_ref,
               0, qi, lambda kj: (qi - kj) * tq + qcol - krow >= 0, False)
    for j in range(HPG):
        pair_t = jnp.concatenate([_finish_t(acc_ref[j]), _finish_t(acc_ref[HPG + j])], axis=0)
        o_ref[:, j * LANES:(j + 1) * LANES] = pair_t.T.astype(BF16)


def _fox_prompt_call(qf3, qx3, kaug3, vft, tq, tk):
    b, l, _ = qf3.shape
    return pl.pallas_call(
        functools.partial(_fox_prompt_kernel, tq=tq, tk=tk),
        grid=(b, l // tq),
        in_specs=[pl.BlockSpec((None, tq, 512), lambda i, j: (i, j, 0)),
                  pl.BlockSpec((None, tq, LANES), lambda i, j: (i, j, 0)),
                  pl.BlockSpec((None, l, 2 * LANES), lambda i, j: (i, 0, 0)),
                  pl.BlockSpec((None, l // tk, FOX_KV_HEADS, VT_ROWS, tk), lambda i, j: (i, 0, 0, 0, 0))],
        out_specs=pl.BlockSpec((None, tq, 512), lambda i, j: (i, j, 0)),
        out_shape=jax.ShapeDtypeStruct((b, l, 512), BF16),
        scratch_shapes=_attn_scratch(tq, tk)[1:5],
        compiler_params=_params(("parallel", "arbitrary")),
        name="fox_prompt",
    )(qf3, qx3, kaug3, vft)


def _lane_cumsum(x):
    n = x.shape[1]
    r = lax.broadcasted_iota(jnp.int32, (n, n), 0)
    c = lax.broadcasted_iota(jnp.int32, (n, n), 1)
    upper = jnp.where(r <= c, 1.0, 0.0).astype(BF16)
    return sum(_nn(p, upper) for p in _split3(x))


def _dec_cmp_kernel(pt_ref, q_ref, rowc_ref, wct_ref, bsum_ref, *rest, pp, nb, past_len):
    del pt_ref
    pages = rest[:pp]
    lfs = rest[pp:2 * pp]
    oc_ref, sel_ref, ftot_ref, cmp_scr, ftot_scr = rest[2 * pp:]
    c = pl.program_id(1)

    @pl.when(c == 0)
    def _():
        ftot_scr[...] = jnp.zeros_like(ftot_scr)

    nblk = bsum_ref.shape[0]
    bsum = bsum_ref[...]
    for plane in range(2):
        w = wct_ref[plane]
        for g in range(NSA_KV_HEADS):
            prod = jnp.concatenate([(p[plane, g] * w).astype(BF16) for p in pages], axis=1)
            blk = _nt(bsum, prod)
            cmp_scr[2 * plane + g, pl.ds(pl.multiple_of(c * nblk, nblk), nblk), :] = blk
    tot = lfs[0][...]
    for p in lfs[1:]:
        tot = tot + p[...]
    ftot_scr[...] += jnp.broadcast_to(jnp.sum(tot, axis=-1, keepdims=True), ftot_scr.shape)

    @pl.when(c == pl.num_programs(1) - 1)
    def _():
        ftot_ref[...] = ftot_scr[...]
        ckb = jnp.concatenate([cmp_scr[0], cmp_scr[1]], axis=1).astype(BF16)
        cvb = jnp.concatenate([cmp_scr[2], cmp_scr[3]], axis=1).astype(BF16)
        nrow = q_ref.shape[0]
        rowc = rowc_ref[...]
        slope = rowc[:, 0:1]
        t = past_len + rowc[:, 1:2]
        nidx = lax.broadcasted_iota(jnp.int32, (nrow, nb), 1)
        cmid = nidx.astype(F32) * CMP_BLOCK + 0.5 * (CMP_BLOCK - 1)
        valid = cmid + 0.5 * (CMP_BLOCK - 1) <= t
        s = jnp.where(valid, _nt(q_ref[...], ckb) - slope * (t - cmid), NEG_INF)
        e = jnp.exp(s - jnp.max(s, axis=-1, keepdims=True))
        p = jnp.where(valid, e / jnp.sum(e, axis=-1, keepdims=True), 0.0)
        oc_ref[...] = _nn(p.astype(BF16), cvb)
        nq = nrow // NSA_HEADS
        n8 = lax.broadcasted_iota(jnp.int32, (nq, nb), 1)
        forced = (n8 == 0) | (n8 == nb - 1)
        sel_rows = []
        for g in range(NSA_KV_HEADS):
            imp = jnp.zeros((nq, nb), F32)
            for j in range(HPG):
                r0 = (g * HPG + j) * nq
                imp = imp + p[r0:r0 + nq, :]
            sel = _top_blocks(imp + jnp.where(forced, FORCE_BONUS, 0.0), SLC_TOPK - 1)
            sel_rows += [sel] * HPG
        sel_ref[...] = ((jnp.concatenate(sel_rows, axis=0) - 1.0) * MASK_BIG).astype(BF16)


def _page_specs(pp, n_pages, block, plane_pair):
    tail = (0,) * (len(block) - 2)

    def one(i):
        return pl.BlockSpec(block, lambda b, c, pt: (pt[b * n_pages + c * pp + i], plane_pair) + tail)
    return [one(i) for i in range(pp)]


def _dec_cmp_call(pt_flat, q_dec, rowc, wct, bsum, nsa_t, lf_pool, pp, n_pages, past_len):
    b, nrow, _ = q_dec.shape
    nb = past_len // CMP_BLOCK
    page = nsa_t.shape[-1]
    nc = n_pages // pp
    assert nb <= LANES, "block mask lanes cover at most 128 cached blocks"
    per = lambda shape: pl.BlockSpec((None,) + shape, lambda i, c, pt: (i, 0, 0))
    cst = lambda shape: pl.BlockSpec(shape, lambda i, c, pt: (0,) * len(shape))
    grid_spec = pltpu.PrefetchScalarGridSpec(
        num_scalar_prefetch=1,
        grid=(b, nc),
        in_specs=[per((nrow, LANES)), cst(rowc.shape), cst(wct.shape), cst(bsum.shape)]
        + _page_specs(pp, n_pages, (None, 2, NSA_KV_HEADS, HEAD_DIM, page), 0)
        + _page_specs(pp, n_pages, (None, FOX_HEADS, page), 0),
        out_specs=[per((nrow, LANES)), per((nrow, nb)), per((FOX_HEADS, LANES))],
        scratch_shapes=[pltpu.VMEM((4, nb, HEAD_DIM), F32), pltpu.VMEM((FOX_HEADS, LANES), F32)],
    )
    return pl.pallas_call(
        functools.partial(_dec_cmp_kernel, pp=pp, nb=nb, past_len=past_len),
        grid_spec=grid_spec,
        out_shape=[jax.ShapeDtypeStruct((b, nrow, LANES), F32),
                   jax.ShapeDtypeStruct((b, nrow, nb), BF16),
                   jax.ShapeDtypeStruct((b, FOX_HEADS, LANES), F32)],
        compiler_params=_params(("parallel", "arbitrary")),
        name="decode_cmp",
    )(pt_flat, q_dec, rowc, wct, bsum, *([nsa_t] * pp), *([lf_pool] * pp))


def _rep_rows(x, n):
    return jnp.concatenate([jnp.broadcast_to(x[h:h + 1, :], (n, x.shape[1])) for h in range(x.shape[0])], axis=0)


def _scr_flash_t(scores, vts, stats):
    staged = []
    for s, (m_ref, l_ref, _) in zip(scores, stats):
        m_old = m_ref[:, 0:1]
        m = jnp.maximum(m_old, jnp.max(s, axis=-1, keepdims=True))
        alpha = jnp.exp(m_old - m)
        p = jnp.exp(s - m)
        l = alpha * l_ref[:, 0:1] + jnp.sum(p, axis=-1, keepdims=True)
        m_ref[...] = jnp.broadcast_to(m, m_ref.shape)
        l_ref[...] = jnp.broadcast_to(l, l_ref.shape)
        staged.append((alpha, p.astype(BF16)))
    for (alpha, p), vt, (_, _, acc_ref) in zip(staged, vts, stats):
        acc_ref[...] = alpha * acc_ref[...] + _nt(p, vt)


def _pages_t(pages, plane):
    return jnp.concatenate(
        [jnp.concatenate([p[plane, g] for g in range(NSA_KV_HEADS)], axis=0) for p in pages], axis=1).astype(BF16)


def _dec_main_kernel(pt_ref, qn_ref, qf_ref, sel_ref, rowc_ref, oc_ref, gate_ref, ftot_ref,
                     newn_ref, newf_ref, newlf_ref, wbuf_ref, wnew_ref, hot_ref, *rest, pp, nb, past_len, nq):
    del pt_ref
    slc = rest[:pp]
    fox = rest[pp:2 * pp]
    lfs = rest[2 * pp:3 * pp]
    (on_ref, of_ref, wst_ref, ms_ref, ls_ref, as_ref, mf_ref, lf_ref, af_ref, fcar_ref, ftc_ref,
     qna_ref) = rest[3 * pp:]
    c = pl.program_id(1)
    nrow = qn_ref.shape[0]
    page = lfs[0].shape[1]
    kc = pp * page
    rowc = rowc_ref[...]
    slope = rowc[:, 0:1]
    t = past_len + rowc[:, 1:2]
    qn = qn_ref[...]
    qf = qf_ref[...]

    @pl.when(c == 0)
    def _():
        for r in (ms_ref, mf_ref):
            r[...] = jnp.full(r.shape, NEG_INF, F32)
        for r in (ls_ref, as_ref, lf_ref, af_ref, fcar_ref):
            r[...] = jnp.zeros_like(r)
        f_new = ftot_ref[...] + _lane_cumsum(newlf_ref[...])
        rep = _rep_rows(f_new, nq)
        lane = lax.broadcasted_iota(jnp.int32, rep.shape, 1)
        tok = lax.broadcasted_iota(jnp.int32, rep.shape, 0) & (nq - 1)
        ftc_ref[...] = jnp.broadcast_to(
            jnp.sum(jnp.where(lane == tok, rep, 0.0), axis=-1, keepdims=True), ftc_ref.shape)
        qna_ref[:, :LANES] = qn
        qna_ref[:, LANES:] = sel_ref[...]

    ft = ftc_ref[:, 0:1]

    kts = jnp.concatenate([_pages_t(slc, 0), hot_ref[...]], axis=0)
    pos = (c * kc + lax.broadcasted_iota(jnp.int32, (1, kc), 1)).astype(F32)
    s_slc = _nn(qna_ref[...], kts) - slope * (t - pos)

    s_fox = _nn(qf, _pages_t(fox, 0))
    cs = _lane_cumsum(jnp.concatenate([p[...] for p in lfs], axis=0))
    car = fcar_ref[:, 0:1]
    fs_parts = []
    for i in range(pp):
        cs_i = cs[i * FOX_HEADS:(i + 1) * FOX_HEADS, :]
        fs_parts.append(_rep_rows(car + cs_i, nq))
        car = car + cs_i[:, page - 1:page]
    fcar_ref[...] = jnp.broadcast_to(car, fcar_ref.shape)
    fs = jnp.concatenate(fs_parts, axis=1)
    _scr_flash_t([s_slc, s_fox + (ft - fs)], [_pages_t(slc, 1), _pages_t(fox, 1)],
                 [(ms_ref, ls_ref, as_ref), (mf_ref, lf_ref, af_ref)])

    @pl.when(c == pl.num_programs(1) - 1)
    def _():
        npad = newn_ref.shape[0]
        lane_i = lax.broadcasted_iota(jnp.int32, (1, npad), 1)
        posn = (past_len + lane_i).astype(F32)
        okn = (lane_i < nq) & (posn <= t)
        newn = newn_ref[...]
        sn = jnp.where(okn, _nt(qn, newn[:, :LANES].astype(BF16)) - slope * (t - posn), NEG_INF)
        m, l, acc = _flash_step(sn, newn[:, LANES:].astype(BF16), ms_ref[:, 0:1], ls_ref[:, 0:1], as_ref[...])
        o_s = acc / l
        newf = newf_ref[...]
        fsn = _rep_rows(fcar_ref[:, 0:1] + _lane_cumsum(newlf_ref[...]), nq)
        sf = jnp.where(okn, _nt(qf, newf[:, :LANES].astype(BF16)) + (ft - fsn), NEG_INF)
        m, l, acc = _flash_step(sf, newf[:, LANES:].astype(BF16), mf_ref[:, 0:1], lf_ref[:, 0:1], af_ref[...])
        of_ref[...] = acc / l
        wbuf = wbuf_ref[...]
        wnew = wnew_ref[...]
        nbuf = wbuf.shape[0]
        kw = jnp.concatenate([wbuf[:, :LANES], wnew[:, :LANES]], axis=0).astype(BF16)
        vw = jnp.concatenate([wbuf[:, LANES:], wnew[:, LANES:]], axis=0).astype(BF16)
        wl = lax.broadcasted_iota(jnp.int32, (1, nbuf + npad), 1)
        posw = jnp.where(wl < nbuf, past_len - nbuf + wl, past_len + wl - nbuf).astype(F32)
        dw = t - posw
        okw = (dw >= 0) & (dw < WINDOW) & ((wl < nbuf) | (wl < nbuf + nq))
        sw = jnp.where(okw, _nt(qn, kw) - slope * dw, NEG_INF)
        ew = jnp.exp(sw - jnp.max(sw, axis=-1, keepdims=True))
        o_w = _nn(ew.astype(BF16), vw) / jnp.sum(ew, axis=-1, keepdims=True)
        gate = gate_ref[...]
        on_ref[...] = gate[:, 0:1] * oc_ref[...] + gate[:, 1:2] * o_s + gate[:, 2:3] * o_w
        wst_ref[0:nbuf - nq, :] = wbuf[nq:, :]
        wst_ref[nbuf - nq:, :] = wnew[0:nq, :]


def _dec_main_call(pt_flat, qn_dec, qf_dec, sel, rowc, o_c, gates, ftot, newn, newf, newlf, wbuf, wnew, hot,
                   nsa_t, fox_t, lf_pool, pp, n_pages, past_len, nq):
    b, nrow, _ = qn_dec.shape
    nb = past_len // CMP_BLOCK
    page = nsa_t.shape[-1]
    nc = n_pages // pp
    nbuf = wbuf.shape[1]
    assert nb % 16 == 0 and nb <= LANES, "query-side block mask lanes"
    per = lambda shape: pl.BlockSpec((None,) + shape, lambda i, c, pt: (i, 0, 0))
    cst = lambda shape: pl.BlockSpec(shape, lambda i, c, pt: (0, 0))
    rl = (nrow, LANES)
    kv_block = (None, 2, NSA_KV_HEADS, HEAD_DIM, page)
    grid_spec = pltpu.PrefetchScalarGridSpec(
        num_scalar_prefetch=1,
        grid=(b, nc),
        in_specs=[per(rl), per(rl), per((nrow, nb)), cst(rowc.shape), per(rl), per(rl), per((FOX_HEADS, LANES)),
                  per(newn.shape[1:]), per(newf.shape[1:]), per(newlf.shape[1:]),
                  per((nbuf, 2 * LANES)), per(wnew.shape[1:]),
                  pl.BlockSpec((None,) + hot.shape[1:], lambda i, c, pt: (c, 0, 0))]
        + _page_specs(pp, n_pages, kv_block, 1)
        + _page_specs(pp, n_pages, kv_block, 0)
        + _page_specs(pp, n_pages, (None, FOX_HEADS, page), 0),
        out_specs=[per(rl), per(rl), per((nbuf, 2 * LANES))],
        scratch_shapes=[pltpu.VMEM(rl, F32)] * 6 + [pltpu.VMEM((FOX_HEADS, LANES), F32), pltpu.VMEM(rl, F32),
                                                    pltpu.VMEM((nrow, LANES + nb), BF16)],
    )
    return pl.pallas_call(
        functools.partial(_dec_main_kernel, pp=pp, nb=nb, past_len=past_len, nq=nq),
        grid_spec=grid_spec,
        out_shape=[jax.ShapeDtypeStruct((b,) + rl, F32), jax.ShapeDtypeStruct((b,) + rl, F32),
                   jax.ShapeDtypeStruct((b, nbuf, 2 * LANES), F32)],
        compiler_params=_params(("parallel", "arbitrary")),
        name="decode_main",
    )(pt_flat, qn_dec, qf_dec, sel, rowc, o_c, gates, ftot, newn, newf, newlf, wbuf, wnew, hot,
      *([nsa_t] * pp), *([fox_t] * pp), *([lf_pool] * pp))


def _outproj_kernel(x_ref, mod_ref, g_ref, on_ref, of_ref, wm_ref, wa_ref, wb_ref, wo_ref, o_ref):
    x = x_ref[...]
    nb, nt, d = x.shape
    h = _pre(x, mod_ref, g_ref, 1).reshape(nb * nt, d).astype(BF16)
    gm = jax.nn.sigmoid(_nn(h, wm_ref[...]))
    y = gm[:, :d] * _nn(on_ref[...], wa_ref[...]) + gm[:, d:] * _nn(of_ref[...], wb_ref[...])
    mix = _nn(y.astype(BF16), wo_ref[...]).reshape(nb, nt, d)
    o_ref[...] = x + mod_ref[:, 5:6, :] * mix


def _outproj_call(x3, mod, g_norm, o_n, o_f, wm, wa, wb, wo, blk):
    bs, lt, d = x3.shape
    nb, nt = blk
    rows = nb * nt
    nj = lt // nt
    tok = pl.BlockSpec((rows, o_n.shape[1]), lambda i, j: (i * nj + j, 0))
    return pl.pallas_call(
        _outproj_kernel,
        grid=(bs // nb, nj),
        in_specs=[pl.BlockSpec((nb, nt, d), lambda i, j: (i, j, 0)),
                  pl.BlockSpec((nb, 3 * N_SUB, d), lambda i, j: (i, 0, 0)),
                  _const_spec(g_norm.shape), tok, tok,
                  _const_spec(wm.shape), _const_spec(wa.shape), _const_spec(wb.shape), _const_spec(wo.shape)],
        out_specs=pl.BlockSpec((nb, nt, d), lambda i, j: (i, j, 0)),
        out_shape=jax.ShapeDtypeStruct(x3.shape, F32),
        compiler_params=_params(("parallel", "parallel")),
        name="out_proj",
    )(x3, mod, g_norm, o_n, o_f, wm, wa, wb, wo)


def _pair_major_cols(w):
    d = w.shape[0]
    return w.reshape(d, NSA_HEADS, HEAD_DIM)[:, jnp.array(PAIR_PERM), :].reshape(d, NSA_HEADS * HEAD_DIM)


def _pair_major_rows(w):
    d = w.shape[1]
    return w.reshape(NSA_HEADS, HEAD_DIM, d)[jnp.array(PAIR_PERM)].reshape(NSA_HEADS * HEAD_DIM, d)


def _prep_weights(w_in, b_fgt, g_q_nsa, g_k_nsa, g_q_fox, g_k_fox, w_cmp):
    d = w_in.shape[0]
    hq = NSA_HEADS * HEAD_DIM
    o = 0
    w_qn = w_in[:, o:o + hq]; o += hq
    w_kvn = w_in[:, o:o + 3 * 2 * LANES]; o += 3 * 2 * LANES
    w_gt = w_in[:, o:o + 3 * NSA_HEADS]; o += 3 * NSA_HEADS
    w_qf = w_in[:, o:o + hq]; o += hq
    w_kvf = w_in[:, o:o + 2 * LANES]; o += 2 * LANES
    w_ff = w_in[:, o:o + FOX_HEADS]; o += FOX_HEADS
    w_merge = w_in[:, o:]
    k_n = [w_kvn[:, br * 256:br * 256 + LANES] for br in range(3)]
    v_n = [w_kvn[:, br * 256 + LANES:(br + 1) * 256] for br in range(3)]
    small = jnp.concatenate([w_ff, w_gt, jnp.zeros((d, LANES - FOX_HEADS - 3 * NSA_HEADS), w_in.dtype)], axis=1)
    w1 = jnp.concatenate([_pair_major_cols(w_qn), _pair_major_cols(w_qf), k_n[0], k_n[1], k_n[2], w_kvf[:, :LANES],
                          v_n[0], v_n[1], v_n[2], w_kvf[:, LANES:], small], axis=1).astype(BF16)
    scale = HEAD_DIM ** -0.5
    gains = jnp.concatenate([jnp.tile(g_q_nsa * scale, NSA_HEADS), jnp.tile(g_q_fox * scale, FOX_HEADS),
                             jnp.tile(g_k_nsa[0], 2), jnp.tile(g_k_nsa[1], 2), jnp.tile(g_k_nsa[2], 2),
                             jnp.tile(g_k_fox, 2)]).reshape(1, N_NORMED * LANES)
    bf_pad = jnp.concatenate([b_fgt, jnp.zeros((LANES - FOX_HEADS,), F32)]).reshape(1, LANES)
    lane = jnp.arange(LANES)
    seg = (lane[:, None] // HEAD_DIM == lane[None, :] // HEAD_DIM).astype(BF16)
    wc = jnp.concatenate([w_cmp[0], w_cmp[0], w_cmp[1], w_cmp[1]], axis=1)
    return w1, gains, bf_pad, seg, wc, w_merge.astype(BF16)


def _position_lanes(l):
    pos = jnp.arange(l)
    lane = jnp.arange(LANES)[None, :]
    blk = (pos // CMP_BLOCK)[:, None]
    kx = jnp.where(lane < CMP_BLOCK, (lane == blk).astype(F32),
                   jnp.where((lane == XL_ONE_A) | (lane == XL_ONE_B), 1.0,
                             jnp.where(lane == XL_HI, blk.astype(F32),
                                       jnp.where(lane == XL_LO, (pos % CMP_BLOCK)[:, None].astype(F32), 0.0))))
    n = jnp.arange(LANES)[:, None]
    cx = jnp.where((lane == XL_ONE_A) | (lane == XL_ONE_B), 1.0,
                   jnp.where(lane == XL_HI, n.astype(F32), jnp.where(lane == XL_LO, 0.5 * (CMP_BLOCK - 1), 0.0)))
    return kx.astype(BF16), cx.astype(BF16)


def _to_decode_rows(q, b, nq):
    q5 = q.reshape(b, nq, HPG, 2, HEAD_DIM)
    eye = jnp.eye(2, dtype=q.dtype)
    return jnp.einsum('btjgd,gk->bgjtkd', q5, eye).reshape(b, NSA_HEADS * nq, LANES)


def _from_decode_rows(o, b, nq):
    o6 = o.reshape(b, 2, HPG, nq, 2, HEAD_DIM)
    own = jnp.stack([o6[:, 0, :, :, 0], o6[:, 1, :, :, 1]], axis=1)
    return own.transpose(0, 3, 1, 2, 4).reshape(b * nq, NSA_HEADS * HEAD_DIM)


def _pad_rows(x, n):
    return jnp.pad(x, ((0, 0), (0, n - x.shape[1]), (0, 0)))


def kernel(x_prompt, x_sample, cache_nsa_kv, cache_fox_kv, cache_fox_logf, state_win_kv, page_table,
           c_prompt, c_sample, w_ada, b_ada, g_norm, w_ffn_up, w_ffn_down, w_in, b_fgt,
           g_q_nsa, g_k_nsa, g_q_fox, g_k_fox, w_cmp, w_br_nsa, w_br_fox, w_out):
    assert w_ada.shape[0] == 1, "single-layer trunk"
    bp, lp, d = x_prompt.shape
    bs, nq, _ = x_sample.shape
    n_pool, page = cache_nsa_kv.shape[1:3]
    n_pages = page_table.shape[1]
    past_len = n_pages * page
    nbuf = state_win_kv.shape[2]
    dff = w_ffn_down.shape[2]
    assert nq % 8 == 0 and nq & (nq - 1) == 0 and nq <= CMP_BLOCK and past_len // CMP_BLOCK >= SLC_TOPK
    assert nbuf == WINDOW and lp % 512 == 0

    mod = _mod_call(jnp.concatenate([c_prompt, c_sample], axis=0), w_ada[0], b_ada[0]).reshape(bp + bs, 3 * N_SUB, d)
    mod_p, mod_s = mod[:bp], mod[bp:]
    gn = g_norm[0]
    up = w_ffn_up[0].astype(BF16)
    ffn_w = [(up[i, :, :dff], up[i, :, dff:], w_ffn_down[0, i].astype(BF16)) for i in range(2)]
    w1, gains, bf_pad, seg, wc, w_merge = _prep_weights(
        w_in[0], b_fgt[0], g_q_nsa[0], g_k_nsa[0], g_q_fox[0], g_k_fox[0], w_cmp[0])
    wo = w_out[0].astype(BF16)
    wa_nat, wb_nat = w_br_nsa[0].astype(BF16), w_br_fox[0].astype(BF16)
    wa_pm, wb_pm = _pair_major_rows(wa_nat), _pair_major_rows(wb_nat)

    pblk = (1, 512)
    tq, tk = 256, 256
    x1 = _ffn_call(x_prompt, mod_p, gn, *ffn_w[0], 0, pblk)
    qn, qf, nsa_new, win_new, fox_new, small, kvb, cmpb = _inproj_call(
        x1, mod_p, gn, w1, gains, bf_pad, seg, wc, pblk, True)
    small3 = small.reshape(bp, lp, LANES)
    logf_p = small3[:, :, :FOX_HEADS]
    kv3 = kvb.reshape(bp, lp, 768)
    kaug_f, qx_f = _cumsum_call(small3, kv3, tq)

    ones_rows = (jnp.arange(VT_ROWS - HEAD_DIM) == 0).astype(BF16)[None, None, None, :, None]

    def dims_major(c):
        vt = kv3[:, :, c * LANES:(c + 1) * LANES].reshape(bp, lp // tk, tk, NSA_KV_HEADS, HEAD_DIM)
        vt = vt.transpose(0, 1, 3, 4, 2)
        pad = jnp.broadcast_to(ones_rows, (bp, lp // tk, NSA_KV_HEADS, VT_ROWS - HEAD_DIM, tk))
        return jnp.concatenate([vt, pad], axis=3)

    kx, cx = _position_lanes(lp)
    o_n = _nsa_prompt_call(qn.reshape(bp, lp, 512), kv3, kx, dims_major(1), dims_major(3),
                           cmpb.reshape(bp, lp // CMP_BLOCK, 256), cx[:lp // CMP_BLOCK], small3, tq, tk)
    o_f = _fox_prompt_call(qf.reshape(bp, lp, 512), qx_f, kaug_f, dims_major(5), tq, tk)
    x2 = _outproj_call(x1, mod_p, gn, o_n.reshape(bp * lp, 512), o_f.reshape(bp * lp, 512),
                       w_merge, wa_pm, wb_pm, wo, pblk)
    y_prompt = _ffn_call(x2, mod_p, gn, *ffn_w[1], 2, pblk)
    def rows_major(t, planes, rows):
        return t.reshape(bp, planes, NSA_KV_HEADS, HEAD_DIM, rows).transpose(0, 4, 1, 2, 3)[None]

    nwin = min(WINDOW, lp)
    nsa_p = rows_major(nsa_new, 4, lp)
    fox_p = rows_major(fox_new, 2, lp)
    win_p = rows_major(win_new[:, :, lp - nwin:], 2, nwin)

    sblk = (min(bs, 64), nq)
    pp = next(p for p in (16, 8, 4, 2, 1) if n_pages % p == 0)
    x1s = _ffn_call(x_sample, mod_s, gn, *ffn_w[0], 0, sblk)
    qn_s, qf_s, nsa_s, win_s, fox_s, small_s, _ = _inproj_call(
        x1s, mod_s, gn, w1, gains, bf_pad, seg, wc, sblk, False)
    nrow = NSA_HEADS * nq
    qn_dec = _to_decode_rows(qn_s, bs, nq)
    qf_dec = _to_decode_rows(qf_s, bs, nq)
    rows = jnp.arange(nrow)
    rowc = jnp.zeros((nrow, LANES), F32).at[:, 0].set(jnp.array(SLOPES, F32)[rows // nq]).at[:, 1].set(
        (rows % nq).astype(F32))
    sm3 = small_s.reshape(bs, nq, LANES)
    gates = sm3[:, :, FOX_HEADS:FOX_HEADS + 3 * NSA_HEADS].reshape(bs, nq, 3, NSA_HEADS)
    gates = jnp.pad(gates.transpose(0, 3, 1, 2).reshape(bs, nrow, 3), ((0, 0), (0, 0), (0, LANES - 3)))
    newlf = _pad_rows(sm3[:, :, :FOX_HEADS], LANES).transpose(0, 2, 1)
    newn = _pad_rows(nsa_s.reshape(bs, nq, 512)[:, :, 256:], LANES)
    newf = _pad_rows(fox_s.reshape(bs, nq, 256), LANES)
    wnew = _pad_rows(win_s.reshape(bs, nq, 256), LANES)
    nsa_t = cache_nsa_kv[0].transpose(0, 2, 3, 4, 1)
    fox_t = cache_fox_kv[0].transpose(0, 2, 3, 4, 1)
    wct = jnp.tile(w_cmp[0].transpose(0, 2, 1), (1, 1, page // CMP_BLOCK))
    kc = pp * page
    col_blk = jnp.arange(kc) // CMP_BLOCK
    bsum = (jnp.arange(kc // CMP_BLOCK)[:, None] == col_blk[None, :]).astype(BF16)
    nblocks = past_len // CMP_BLOCK
    hot = (jnp.arange(nblocks)[None, :, None]
           == (jnp.arange(n_pages // pp)[:, None, None] * (kc // CMP_BLOCK) + col_blk[None, None, :])).astype(BF16)
    lf_pool = cache_fox_logf[0].astype(F32).transpose(0, 2, 1)
    wbuf = state_win_kv[0].reshape(bs, nbuf, 256)
    pt_flat = page_table.reshape(-1).astype(jnp.int32)
    o_c, sel, ftot = _dec_cmp_call(pt_flat, qn_dec, rowc, wct, bsum, nsa_t, lf_pool, pp, n_pages, past_len)
    on_dec, of_dec, wst = _dec_main_call(pt_flat, qn_dec, qf_dec, sel, rowc, o_c, gates, ftot, newn, newf, newlf,
                                         wbuf, wnew, hot, nsa_t, fox_t, lf_pool, pp, n_pages, past_len, nq)
    o_ns = _from_decode_rows(on_dec, bs, nq).astype(BF16)
    o_fs = _from_decode_rows(of_dec, bs, nq).astype(BF16)
    x2s = _outproj_call(x1s, mod_s, gn, o_ns, o_fs, w_merge, wa_nat, wb_nat, wo, sblk)
    y_sample = _ffn_call(x2s, mod_s, gn, *ffn_w[1], 2, sblk)
    nsa_o = nsa_s.reshape(1, bs, nq, 4, NSA_KV_HEADS, HEAD_DIM)
    fox_o = fox_s.reshape(1, bs, nq, 2, FOX_KV_HEADS, HEAD_DIM)
    win_o = wst.reshape(1, bs, nbuf, 2, NSA_KV_HEADS, HEAD_DIM)

    return (y_prompt, y_sample, nsa_p, nsa_o, fox_p, fox_o,
            logf_p[None], sm3[None, :, :, :FOX_HEADS], win_p, win_o)
```

```python
import functools

import jax
import jax.numpy as jnp
from jax import lax
from jax.experimental import pallas as pl
from jax.experimental.pallas import tpu as pltpu

HEAD_DIM = 64
NSA_HEADS = 8
NSA_KV_HEADS = 2
FOX_HEADS = 8
FOX_KV_HEADS = 2
HPG = 4
CMP_BLOCK = 64
CMP_SHIFT = 6
SLC_TOPK = 8
WINDOW = 512
N_SUB = 3
RMS_EPS = 1e-6
NEG_INF = -1e30
FORCE_BONUS = 1e4
LANES = 128
PAIR_PERM = (0, 4, 1, 5, 2, 6, 3, 7)
SLOPES = tuple(2.0 ** -(h + 1) for h in range(NSA_HEADS))
VMEM_LIMIT = 56 * 1024 * 1024

F32 = jnp.float32
BF16 = jnp.bfloat16


def _nt(a, b):
    return lax.dot_general(a, b, (((1,), (1,)), ((), ())), preferred_element_type=F32)


def _nn(a, b):
    return jnp.dot(a, b, preferred_element_type=F32)


def _split3(x):
    hi = x.astype(BF16)
    r1 = x - hi.astype(F32)
    mid = r1.astype(BF16)
    lo = (r1 - mid.astype(F32)).astype(BF16)
    return hi, mid, lo


def _params(sem):
    return pltpu.CompilerParams(dimension_semantics=sem, vmem_limit_bytes=VMEM_LIMIT)


def _const_spec(shape):
    nd = len(shape)
    return pl.BlockSpec(shape, lambda *_: (0,) * nd, pipeline_mode=pl.Buffered(1))


def _mod_kernel(c_ref, w_ref, b_ref, o_ref):
    c = c_ref[...]
    a = (c * jax.nn.sigmoid(c)).astype(BF16)
    o_ref[...] = _nn(a, w_ref[...].astype(BF16)) + b_ref[...]


def _mod_call(c_all, w_ada, b_ada):
    bc, d = c_all.shape
    n = w_ada.shape[1]
    tn = 1024 if n % 1024 == 0 else n
    return pl.pallas_call(
        _mod_kernel,
        grid=(n // tn,),
        in_specs=[pl.BlockSpec((bc, d), lambda i: (0, 0)),
                  pl.BlockSpec((d, tn), lambda i: (0, i)),
                  pl.BlockSpec((1, tn), lambda i: (0, i))],
        out_specs=pl.BlockSpec((bc, tn), lambda i: (0, i)),
        out_shape=jax.ShapeDtypeStruct((bc, n), F32),
        compiler_params=_params(("parallel",)),
        name="adaln_mod",
    )(c_all, w_ada, b_ada.reshape(1, n))


def _pre(x, mod_ref, g_ref, sub):
    ms = jnp.mean(x * x, axis=-1, keepdims=True)
    y = x * lax.rsqrt(ms + RMS_EPS) * g_ref[sub:sub + 1, :]
    return y * (1.0 + mod_ref[:, 3 * sub + 1:3 * sub + 2, :]) + mod_ref[:, 3 * sub:3 * sub + 1, :]


def _ffn_kernel(x_ref, mod_ref, g_ref, wa_ref, wb_ref, wd_ref, o_ref, act_ref, *, sub, fc):
    x = x_ref[...]
    nb, nt, d = x.shape
    h = _pre(x, mod_ref, g_ref, sub).reshape(nb * nt, d).astype(BF16)
    dff = wa_ref.shape[1]
    for c in range(dff // fc):
        a = _nn(h, wa_ref[:, c * fc:(c + 1) * fc])
        b = _nn(h, wb_ref[:, c * fc:(c + 1) * fc])
        act_ref[:, c * fc:(c + 1) * fc] = (a * jax.nn.sigmoid(a) * b).astype(BF16)
    y = _nn(act_ref[...], wd_ref[...]).reshape(nb, nt, d)
    o_ref[...] = x + 0.5 * mod_ref[:, 3 * sub + 2:3 * sub + 3, :] * y


def _ffn_call(x3, mod, g_norm, wa, wb, wd, sub, blk):
    bs, lt, d = x3.shape
    nb, nt = blk
    dff = wa.shape[1]
    fc = 256 if dff % 256 == 0 else dff
    grid = (bs // nb, lt // nt)
    return pl.pallas_call(
        functools.partial(_ffn_kernel, sub=sub, fc=fc),
        grid=grid,
        in_specs=[pl.BlockSpec((nb, nt, d), lambda i, j: (i, j, 0)),
                  pl.BlockSpec((nb, 3 * N_SUB, d), lambda i, j: (i, 0, 0)),
                  _const_spec(g_norm.shape), _const_spec(wa.shape), _const_spec(wb.shape),
                  _const_spec(wd.shape)],
        out_specs=pl.BlockSpec((nb, nt, d), lambda i, j: (i, j, 0)),
        out_shape=jax.ShapeDtypeStruct(x3.shape, F32),
        scratch_shapes=[pltpu.VMEM((nb * nt, dff), BF16)],
        compiler_params=_params(("parallel", "parallel")),
        name="ffn_sub%d" % sub,
    )(x3, mod, g_norm, wa, wb, wd)


N_NORMED = 12


def _inproj_kernel(x_ref, mod_ref, g_ref, w_ref, gain_ref, bf_ref, seg_ref, *rest, with_cmp):
    if with_cmp:
        wc_ref, qn_ref, qf_ref, nsa_ref, win_ref, fox_ref, sm_ref, kv_ref, cmp_ref, *vt_refs = rest
    else:
        qn_ref, qf_ref, nsa_ref, win_ref, fox_ref, sm_ref, kv_ref = rest
    x = x_ref[...]
    nb, nt, d = x.shape
    h = _pre(x, mod_ref, g_ref, 1).reshape(nb * nt, d).astype(BF16)
    z = _nn(h, w_ref[...])
    seg = seg_ref[...]
    normed = []
    for c in range(N_NORMED):
        zc = z[:, c * LANES:(c + 1) * LANES]
        zz = zc * zc
        hi = zz.astype(BF16)
        lo = (zz - hi.astype(F32)).astype(BF16)
        ss = _nn(hi, seg) + _nn(lo, seg)
        normed.append(zc * lax.rsqrt(ss * (1.0 / HEAD_DIM) + RMS_EPS) * gain_ref[:, c * LANES:(c + 1) * LANES])
    raw = [z[:, (N_NORMED + c) * LANES:(N_NORMED + c + 1) * LANES] for c in range(4)]
    k_cmp, k_slc, k_win, k_fox = normed[8:12]
    v_cmp, v_slc, v_win, v_fox = raw
    for j in range(4):
        qn_ref[:, j * LANES:(j + 1) * LANES] = normed[j].astype(BF16)
        qf_ref[:, j * LANES:(j + 1) * LANES] = normed[4 + j].astype(BF16)
    def put(ref, c, t, vt_ref=None):
        if not with_cmp:
            ref[:, c * LANES:(c + 1) * LANES] = t
            return
        tt = t.T
        ref[c * LANES:(c + 1) * LANES, :] = tt
        if vt_ref is not None:
            n_kt, n_g, vrows, tkv = vt_ref.shape
            tail = jnp.where(lax.broadcasted_iota(jnp.int32, (vrows - HEAD_DIM, tkv), 0) == 0, 1.0, 0.0).astype(BF16)
            for kt in range(n_kt):
                for g in range(n_g):
                    vt_ref[kt, g, :HEAD_DIM, :] = tt[g * HEAD_DIM:(g + 1) * HEAD_DIM,
                                                      kt * tkv:(kt + 1) * tkv].astype(BF16)
                    vt_ref[kt, g, HEAD_DIM:, :] = tail

    vts = vt_refs if with_cmp else (None, None, None)
    for c, t in enumerate((k_cmp, v_cmp, k_slc)):
        put(nsa_ref, c, t)
    put(nsa_ref, 3, v_slc, vts[0])
    put(win_ref, 0, k_win)
    put(win_ref, 1, v_win, vts[1])
    put(fox_ref, 0, k_fox)
    put(fox_ref, 1, v_fox, vts[2])
    for c, t in enumerate((k_slc, k_win, k_fox)):
        kv_ref[:, c * LANES:(c + 1) * LANES] = t.astype(BF16)
    zs = z[:, 16 * LANES:17 * LANES]
    lane = lax.broadcasted_iota(jnp.int32, zs.shape, 1)
    zf = zs + bf_ref[...]
    logf = jnp.minimum(zf, 0.0) - jnp.log1p(jnp.exp(-jnp.abs(zf)))
    sm_ref[...] = jnp.where(lane < FOX_HEADS, logf, jax.nn.sigmoid(zs))
    if with_cmp:
        rows = nb * nt
        kvc = jnp.concatenate([k_cmp, v_cmp], axis=1).reshape(rows // CMP_BLOCK, CMP_BLOCK, 2 * LANES)
        cmp_ref[...] = jnp.sum(kvc * wc_ref[...][None], axis=1)


def _inproj_call(x3, mod, g_norm, w1, gains, bf_pad, seg, wc, blk, with_cmp, tk=None):
    bs, lt, d = x3.shape
    nb, nt = blk
    rows = nb * nt
    t = bs * lt
    grid = (bs // nb, lt // nt)
    nj = lt // nt

    def tok(cols):
        return pl.BlockSpec((rows, cols), lambda i, j: (i * nj + j, 0))

    in_specs = [pl.BlockSpec((nb, nt, d), lambda i, j: (i, j, 0)),
                pl.BlockSpec((nb, 3 * N_SUB, d), lambda i, j: (i, 0, 0)),
                _const_spec(g_norm.shape), _const_spec(w1.shape), _const_spec(gains.shape),
                _const_spec(bf_pad.shape), _const_spec(seg.shape)]
    args = [x3, mod, g_norm, w1, gains, bf_pad, seg]
    out_specs = [tok(512), tok(512), tok(512), tok(256), tok(256), tok(128), tok(384)]
    out_shape = [jax.ShapeDtypeStruct((t, 512), BF16), jax.ShapeDtypeStruct((t, 512), BF16),
                 jax.ShapeDtypeStruct((t, 512), F32), jax.ShapeDtypeStruct((t, 256), F32),
                 jax.ShapeDtypeStruct((t, 256), F32), jax.ShapeDtypeStruct((t, 128), F32),
                 jax.ShapeDtypeStruct((t, 384), BF16)]
    if with_cmp:
        assert nb == 1
        for k, cols in ((2, 512), (3, 256), (4, 256)):
            out_specs[k] = pl.BlockSpec((None, cols, nt), lambda i, j: (i, 0, j))
            out_shape[k] = jax.ShapeDtypeStruct((bs, cols, lt), F32)
        in_specs.append(_const_spec(wc.shape))
        args.append(wc)
        out_specs.append(pl.BlockSpec((rows // CMP_BLOCK, 256), lambda i, j: (i * nj + j, 0)))
        out_shape.append(jax.ShapeDtypeStruct((t // CMP_BLOCK, 256), F32))
        for _ in range(3):
            out_specs.append(pl.BlockSpec((None, nt // tk, NSA_KV_HEADS, VT_ROWS, tk), lambda i, j: (i, j, 0, 0, 0)))
            out_shape.append(jax.ShapeDtypeStruct((bs, lt // tk, NSA_KV_HEADS, VT_ROWS, tk), BF16))
    return pl.pallas_call(
        functools.partial(_inproj_kernel, with_cmp=with_cmp),
        grid=grid, in_specs=in_specs, out_specs=out_specs, out_shape=out_shape,
        compiler_params=_params(("parallel", "parallel")),
        name="in_proj",
    )(*args)


FQ0 = 0
FK0 = 3 * FOX_HEADS


def _cumsum_kernel(tok_ref, k_ref, kaug_ref, qx_ref, carry_ref):
    @pl.when(pl.program_id(1) == 0)
    def _():
        carry_ref[...] = jnp.zeros_like(carry_ref)

    tc = tok_ref.shape[0]
    r = lax.broadcasted_iota(jnp.int32, (tc, tc), 0)
    c = lax.broadcasted_iota(jnp.int32, (tc, tc), 1)
    lower = jnp.where(r >= c, 1.0, 0.0).astype(BF16)
    cs = carry_ref[...] + sum(_nn(lower, p) for p in _split3(tok_ref[...]))
    carry_ref[...] = cs[tc - 1:tc, :]
    hrow = lax.broadcasted_iota(jnp.int32, (LANES, LANES), 0)
    col = lax.broadcasted_iota(jnp.int32, (LANES, LANES), 1)
    placed = jnp.zeros((tc, LANES), F32)
    for p, piece in enumerate(_split3(cs)):
        tgt = 3 * hrow + p
        place = jnp.where((hrow < FOX_HEADS) & ((col == FQ0 + tgt) | (col == FK0 + tgt)), 1.0, 0.0).astype(BF16)
        placed = placed + _nn(piece, place)
    lane = lax.broadcasted_iota(jnp.int32, (tc, LANES), 1)
    in_q = lane < FK0
    in_k = (lane >= FK0) & (lane < 2 * FK0)
    qx_ref[...] = jnp.where(in_q, placed, jnp.where(in_k, 1.0, 0.0)).astype(BF16)
    kaug_ref[:, :LANES] = k_ref[...]
    kaug_ref[:, LANES:] = jnp.where(in_q, 1.0, jnp.where(in_k, -placed, 0.0)).astype(BF16)


def _cumsum_call(small3, kv3, tc):
    b, l, _ = small3.shape
    return pl.pallas_call(
        _cumsum_kernel,
        grid=(b, l // tc),
        in_specs=[pl.BlockSpec((None, tc, LANES), lambda i, j: (i, j, 0)),
                  pl.BlockSpec((None, tc, LANES), lambda i, j: (i, j, 2))],
        out_specs=[pl.BlockSpec((None, tc, 2 * LANES), lambda i, j: (i, j, 0)),
                   pl.BlockSpec((None, tc, LANES), lambda i, j: (i, j, 0))],
        out_shape=[jax.ShapeDtypeStruct((b, l, 2 * LANES), BF16),
                   jax.ShapeDtypeStruct((b, l, LANES), BF16)],
        scratch_shapes=[pltpu.VMEM((1, LANES), F32)],
        compiler_params=_params(("parallel", "arbitrary")),
        name="fox_cumsum",
    )(small3, kv3)


def _flash_step(s, v, m, l, acc):
    m_new = jnp.maximum(m, jnp.max(s, axis=-1, keepdims=True))
    alpha = jnp.exp(m - m_new)
    p = jnp.exp(s - m_new)
    l = alpha * l + jnp.sum(p, axis=-1, keepdims=True)
    acc = alpha * acc + _nn(p.astype(BF16), v)
    return m_new, l, acc


def _flash_init(rows):
    return (jnp.full((rows, 1), NEG_INF, F32), jnp.zeros((rows, 1), F32), jnp.zeros((rows, LANES), F32))


def _group_masked(qp, g):
    lane = lax.broadcasted_iota(jnp.int32, qp.shape, 1)
    keep = (lane < HEAD_DIM) if g == 0 else (lane >= HEAD_DIM)
    return jnp.where(keep, qp, jnp.zeros_like(qp))


def _top_blocks(score, n_pick, axis=1):
    idx = lax.broadcasted_iota(jnp.int32, score.shape, axis).astype(F32)
    sel = jnp.zeros(score.shape, F32)
    for _ in range(n_pick):
        mx = jnp.max(score, axis=axis, keepdims=True)
        first = jnp.min(jnp.where(score == mx, idx, 1e9), axis=axis, keepdims=True)
        pick = idx == first
        sel = jnp.where(pick, 1.0, sel)
        score = jnp.where(pick, -jnp.inf, score)
    return sel


XL_ONE_A = CMP_BLOCK
XL_ONE_B = CMP_BLOCK + 1
XL_HI = CMP_BLOCK + 2
XL_LO = CMP_BLOCK + 3
MASK_BIG = 1e30
N_FREE = SLC_TOPK - 3


VT_ROWS = HEAD_DIM + 16


def _col_max(s):
    k = s.shape[0]
    if k % 64 == 0:
        s = jnp.max(s.reshape(8, k // 8, s.shape[1]), axis=0)
    return jnp.max(s, axis=0, keepdims=True)


def _heads_scores(k, qa_ref, ok):
    out = []
    for h in range(qa_ref.shape[0]):
        s = _nt(k, qa_ref[h])
        out.append(s if ok is None else jnp.where(ok, s, NEG_INF))
    return out


def _heads_stage(scores, s_ref):
    for h, s in enumerate(scores):
        s_ref[h] = s


def _heads_consume(vt, s_ref, m_ref, acc_ref):
    nh = s_ref.shape[0]
    probs, alphas = [], []
    for h in range(nh):
        s = s_ref[h]
        m_old = m_ref[h:h + 1, :]
        m_new = jnp.maximum(m_old, _col_max(s))
        m_ref[h:h + 1, :] = m_new
        alphas.append(jnp.exp(m_old - m_new))
        probs.append(jnp.exp(s - m_new).astype(BF16))
    for h in range(nh):
        acc_ref[h] = alphas[h] * acc_ref[h] + _nn(vt[h // HPG], probs[h])


def _heads_run(k_of, vt_ref, qa_ref, s_ref, m_ref, acc_ref, lo, qi, mask_of, mask_all, need_ref=None):
    m_ref[...] = jnp.full(m_ref.shape, NEG_INF, F32)
    acc_ref[...] = jnp.zeros_like(acc_ref)
    _heads_stage(_heads_scores(k_of(qi), qa_ref, mask_of(qi)), s_ref)

    def step(kj, staged):
        new = _heads_scores(k_of(kj), qa_ref, mask_of(kj) if mask_all else None)
        _heads_consume(vt_ref[staged], s_ref, m_ref, acc_ref)
        _heads_stage(new, s_ref)
        return kj

    def body(kj, staged):
        if need_ref is None:
            return step(kj, staged)
        wanted = lax.shift_right_logical(need_ref[0], kj) & 1
        return lax.cond(wanted > 0, lambda st: step(kj, st), lambda st: st, staged)

    staged = lax.fori_loop(lo, qi, body, qi)
    _heads_consume(vt_ref[staged], s_ref, m_ref, acc_ref)


def _attn_scratch(tq, tk):
    return [pltpu.VMEM((NSA_HEADS, tq, 2 * LANES), BF16), pltpu.VMEM((NSA_HEADS, tq, 2 * LANES), BF16),
            pltpu.VMEM((NSA_HEADS, tq), F32), pltpu.VMEM((NSA_HEADS, tk, tq), F32),
            pltpu.VMEM((NSA_HEADS, VT_ROWS, tq), F32), pltpu.VMEM((NSA_HEADS, HEAD_DIM, tq), F32)]


def _finish_t(acc):
    return acc[:HEAD_DIM, :] / acc[HEAD_DIM:HEAD_DIM + 1, :]


def _nsa_prompt_kernel(q_ref, ks_ref, kw_ref, kx_ref, vst_ref, vwt_ref, cmp_ref, cx_ref, sm_ref, o_ref,
                       qas_ref, qaw_ref, m_ref, s_ref, acc_ref, out_ref, need_ref, *, tq, tk, nb):
    qi = pl.program_id(1)
    lane = lax.broadcasted_iota(jnp.int32, (tq, LANES), 1)
    t2 = qi * tq + lax.broadcasted_iota(jnp.int32, (tq, LANES), 0)
    t_hi = lax.shift_right_logical(t2, CMP_SHIFT).astype(F32)
    t_lo = (t2 & (CMP_BLOCK - 1)).astype(F32)
    alibi_q = jnp.where(lane == XL_ONE_A, -CMP_BLOCK * t_hi,
                        jnp.where(lane == XL_ONE_B, -t_lo,
                                  jnp.where(lane == XL_HI, float(CMP_BLOCK), jnp.where(lane == XL_LO, 1.0, 0.0))))
    krow = lax.broadcasted_iota(jnp.int32, (tk, tq), 0)
    qcol = lax.broadcasted_iota(jnp.int32, (tk, tq), 1)

    cmpv = cmp_ref[...]
    ck_aug = jnp.concatenate([cmpv[:, :LANES].astype(BF16), cx_ref[...]], axis=1)
    cvt = jnp.concatenate([cmpv[:, LANES:], jnp.zeros((LANES - nb, LANES), F32)],
                          axis=0).T[:, :nb].astype(BF16)
    brow = lax.broadcasted_iota(jnp.int32, (nb, tq), 0)
    tcol = qi * tq + lax.broadcasted_iota(jnp.int32, (nb, tq), 1)
    valid_c = (brow + 1) * CMP_BLOCK - 1 <= tcol
    cur = lax.shift_right_logical(tcol, CMP_SHIFT)
    forced = (brow == 0) | (brow == cur) | (brow == cur - 1)
    free = (brow * CMP_BLOCK <= tcol) & jnp.logical_not(forced)
    smt = sm_ref[...].T

    qx = [SLOPES[h] * alibi_q for h in range(NSA_HEADS)]
    for g in range(NSA_KV_HEADS):
        for j in range(HPG):
            h = g * HPG + j
            qaw_ref[h, :, :LANES] = _group_masked(q_ref[:, j * LANES:(j + 1) * LANES], g)
            qaw_ref[h, :, LANES:] = qx[h].astype(BF16)
    s_c = [jnp.where(valid_c, _nt(ck_aug, qaw_ref[h]), NEG_INF) for h in range(NSA_HEADS)]
    p_c = []
    for s in s_c:
        e = jnp.exp(s - jnp.max(s, axis=0, keepdims=True))
        p_c.append(jnp.where(valid_c, e / jnp.sum(e, axis=0, keepdims=True), 0.0))
    for h in range(NSA_HEADS):
        g = h // HPG
        out_ref[h] = smt[FOX_HEADS + h:FOX_HEADS + h + 1, :] * _nn(cvt[g * HEAD_DIM:(g + 1) * HEAD_DIM, :],
                                                                 p_c[h].astype(BF16))
    imp = jnp.stack([sum(p_c[g * HPG:(g + 1) * HPG]) for g in range(NSA_KV_HEADS)])
    score = jnp.where(free[None], imp, jnp.where(forced[None], 2 * NEG_INF, NEG_INF))
    sel_all = jnp.where(forced[None], 1.0, _top_blocks(score, N_FREE, axis=1))
    picked = jnp.zeros((8, LANES), F32)
    for g in range(NSA_KV_HEADS):
        sel = jnp.concatenate([sel_all[g], jnp.zeros((LANES - nb, tq), F32)], axis=0).T
        sel_q = jnp.where(lane < CMP_BLOCK, (sel - 1.0) * MASK_BIG, 0.0)
        picked = jnp.maximum(picked, jnp.max(sel, axis=0, keepdims=True))
        for j in range(HPG):
            h = g * HPG + j
            qas_ref[h, :, :LANES] = qaw_ref[h, :, :LANES]
            qas_ref[h, :, LANES:] = (qx[h] + sel_q).astype(BF16)

    assert tq == tk, "the pipelined key loop walks one key tile per query tile"
    bpt = tk // CMP_BLOCK
    assert bpt & (bpt - 1) == 0 and nb // bpt <= 24, "tile bits must be exact in f32"
    sh = 1
    while sh < bpt:
        picked = jnp.maximum(picked, pltpu.roll(picked, sh, axis=1))
        sh *= 2
    blane = lax.broadcasted_iota(jnp.int32, (8, LANES), 1)
    weight = jnp.where(((blane & (bpt - 1)) == bpt - 1) & (blane < nb),
                       lax.shift_left(jnp.ones_like(blane), lax.shift_right_logical(blane, bpt.bit_length() - 1)),
                       0).astype(F32)
    bits = jnp.sum(picked[0:1, :] * weight[0:1, :], axis=1, keepdims=True)
    need_ref[0] = bits[0, 0].astype(jnp.int32)

    def run_branch(qa_ref, k_ref, vt_ref, lo, window, gate_row, need=None):
        def k_of(kj):
            off = pl.multiple_of(kj * tk, tk)
            return jnp.concatenate([k_ref[pl.ds(off, tk), :], kx_ref[pl.ds(off, tk), :]], axis=1)

        def mask_of(kj):
            dist = (qi - kj) * tq + qcol - krow
            return (dist >= 0) & (dist < WINDOW) if window else dist >= 0

        _heads_run(k_of, vt_ref, qa_ref, s_ref, m_ref, acc_ref, lo, qi, mask_of, window, need)
        for h in range(NSA_HEADS):
            r = gate_row + h
            out_ref[h] = out_ref[h] + smt[r:r + 1, :] * _finish_t(acc_ref[h])

    run_branch(qas_ref, ks_ref, vst_ref, 0, False, FOX_HEADS + NSA_HEADS, need_ref)
    run_branch(qaw_ref, kw_ref, vwt_ref, jnp.maximum(qi - (WINDOW + tk - 1) // tk, 0), True,
               FOX_HEADS + 2 * NSA_HEADS)
    for j in range(HPG):
        pair_t = jnp.concatenate([out_ref[j], out_ref[HPG + j]], axis=0)
        o_ref[:, j * LANES:(j + 1) * LANES] = pair_t.T.astype(BF16)


def _nsa_prompt_call(qn3, kv3, kx, vst, vwt, cmp3, cx, small3, tq, tk):
    b, l, _ = qn3.shape
    nb = cmp3.shape[1]
    assert nb <= CMP_BLOCK, "one-hot block lanes cover at most CMP_BLOCK blocks"

    def kspec(c):
        return pl.BlockSpec((None, l, LANES), lambda i, j: (i, 0, c))

    vspec = pl.BlockSpec((None, l // tk, NSA_KV_HEADS, VT_ROWS, tk), lambda i, j: (i, 0, 0, 0, 0))
    return pl.pallas_call(
        functools.partial(_nsa_prompt_kernel, tq=tq, tk=tk, nb=nb),
        grid=(b, l // tq),
        in_specs=[pl.BlockSpec((None, tq, 512), lambda i, j: (i, j, 0)),
                  kspec(0), kspec(1), pl.BlockSpec((l, LANES), lambda i, j: (0, 0)), vspec, vspec,
                  pl.BlockSpec((None, nb, 256), lambda i, j: (i, 0, 0)),
                  pl.BlockSpec((nb, LANES), lambda i, j: (0, 0)),
                  pl.BlockSpec((None, tq, LANES), lambda i, j: (i, j, 0))],
        out_specs=pl.BlockSpec((None, tq, 512), lambda i, j: (i, j, 0)),
        out_shape=jax.ShapeDtypeStruct((b, l, 512), BF16),
        scratch_shapes=_attn_scratch(tq, tk) + [pltpu.SMEM((1,), jnp.int32)],
        compiler_params=_params(("parallel", "arbitrary")),
        name="nsa_prompt",
    )(qn3, kv3, kv3, kx, vst, vwt, cmp3, cx, small3)


def _fox_prompt_kernel(q_ref, qx_ref, k_ref, vt_ref, o_ref, qa_ref, m_ref, s_ref, acc_ref, *, tq, tk):
    assert tq == tk, "the pipelined key loop walks one key tile per query tile"
    qi = pl.program_id(1)
    lane = lax.broadcasted_iota(jnp.int32, (tq, LANES), 1)
    krow = lax.broadcasted_iota(jnp.int32, (tk, tq), 0)
    qcol = lax.broadcasted_iota(jnp.int32, (tk, tq), 1)
    qx = qx_ref[...]
    for g in range(FOX_KV_HEADS):
        for j in range(HPG):
            h = g * HPG + j
            own = ((lane >= FQ0 + 3 * h) & (lane < FQ0 + 3 * h + 3)) | ((lane >= FK0 + 3 * h) & (lane < FK0 + 3 * h + 3))
            qa_ref[h, :, :LANES] = _group_masked(q_ref[:, j * LANES:(j + 1) * LANES], g)
            qa_ref[h, :, LANES:] = jnp.where(own, qx, jnp.zeros_like(qx))
    _heads_run(lambda kj: k_ref[pl.ds(pl.multiple_of(kj * tk, tk), tk), :], vt_ref, qa_ref, s_ref, m_ref, acc_ref,
               0, qi, lambda kj: (qi - kj) * tq + qcol - krow >= 0, False)
    for j in range(HPG):
        pair_t = jnp.concatenate([_finish_t(acc_ref[j]), _finish_t(acc_ref[HPG + j])], axis=0)
        o_ref[:, j * LANES:(j + 1) * LANES] = pair_t.T.astype(BF16)


def _fox_prompt_call(qf3, qx3, kaug3, vft, tq, tk):
    b, l, _ = qf3.shape
    return pl.pallas_call(
        functools.partial(_fox_prompt_kernel, tq=tq, tk=tk),
        grid=(b, l // tq),
        in_specs=[pl.BlockSpec((None, tq, 512), lambda i, j: (i, j, 0)),
                  pl.BlockSpec((None, tq, LANES), lambda i, j: (i, j, 0)),
                  pl.BlockSpec((None, l, 2 * LANES), lambda i, j: (i, 0, 0)),
                  pl.BlockSpec((None, l // tk, FOX_KV_HEADS, VT_ROWS, tk), lambda i, j: (i, 0, 0, 0, 0))],
        out_specs=pl.BlockSpec((None, tq, 512), lambda i, j: (i, j, 0)),
        out_shape=jax.ShapeDtypeStruct((b, l, 512), BF16),
        scratch_shapes=_attn_scratch(tq, tk)[1:5],
        compiler_params=_params(("parallel", "arbitrary")),
        name="fox_prompt",
    )(qf3, qx3, kaug3, vft)


def _lane_cumsum(x):
    n = x.shape[1]
    r = lax.broadcasted_iota(jnp.int32, (n, n), 0)
    c = lax.broadcasted_iota(jnp.int32, (n, n), 1)
    upper = jnp.where(r <= c, 1.0, 0.0).astype(BF16)
    return sum(_nn(p, upper) for p in _split3(x))


def _dec_cmp_kernel(pt_ref, q_ref, rowc_ref, wct_ref, bsum_ref, *rest, pp, nb, past_len):
    del pt_ref
    pages = rest[:pp]
    lfs = rest[pp:2 * pp]
    oc_ref, sel_ref, ftot_ref, chunk_ref, cmp_scr, ftot_scr = rest[2 * pp:]
    c = pl.program_id(1)

    @pl.when(c == 0)
    def _():
        ftot_scr[...] = jnp.zeros_like(ftot_scr)

    nblk = bsum_ref.shape[0]
    bsum = bsum_ref[...]
    for plane in range(2):
        w = wct_ref[plane]
        for g in range(NSA_KV_HEADS):
            prod = jnp.concatenate([(p[plane, g] * w).astype(BF16) for p in pages], axis=1)
            blk = _nt(bsum, prod)
            cmp_scr[2 * plane + g, pl.ds(pl.multiple_of(c * nblk, nblk), nblk), :] = blk
    tot = lfs[0][...]
    for p in lfs[1:]:
        tot = tot + p[...]
    ftot_scr[...] += jnp.broadcast_to(jnp.sum(tot, axis=-1, keepdims=True), ftot_scr.shape)

    @pl.when(c == pl.num_programs(1) - 1)
    def _():
        ftot_ref[...] = ftot_scr[...]
        ckb = jnp.concatenate([cmp_scr[0], cmp_scr[1]], axis=1).astype(BF16)
        cvb = jnp.concatenate([cmp_scr[2], cmp_scr[3]], axis=1).astype(BF16)
        nrow = q_ref.shape[0]
        rowc = rowc_ref[...]
        slope = rowc[:, 0:1]
        t = past_len + rowc[:, 1:2]
        nidx = lax.broadcasted_iota(jnp.int32, (nrow, nb), 1)
        cmid = nidx.astype(F32) * CMP_BLOCK + 0.5 * (CMP_BLOCK - 1)
        valid = cmid + 0.5 * (CMP_BLOCK - 1) <= t
        s = jnp.where(valid, _nt(q_ref[...], ckb) - slope * (t - cmid), NEG_INF)
        e = jnp.exp(s - jnp.max(s, axis=-1, keepdims=True))
        p = jnp.where(valid, e / jnp.sum(e, axis=-1, keepdims=True), 0.0)
        oc_ref[...] = _nn(p.astype(BF16), cvb)
        nq = nrow // NSA_HEADS
        n8 = lax.broadcasted_iota(jnp.int32, (nq, nb), 1)
        forced = (n8 == 0) | (n8 == nb - 1)
        sel_rows = []
        for g in range(NSA_KV_HEADS):
            imp = jnp.zeros((nq, nb), F32)
            for j in range(HPG):
                r0 = (g * HPG + j) * nq
                imp = imp + p[r0:r0 + nq, :]
            sel = _top_blocks(imp + jnp.where(forced, FORCE_BONUS, 0.0), SLC_TOPK - 1)
            sel_rows += [sel] * HPG
        sel_all = jnp.concatenate(sel_rows, axis=0)
        sel_ref[...] = ((sel_all - 1.0) * MASK_BIG).astype(BF16)
        any_blk = jnp.max(sel_all, axis=0, keepdims=True)
        blane = lax.broadcasted_iota(jnp.int32, any_blk.shape, 1)
        flane = lax.broadcasted_iota(jnp.int32, chunk_ref.shape, 1)
        flags = jnp.zeros(chunk_ref.shape, F32)
        nblk = bsum_ref.shape[0]
        for ci in range(nb // nblk):
            hit = jnp.max(jnp.where((blane >= ci * nblk) & (blane < (ci + 1) * nblk), any_blk, 0.0),
                          axis=1, keepdims=True)
            flags = jnp.where(flane == ci, hit, flags)
        chunk_ref[...] = flags.astype(jnp.int32)


def _page_specs(pp, n_pages, block, plane_pair, gated=False):
    tail = (0,) * (len(block) - 2)
    nc = n_pages // pp

    def one(i):
        if gated:
            return pl.BlockSpec(block, lambda b, c, pt, need: (
                pt[b * n_pages + jnp.where(need[b * nc + c] > 0, c, 0) * pp + i], plane_pair) + tail)
        return pl.BlockSpec(block, lambda b, c, pt, *_: (pt[b * n_pages + c * pp + i], plane_pair) + tail)
    return [one(i) for i in range(pp)]


def _dec_cmp_call(pt_flat, q_dec, rowc, wct, bsum, nsa_t, lf_pool, pp, n_pages, past_len):
    b, nrow, _ = q_dec.shape
    nb = past_len // CMP_BLOCK
    page = nsa_t.shape[-1]
    nc = n_pages // pp
    assert nb <= LANES, "block mask lanes cover at most 128 cached blocks"
    per = lambda shape: pl.BlockSpec((None,) + shape, lambda i, c, pt: (i, 0, 0))
    cst = lambda shape: pl.BlockSpec(shape, lambda i, c, pt: (0,) * len(shape))
    grid_spec = pltpu.PrefetchScalarGridSpec(
        num_scalar_prefetch=1,
        grid=(b, nc),
        in_specs=[per((nrow, LANES)), cst(rowc.shape), cst(wct.shape), cst(bsum.shape)]
        + _page_specs(pp, n_pages, (None, 2, NSA_KV_HEADS, HEAD_DIM, page), 0)
        + _page_specs(pp, n_pages, (None, FOX_HEADS, page), 0),
        out_specs=[per((nrow, LANES)), per((nrow, nb)), per((FOX_HEADS, LANES)), per((8, LANES))],
        scratch_shapes=[pltpu.VMEM((4, nb, HEAD_DIM), F32), pltpu.VMEM((FOX_HEADS, LANES), F32)],
    )
    return pl.pallas_call(
        functools.partial(_dec_cmp_kernel, pp=pp, nb=nb, past_len=past_len),
        grid_spec=grid_spec,
        out_shape=[jax.ShapeDtypeStruct((b, nrow, LANES), F32),
                   jax.ShapeDtypeStruct((b, nrow, nb), BF16),
                   jax.ShapeDtypeStruct((b, FOX_HEADS, LANES), F32),
                   jax.ShapeDtypeStruct((b, 8, LANES), jnp.int32)],
        compiler_params=_params(("parallel", "arbitrary")),
        name="decode_cmp",
    )(pt_flat, q_dec, rowc, wct, bsum, *([nsa_t] * pp), *([lf_pool] * pp))


def _rep_rows(x, n):
    return jnp.concatenate([jnp.broadcast_to(x[h:h + 1, :], (n, x.shape[1])) for h in range(x.shape[0])], axis=0)


def _scr_flash_t(scores, vts, stats):
    staged = []
    for s, (m_ref, l_ref, _) in zip(scores, stats):
        m_old = m_ref[:, 0:1]
        m = jnp.maximum(m_old, jnp.max(s, axis=-1, keepdims=True))
        alpha = jnp.exp(m_old - m)
        p = jnp.exp(s - m)
        l = alpha * l_ref[:, 0:1] + jnp.sum(p, axis=-1, keepdims=True)
        m_ref[...] = jnp.broadcast_to(m, m_ref.shape)
        l_ref[...] = jnp.broadcast_to(l, l_ref.shape)
        staged.append((alpha, p.astype(BF16)))
    for (alpha, p), vt, (_, _, acc_ref) in zip(staged, vts, stats):
        acc_ref[...] = alpha * acc_ref[...] + _nt(p, vt)


def _pages_t(pages, plane):
    return jnp.concatenate(
        [jnp.concatenate([p[plane, g] for g in range(NSA_KV_HEADS)], axis=0) for p in pages], axis=1).astype(BF16)


def _dec_main_kernel(pt_ref, need_ref, qn_ref, qf_ref, sel_ref, rowc_ref, oc_ref, gate_ref, ftot_ref,
                     newn_ref, newf_ref, newlf_ref, wbuf_ref, wnew_ref, hot_ref, *rest, pp, nb, past_len, nq):
    del pt_ref
    slc = rest[:pp]
    fox = rest[pp:2 * pp]
    lfs = rest[2 * pp:3 * pp]
    (on_ref, of_ref, wst_ref, ms_ref, ls_ref, as_ref, mf_ref, lf_ref, af_ref, fcar_ref, ftc_ref,
     qna_ref) = rest[3 * pp:]
    c = pl.program_id(1)
    nrow = qn_ref.shape[0]
    page = lfs[0].shape[1]
    kc = pp * page
    rowc = rowc_ref[...]
    slope = rowc[:, 0:1]
    t = past_len + rowc[:, 1:2]
    qn = qn_ref[...]
    qf = qf_ref[...]

    @pl.when(c == 0)
    def _():
        for r in (ms_ref, mf_ref):
            r[...] = jnp.full(r.shape, NEG_INF, F32)
        for r in (ls_ref, as_ref, lf_ref, af_ref, fcar_ref):
            r[...] = jnp.zeros_like(r)
        f_new = ftot_ref[...] + _lane_cumsum(newlf_ref[...])
        rep = _rep_rows(f_new, nq)
        lane = lax.broadcasted_iota(jnp.int32, rep.shape, 1)
        tok = lax.broadcasted_iota(jnp.int32, rep.shape, 0) & (nq - 1)
        ftc_ref[...] = jnp.broadcast_to(
            jnp.sum(jnp.where(lane == tok, rep, 0.0), axis=-1, keepdims=True), ftc_ref.shape)
        qna_ref[:, :LANES] = qn
        qna_ref[:, LANES:] = sel_ref[...]

    ft = ftc_ref[:, 0:1]

    s_fox = _nn(qf, _pages_t(fox, 0))
    cs = _lane_cumsum(jnp.concatenate([p[...] for p in lfs], axis=0))
    car = fcar_ref[:, 0:1]
    fs_parts = []
    for i in range(pp):
        cs_i = cs[i * FOX_HEADS:(i + 1) * FOX_HEADS, :]
        fs_parts.append(_rep_rows(car + cs_i, nq))
        car = car + cs_i[:, page - 1:page]
    fcar_ref[...] = jnp.broadcast_to(car, fcar_ref.shape)
    s_fox = s_fox + (ft - jnp.concatenate(fs_parts, axis=1))
    fox_stats = (mf_ref, lf_ref, af_ref)

    needed = need_ref[pl.program_id(0) * pl.num_programs(1) + c] > 0

    @pl.when(needed)
    def _():
        kts = jnp.concatenate([_pages_t(slc, 0), hot_ref[...]], axis=0)
        pos = (c * kc + lax.broadcasted_iota(jnp.int32, (1, kc), 1)).astype(F32)
        s_slc = _nn(qna_ref[...], kts) - slope * (t - pos)
        _scr_flash_t([s_slc, s_fox], [_pages_t(slc, 1), _pages_t(fox, 1)], [(ms_ref, ls_ref, as_ref), fox_stats])

    @pl.when(jnp.logical_not(needed))
    def _():
        _scr_flash_t([s_fox], [_pages_t(fox, 1)], [fox_stats])

    @pl.when(c == pl.num_programs(1) - 1)
    def _():
        npad = newn_ref.shape[0]
        lane_i = lax.broadcasted_iota(jnp.int32, (1, npad), 1)
        posn = (past_len + lane_i).astype(F32)
        okn = (lane_i < nq) & (posn <= t)
        newn = newn_ref[...]
        sn = jnp.where(okn, _nt(qn, newn[:, :LANES].astype(BF16)) - slope * (t - posn), NEG_INF)
        m, l, acc = _flash_step(sn, newn[:, LANES:].astype(BF16), ms_ref[:, 0:1], ls_ref[:, 0:1], as_ref[...])
        o_s = acc / l
        newf = newf_ref[...]
        fsn = _rep_rows(fcar_ref[:, 0:1] + _lane_cumsum(newlf_ref[...]), nq)
        sf = jnp.where(okn, _nt(qf, newf[:, :LANES].astype(BF16)) + (ft - fsn), NEG_INF)
        m, l, acc = _flash_step(sf, newf[:, LANES:].astype(BF16), mf_ref[:, 0:1], lf_ref[:, 0:1], af_ref[...])
        of_ref[...] = acc / l
        wbuf = wbuf_ref[...]
        wnew = wnew_ref[...]
        nbuf = wbuf.shape[0]
        kw = jnp.concatenate([wbuf[:, :LANES], wnew[:, :LANES]], axis=0).astype(BF16)
        vw = jnp.concatenate([wbuf[:, LANES:], wnew[:, LANES:]], axis=0).astype(BF16)
        wl = lax.broadcasted_iota(jnp.int32, (1, nbuf + npad), 1)
        posw = jnp.where(wl < nbuf, past_len - nbuf + wl, past_len + wl - nbuf).astype(F32)
        dw = t - posw
        okw = (dw >= 0) & (dw < WINDOW) & ((wl < nbuf) | (wl < nbuf + nq))
        sw = jnp.where(okw, _nt(qn, kw) - slope * dw, NEG_INF)
        ew = jnp.exp(sw - jnp.max(sw, axis=-1, keepdims=True))
        o_w = _nn(ew.astype(BF16), vw) / jnp.sum(ew, axis=-1, keepdims=True)
        gate = gate_ref[...]
        on_ref[...] = gate[:, 0:1] * oc_ref[...] + gate[:, 1:2] * o_s + gate[:, 2:3] * o_w
        wst_ref[0:nbuf - nq, :] = wbuf[nq:, :]
        wst_ref[nbuf - nq:, :] = wnew[0:nq, :]


def _dec_main_call(pt_flat, need_flat, qn_dec, qf_dec, sel, rowc, o_c, gates, ftot, newn, newf, newlf, wbuf, wnew,
                   hot, nsa_t, fox_t, lf_pool, pp, n_pages, past_len, nq):
    b, nrow, _ = qn_dec.shape
    nb = past_len // CMP_BLOCK
    page = nsa_t.shape[-1]
    nc = n_pages // pp
    nbuf = wbuf.shape[1]
    assert nb % 16 == 0 and nb <= LANES, "query-side block mask lanes"
    per = lambda shape: pl.BlockSpec((None,) + shape, lambda i, c, *_: (i, 0, 0))
    cst = lambda shape: pl.BlockSpec(shape, lambda i, c, *_: (0, 0))
    rl = (nrow, LANES)
    kv_block = (None, 2, NSA_KV_HEADS, HEAD_DIM, page)
    grid_spec = pltpu.PrefetchScalarGridSpec(
        num_scalar_prefetch=2,
        grid=(b, nc),
        in_specs=[per(rl), per(rl), per((nrow, nb)), cst(rowc.shape), per(rl), per(rl), per((FOX_HEADS, LANES)),
                  per(newn.shape[1:]), per(newf.shape[1:]), per(newlf.shape[1:]),
                  per((nbuf, 2 * LANES)), per(wnew.shape[1:]),
                  pl.BlockSpec((None,) + hot.shape[1:], lambda i, c, *_: (c, 0, 0))]
        + _page_specs(pp, n_pages, kv_block, 1, gated=True)
        + _page_specs(pp, n_pages, kv_block, 0)
        + _page_specs(pp, n_pages, (None, FOX_HEADS, page), 0),
        out_specs=[per(rl), per(rl), per((nbuf, 2 * LANES))],
        scratch_shapes=[pltpu.VMEM(rl, F32)] * 6 + [pltpu.VMEM((FOX_HEADS, LANES), F32), pltpu.VMEM(rl, F32),
                                                    pltpu.VMEM((nrow, LANES + nb), BF16)],
    )
    return pl.pallas_call(
        functools.partial(_dec_main_kernel, pp=pp, nb=nb, past_len=past_len, nq=nq),
        grid_spec=grid_spec,
        out_shape=[jax.ShapeDtypeStruct((b,) + rl, F32), jax.ShapeDtypeStruct((b,) + rl, F32),
                   jax.ShapeDtypeStruct((b, nbuf, 2 * LANES), F32)],
        compiler_params=_params(("parallel", "arbitrary")),
        name="decode_main",
    )(pt_flat, need_flat, qn_dec, qf_dec, sel, rowc, o_c, gates, ftot, newn, newf, newlf, wbuf, wnew, hot,
      *([nsa_t] * pp), *([fox_t] * pp), *([lf_pool] * pp))


def _outproj_kernel(x_ref, mod_ref, g_ref, on_ref, of_ref, wm_ref, wa_ref, wb_ref, wo_ref, o_ref):
    x = x_ref[...]
    nb, nt, d = x.shape
    h = _pre(x, mod_ref, g_ref, 1).reshape(nb * nt, d).astype(BF16)
    gm = jax.nn.sigmoid(_nn(h, wm_ref[...]))
    y = gm[:, :d] * _nn(on_ref[...], wa_ref[...]) + gm[:, d:] * _nn(of_ref[...], wb_ref[...])
    mix = _nn(y.astype(BF16), wo_ref[...]).reshape(nb, nt, d)
    o_ref[...] = x + mod_ref[:, 5:6, :] * mix


def _outproj_call(x3, mod, g_norm, o_n, o_f, wm, wa, wb, wo, blk):
    bs, lt, d = x3.shape
    nb, nt = blk
    rows = nb * nt
    nj = lt // nt
    tok = pl.BlockSpec((rows, o_n.shape[1]), lambda i, j: (i * nj + j, 0))
    return pl.pallas_call(
        _outproj_kernel,
        grid=(bs // nb, nj),
        in_specs=[pl.BlockSpec((nb, nt, d), lambda i, j: (i, j, 0)),
                  pl.BlockSpec((nb, 3 * N_SUB, d), lambda i, j: (i, 0, 0)),
                  _const_spec(g_norm.shape), tok, tok,
                  _const_spec(wm.shape), _const_spec(wa.shape), _const_spec(wb.shape), _const_spec(wo.shape)],
        out_specs=pl.BlockSpec((nb, nt, d), lambda i, j: (i, j, 0)),
        out_shape=jax.ShapeDtypeStruct(x3.shape, F32),
        compiler_params=_params(("parallel", "parallel")),
        name="out_proj",
    )(x3, mod, g_norm, o_n, o_f, wm, wa, wb, wo)


def _pair_major_cols(w):
    d = w.shape[0]
    return w.reshape(d, NSA_HEADS, HEAD_DIM)[:, jnp.array(PAIR_PERM), :].reshape(d, NSA_HEADS * HEAD_DIM)


def _pair_major_rows(w):
    d = w.shape[1]
    return w.reshape(NSA_HEADS, HEAD_DIM, d)[jnp.array(PAIR_PERM)].reshape(NSA_HEADS * HEAD_DIM, d)


def _prep_weights(w_in, b_fgt, g_q_nsa, g_k_nsa, g_q_fox, g_k_fox, w_cmp):
    d = w_in.shape[0]
    hq = NSA_HEADS * HEAD_DIM
    o = 0
    w_qn = w_in[:, o:o + hq]; o += hq
    w_kvn = w_in[:, o:o + 3 * 2 * LANES]; o += 3 * 2 * LANES
    w_gt = w_in[:, o:o + 3 * NSA_HEADS]; o += 3 * NSA_HEADS
    w_qf = w_in[:, o:o + hq]; o += hq
    w_kvf = w_in[:, o:o + 2 * LANES]; o += 2 * LANES
    w_ff = w_in[:, o:o + FOX_HEADS]; o += FOX_HEADS
    w_merge = w_in[:, o:]
    k_n = [w_kvn[:, br * 256:br * 256 + LANES] for br in range(3)]
    v_n = [w_kvn[:, br * 256 + LANES:(br + 1) * 256] for br in range(3)]
    small = jnp.concatenate([w_ff, w_gt, jnp.zeros((d, LANES - FOX_HEADS - 3 * NSA_HEADS), w_in.dtype)], axis=1)
    w1 = jnp.concatenate([_pair_major_cols(w_qn), _pair_major_cols(w_qf), k_n[0], k_n[1], k_n[2], w_kvf[:, :LANES],
                          v_n[0], v_n[1], v_n[2], w_kvf[:, LANES:], small], axis=1).astype(BF16)
    scale = HEAD_DIM ** -0.5
    gains = jnp.concatenate([jnp.tile(g_q_nsa * scale, NSA_HEADS), jnp.tile(g_q_fox * scale, FOX_HEADS),
                             jnp.tile(g_k_nsa[0], 2), jnp.tile(g_k_nsa[1], 2), jnp.tile(g_k_nsa[2], 2),
                             jnp.tile(g_k_fox, 2)]).reshape(1, N_NORMED * LANES)
    bf_pad = jnp.concatenate([b_fgt, jnp.zeros((LANES - FOX_HEADS,), F32)]).reshape(1, LANES)
    lane = jnp.arange(LANES)
    seg = (lane[:, None] // HEAD_DIM == lane[None, :] // HEAD_DIM).astype(BF16)
    wc = jnp.concatenate([w_cmp[0], w_cmp[0], w_cmp[1], w_cmp[1]], axis=1)
    return w1, gains, bf_pad, seg, wc, w_merge.astype(BF16)


def _position_lanes(l):
    pos = jnp.arange(l)
    lane = jnp.arange(LANES)[None, :]
    blk = (pos // CMP_BLOCK)[:, None]
    kx = jnp.where(lane < CMP_BLOCK, (lane == blk).astype(F32),
                   jnp.where((lane == XL_ONE_A) | (lane == XL_ONE_B), 1.0,
                             jnp.where(lane == XL_HI, blk.astype(F32),
                                       jnp.where(lane == XL_LO, (pos % CMP_BLOCK)[:, None].astype(F32), 0.0))))
    n = jnp.arange(LANES)[:, None]
    cx = jnp.where((lane == XL_ONE_A) | (lane == XL_ONE_B), 1.0,
                   jnp.where(lane == XL_HI, n.astype(F32), jnp.where(lane == XL_LO, 0.5 * (CMP_BLOCK - 1), 0.0)))
    return kx.astype(BF16), cx.astype(BF16)


def _to_decode_rows(q, b, nq):
    q5 = q.reshape(b, nq, HPG, 2, HEAD_DIM)
    eye = jnp.eye(2, dtype=q.dtype)
    return jnp.einsum('btjgd,gk->bgjtkd', q5, eye).reshape(b, NSA_HEADS * nq, LANES)


def _from_decode_rows(o, b, nq):
    o6 = o.reshape(b, 2, HPG, nq, 2, HEAD_DIM)
    own = jnp.stack([o6[:, 0, :, :, 0], o6[:, 1, :, :, 1]], axis=1)
    return own.transpose(0, 3, 1, 2, 4).reshape(b * nq, NSA_HEADS * HEAD_DIM)


def _pad_rows(x, n):
    return jnp.pad(x, ((0, 0), (0, n - x.shape[1]), (0, 0)))


def kernel(x_prompt, x_sample, cache_nsa_kv, cache_fox_kv, cache_fox_logf, state_win_kv, page_table,
           c_prompt, c_sample, w_ada, b_ada, g_norm, w_ffn_up, w_ffn_down, w_in, b_fgt,
           g_q_nsa, g_k_nsa, g_q_fox, g_k_fox, w_cmp, w_br_nsa, w_br_fox, w_out):
    assert w_ada.shape[0] == 1, "single-layer trunk"
    bp, lp, d = x_prompt.shape
    bs, nq, _ = x_sample.shape
    n_pool, page = cache_nsa_kv.shape[1:3]
    n_pages = page_table.shape[1]
    past_len = n_pages * page
    nbuf = state_win_kv.shape[2]
    dff = w_ffn_down.shape[2]
    assert nq % 8 == 0 and nq & (nq - 1) == 0 and nq <= CMP_BLOCK and past_len // CMP_BLOCK >= SLC_TOPK
    assert nbuf == WINDOW and lp % 512 == 0

    mod = _mod_call(jnp.concatenate([c_prompt, c_sample], axis=0), w_ada[0], b_ada[0]).reshape(bp + bs, 3 * N_SUB, d)
    mod_p, mod_s = mod[:bp], mod[bp:]
    gn = g_norm[0]
    up = w_ffn_up[0].astype(BF16)
    ffn_w = [(up[i, :, :dff], up[i, :, dff:], w_ffn_down[0, i].astype(BF16)) for i in range(2)]
    w1, gains, bf_pad, seg, wc, w_merge = _prep_weights(
        w_in[0], b_fgt[0], g_q_nsa[0], g_k_nsa[0], g_q_fox[0], g_k_fox[0], w_cmp[0])
    wo = w_out[0].astype(BF16)
    wa_nat, wb_nat = w_br_nsa[0].astype(BF16), w_br_fox[0].astype(BF16)
    wa_pm, wb_pm = _pair_major_rows(wa_nat), _pair_major_rows(wb_nat)

    pblk = (1, 512)
    tq, tk = 256, 256
    x1 = _ffn_call(x_prompt, mod_p, gn, *ffn_w[0], 0, pblk)
    qn, qf, nsa_new, win_new, fox_new, small, kvb, cmpb, vt_slc, vt_win, vt_fox = _inproj_call(
        x1, mod_p, gn, w1, gains, bf_pad, seg, wc, pblk, True, tk)
    small3 = small.reshape(bp, lp, LANES)
    logf_p = small3[:, :, :FOX_HEADS]
    kv3 = kvb.reshape(bp, lp, 3 * LANES)
    kaug_f, qx_f = _cumsum_call(small3, kv3, 512)
    kx, cx = _position_lanes(lp)
    o_n = _nsa_prompt_call(qn.reshape(bp, lp, 512), kv3, kx, vt_slc, vt_win,
                           cmpb.reshape(bp, lp // CMP_BLOCK, 256), cx[:lp // CMP_BLOCK], small3, tq, tk)
    o_f = _fox_prompt_call(qf.reshape(bp, lp, 512), qx_f, kaug_f, vt_fox, tq, tk)
    x2 = _outproj_call(x1, mod_p, gn, o_n.reshape(bp * lp, 512), o_f.reshape(bp * lp, 512),
                       w_merge, wa_pm, wb_pm, wo, pblk)
    y_prompt = _ffn_call(x2, mod_p, gn, *ffn_w[1], 2, pblk)
    def rows_major(t, planes, rows):
        return t.reshape(bp, planes, NSA_KV_HEADS, HEAD_DIM, rows).transpose(0, 4, 1, 2, 3)[None]

    nwin = min(WINDOW, lp)
    nsa_p = rows_major(nsa_new, 4, lp)
    fox_p = rows_major(fox_new, 2, lp)
    win_p = rows_major(win_new[:, :, lp - nwin:], 2, nwin)

    sblk = (min(bs, 64), nq)
    pp = next(p for p in (16, 8, 4, 2, 1) if n_pages % p == 0)
    x1s = _ffn_call(x_sample, mod_s, gn, *ffn_w[0], 0, sblk)
    qn_s, qf_s, nsa_s, win_s, fox_s, small_s, _ = _inproj_call(
        x1s, mod_s, gn, w1, gains, bf_pad, seg, wc, sblk, False)
    nrow = NSA_HEADS * nq
    qn_dec = _to_decode_rows(qn_s, bs, nq)
    qf_dec = _to_decode_rows(qf_s, bs, nq)
    rows = jnp.arange(nrow)
    rowc = jnp.zeros((nrow, LANES), F32).at[:, 0].set(jnp.array(SLOPES, F32)[rows // nq]).at[:, 1].set(
        (rows % nq).astype(F32))
    sm3 = small_s.reshape(bs, nq, LANES)
    gates = sm3[:, :, FOX_HEADS:FOX_HEADS + 3 * NSA_HEADS].reshape(bs, nq, 3, NSA_HEADS)
    gates = jnp.pad(gates.transpose(0, 3, 1, 2).reshape(bs, nrow, 3), ((0, 0), (0, 0), (0, LANES - 3)))
    newlf = _pad_rows(sm3[:, :, :FOX_HEADS], LANES).transpose(0, 2, 1)
    newn = _pad_rows(nsa_s.reshape(bs, nq, 512)[:, :, 256:], LANES)
    newf = _pad_rows(fox_s.reshape(bs, nq, 256), LANES)
    wnew = _pad_rows(win_s.reshape(bs, nq, 256), LANES)
    nsa_t = cache_nsa_kv[0].transpose(0, 2, 3, 4, 1)
    fox_t = cache_fox_kv[0].transpose(0, 2, 3, 4, 1)
    wct = jnp.tile(w_cmp[0].transpose(0, 2, 1), (1, 1, page // CMP_BLOCK))
    kc = pp * page
    col_blk = jnp.arange(kc) // CMP_BLOCK
    bsum = (jnp.arange(kc // CMP_BLOCK)[:, None] == col_blk[None, :]).astype(BF16)
    nblocks = past_len // CMP_BLOCK
    hot = (jnp.arange(nblocks)[None, :, None]
           == (jnp.arange(n_pages // pp)[:, None, None] * (kc // CMP_BLOCK) + col_blk[None, None, :])).astype(BF16)
    lf_pool = cache_fox_logf[0].astype(F32).transpose(0, 2, 1)
    wbuf = state_win_kv[0].reshape(bs, nbuf, 256)
    pt_flat = page_table.reshape(-1).astype(jnp.int32)
    o_c, sel, ftot, chunk_need = _dec_cmp_call(pt_flat, qn_dec, rowc, wct, bsum, nsa_t, lf_pool, pp, n_pages, past_len)
    need_flat = chunk_need[:, 0, :n_pages // pp].reshape(-1)
    on_dec, of_dec, wst = _dec_main_call(pt_flat, need_flat, qn_dec, qf_dec, sel, rowc, o_c, gates, ftot, newn, newf,
                                         newlf, wbuf, wnew, hot, nsa_t, fox_t, lf_pool, pp, n_pages, past_len, nq)
    o_ns = _from_decode_rows(on_dec, bs, nq).astype(BF16)
    o_fs = _from_decode_rows(of_dec, bs, nq).astype(BF16)
    x2s = _outproj_call(x1s, mod_s, gn, o_ns, o_fs, w_merge, wa_nat, wb_nat, wo, sblk)
    y_sample = _ffn_call(x2s, mod_s, gn, *ffn_w[1], 2, sblk)
    nsa_o = nsa_s.reshape(1, bs, nq, 4, NSA_KV_HEADS, HEAD_DIM)
    fox_o = fox_s.reshape(1, bs, nq, 2, FOX_KV_HEADS, HEAD_DIM)
    win_o = wst.reshape(1, bs, nbuf, 2, NSA_KV_HEADS, HEAD_DIM)

    return (y_prompt, y_sample, nsa_p, nsa_o, fox_p, fox_o,
            logf_p[None], sm3[None, :, :, :FOX_HEADS], win_p, win_o)
```

```python
import functools

import jax
import jax.numpy as jnp
from jax import lax
from jax.experimental import pallas as pl
from jax.experimental.pallas import tpu as pltpu

HEAD_DIM = 64
NSA_HEADS = 8
NSA_KV_HEADS = 2
FOX_HEADS = 8
FOX_KV_HEADS = 2
HPG = 4
CMP_BLOCK = 64
CMP_SHIFT = 6
SLC_TOPK = 8
WINDOW = 512
N_SUB = 3
RMS_EPS = 1e-6
NEG_INF = -1e30
FORCE_BONUS = 1e4
LANES = 128
PAIR_PERM = (0, 4, 1, 5, 2, 6, 3, 7)
SLOPES = tuple(2.0 ** -(h + 1) for h in range(NSA_HEADS))
VMEM_LIMIT = 56 * 1024 * 1024

F32 = jnp.float32
BF16 = jnp.bfloat16


def _nt(a, b):
    return lax.dot_general(a, b, (((1,), (1,)), ((), ())), preferred_element_type=F32)


def _nn(a, b):
    return jnp.dot(a, b, preferred_element_type=F32)


def _split3(x):
    hi = x.astype(BF16)
    r1 = x - hi.astype(F32)
    mid = r1.astype(BF16)
    lo = (r1 - mid.astype(F32)).astype(BF16)
    return hi, mid, lo


def _params(sem):
    return pltpu.CompilerParams(dimension_semantics=sem, vmem_limit_bytes=VMEM_LIMIT)


def _const_spec(shape):
    nd = len(shape)
    return pl.BlockSpec(shape, lambda *_: (0,) * nd, pipeline_mode=pl.Buffered(1))


def _mod_kernel(c_ref, w_ref, b_ref, o_ref):
    c = c_ref[...]
    a = (c * jax.nn.sigmoid(c)).astype(BF16)
    o_ref[...] = _nn(a, w_ref[...].astype(BF16)) + b_ref[...]


def _mod_call(c_all, w_ada, b_ada):
    bc, d = c_all.shape
    n = w_ada.shape[1]
    tn = 1024 if n % 1024 == 0 else n
    return pl.pallas_call(
        _mod_kernel,
        grid=(n // tn,),
        in_specs=[pl.BlockSpec((bc, d), lambda i: (0, 0)),
                  pl.BlockSpec((d, tn), lambda i: (0, i)),
                  pl.BlockSpec((1, tn), lambda i: (0, i))],
        out_specs=pl.BlockSpec((bc, tn), lambda i: (0, i)),
        out_shape=jax.ShapeDtypeStruct((bc, n), F32),
        compiler_params=_params(("parallel",)),
        name="adaln_mod",
    )(c_all, w_ada, b_ada.reshape(1, n))


def _pre(x, mod_ref, g_ref, sub):
    ms = jnp.mean(x * x, axis=-1, keepdims=True)
    y = x * lax.rsqrt(ms + RMS_EPS) * g_ref[sub:sub + 1, :]
    return y * (1.0 + mod_ref[:, 3 * sub + 1:3 * sub + 2, :]) + mod_ref[:, 3 * sub:3 * sub + 1, :]


def _ffn_kernel(x_ref, mod_ref, g_ref, wa_ref, wb_ref, wd_ref, o_ref, act_ref, *, sub, fc):
    x = x_ref[...]
    nb, nt, d = x.shape
    h = _pre(x, mod_ref, g_ref, sub).reshape(nb * nt, d).astype(BF16)
    dff = wa_ref.shape[1]
    for c in range(dff // fc):
        a = _nn(h, wa_ref[:, c * fc:(c + 1) * fc])
        b = _nn(h, wb_ref[:, c * fc:(c + 1) * fc])
        act_ref[:, c * fc:(c + 1) * fc] = (a * jax.nn.sigmoid(a) * b).astype(BF16)
    y = _nn(act_ref[...], wd_ref[...]).reshape(nb, nt, d)
    o_ref[...] = x + 0.5 * mod_ref[:, 3 * sub + 2:3 * sub + 3, :] * y


def _ffn_call(x3, mod, g_norm, wa, wb, wd, sub, blk):
    bs, lt, d = x3.shape
    nb, nt = blk
    dff = wa.shape[1]
    fc = 256 if dff % 256 == 0 else dff
    grid = (bs // nb, lt // nt)
    return pl.pallas_call(
        functools.partial(_ffn_kernel, sub=sub, fc=fc),
        grid=grid,
        in_specs=[pl.BlockSpec((nb, nt, d), lambda i, j: (i, j, 0)),
                  pl.BlockSpec((nb, 3 * N_SUB, d), lambda i, j: (i, 0, 0)),
                  _const_spec(g_norm.shape), _const_spec(wa.shape), _const_spec(wb.shape),
                  _const_spec(wd.shape)],
        out_specs=pl.BlockSpec((nb, nt, d), lambda i, j: (i, j, 0)),
        out_shape=jax.ShapeDtypeStruct(x3.shape, F32),
        scratch_shapes=[pltpu.VMEM((nb * nt, dff), BF16)],
        compiler_params=_params(("parallel", "parallel")),
        name="ffn_sub%d" % sub,
    )(x3, mod, g_norm, wa, wb, wd)


N_NORMED = 12


def _inproj_kernel(x_ref, mod_ref, g_ref, w_ref, gain_ref, bf_ref, seg_ref, *rest, with_cmp):
    if with_cmp:
        wc_ref, qn_ref, qf_ref, nsa_ref, win_ref, fox_ref, sm_ref, kv_ref, cmp_ref, *vt_refs = rest
    else:
        qn_ref, qf_ref, nsa_ref, win_ref, fox_ref, sm_ref, kv_ref = rest
    x = x_ref[...]
    nb, nt, d = x.shape
    h = _pre(x, mod_ref, g_ref, 1).reshape(nb * nt, d).astype(BF16)
    z = _nn(h, w_ref[...])
    seg = seg_ref[...]
    normed = []
    for c in range(N_NORMED):
        zc = z[:, c * LANES:(c + 1) * LANES]
        zz = zc * zc
        hi = zz.astype(BF16)
        lo = (zz - hi.astype(F32)).astype(BF16)
        ss = _nn(hi, seg) + _nn(lo, seg)
        normed.append(zc * lax.rsqrt(ss * (1.0 / HEAD_DIM) + RMS_EPS) * gain_ref[:, c * LANES:(c + 1) * LANES])
    raw = [z[:, (N_NORMED + c) * LANES:(N_NORMED + c + 1) * LANES] for c in range(4)]
    k_cmp, k_slc, k_win, k_fox = normed[8:12]
    v_cmp, v_slc, v_win, v_fox = raw
    for j in range(4):
        qn_ref[:, j * LANES:(j + 1) * LANES] = normed[j].astype(BF16)
        qf_ref[:, j * LANES:(j + 1) * LANES] = normed[4 + j].astype(BF16)
    def put(ref, c, t, vt_ref=None):
        if not with_cmp:
            ref[:, c * LANES:(c + 1) * LANES] = t
            return
        tt = t.T
        ref[c * LANES:(c + 1) * LANES, :] = tt
        if vt_ref is not None:
            n_kt, n_g, vrows, tkv = vt_ref.shape
            tail = jnp.where(lax.broadcasted_iota(jnp.int32, (vrows - HEAD_DIM, tkv), 0) == 0, 1.0, 0.0).astype(BF16)
            for kt in range(n_kt):
                for g in range(n_g):
                    vt_ref[kt, g, :HEAD_DIM, :] = tt[g * HEAD_DIM:(g + 1) * HEAD_DIM,
                                                      kt * tkv:(kt + 1) * tkv].astype(BF16)
                    vt_ref[kt, g, HEAD_DIM:, :] = tail

    vts = vt_refs if with_cmp else (None, None, None)
    for c, t in enumerate((k_cmp, v_cmp, k_slc)):
        put(nsa_ref, c, t)
    put(nsa_ref, 3, v_slc, vts[0])
    put(win_ref, 0, k_win)
    put(win_ref, 1, v_win, vts[1])
    put(fox_ref, 0, k_fox)
    put(fox_ref, 1, v_fox, vts[2])
    for c, t in enumerate((k_slc, k_win, k_fox)):
        kv_ref[:, c * LANES:(c + 1) * LANES] = t.astype(BF16)
    zs = z[:, 16 * LANES:17 * LANES]
    lane = lax.broadcasted_iota(jnp.int32, zs.shape, 1)
    zf = zs + bf_ref[...]
    logf = jnp.minimum(zf, 0.0) - jnp.log1p(jnp.exp(-jnp.abs(zf)))
    sm_ref[...] = jnp.where(lane < FOX_HEADS, logf, jax.nn.sigmoid(zs))
    if with_cmp:
        rows = nb * nt
        kvc = jnp.concatenate([k_cmp, v_cmp], axis=1).reshape(rows // CMP_BLOCK, CMP_BLOCK, 2 * LANES)
        cmp_ref[...] = jnp.sum(kvc * wc_ref[...][None], axis=1)


def _inproj_call(x3, mod, g_norm, w1, gains, bf_pad, seg, wc, blk, with_cmp, tk=None):
    bs, lt, d = x3.shape
    nb, nt = blk
    rows = nb * nt
    t = bs * lt
    grid = (bs // nb, lt // nt)
    nj = lt // nt

    def tok(cols):
        return pl.BlockSpec((rows, cols), lambda i, j: (i * nj + j, 0))

    in_specs = [pl.BlockSpec((nb, nt, d), lambda i, j: (i, j, 0)),
                pl.BlockSpec((nb, 3 * N_SUB, d), lambda i, j: (i, 0, 0)),
                _const_spec(g_norm.shape), _const_spec(w1.shape), _const_spec(gains.shape),
                _const_spec(bf_pad.shape), _const_spec(seg.shape)]
    args = [x3, mod, g_norm, w1, gains, bf_pad, seg]
    out_specs = [tok(512), tok(512), tok(512), tok(256), tok(256), tok(128), tok(384)]
    out_shape = [jax.ShapeDtypeStruct((t, 512), BF16), jax.ShapeDtypeStruct((t, 512), BF16),
                 jax.ShapeDtypeStruct((t, 512), F32), jax.ShapeDtypeStruct((t, 256), F32),
                 jax.ShapeDtypeStruct((t, 256), F32), jax.ShapeDtypeStruct((t, 128), F32),
                 jax.ShapeDtypeStruct((t, 384), BF16)]
    if with_cmp:
        assert nb == 1
        for k, cols in ((2, 512), (3, 256), (4, 256)):
            out_specs[k] = pl.BlockSpec((None, cols, nt), lambda i, j: (i, 0, j))
            out_shape[k] = jax.ShapeDtypeStruct((bs, cols, lt), F32)
        in_specs.append(_const_spec(wc.shape))
        args.append(wc)
        out_specs.append(pl.BlockSpec((rows // CMP_BLOCK, 256), lambda i, j: (i * nj + j, 0)))
        out_shape.append(jax.ShapeDtypeStruct((t // CMP_BLOCK, 256), F32))
        for _ in range(3):
            out_specs.append(pl.BlockSpec((None, nt // tk, NSA_KV_HEADS, VT_ROWS, tk), lambda i, j: (i, j, 0, 0, 0)))
            out_shape.append(jax.ShapeDtypeStruct((bs, lt // tk, NSA_KV_HEADS, VT_ROWS, tk), BF16))
    return pl.pallas_call(
        functools.partial(_inproj_kernel, with_cmp=with_cmp),
        grid=grid, in_specs=in_specs, out_specs=out_specs, out_shape=out_shape,
        compiler_params=_params(("parallel", "parallel")),
        name="in_proj",
    )(*args)


FQ0 = 0
FK0 = 3 * FOX_HEADS


def _cumsum_kernel(tok_ref, k_ref, kaug_ref, qx_ref, carry_ref):
    @pl.when(pl.program_id(1) == 0)
    def _():
        carry_ref[...] = jnp.zeros_like(carry_ref)

    tc = tok_ref.shape[0]
    r = lax.broadcasted_iota(jnp.int32, (tc, tc), 0)
    c = lax.broadcasted_iota(jnp.int32, (tc, tc), 1)
    lower = jnp.where(r >= c, 1.0, 0.0).astype(BF16)
    cs = carry_ref[...] + sum(_nn(lower, p) for p in _split3(tok_ref[...]))
    carry_ref[...] = cs[tc - 1:tc, :]
    hrow = lax.broadcasted_iota(jnp.int32, (LANES, LANES), 0)
    col = lax.broadcasted_iota(jnp.int32, (LANES, LANES), 1)
    placed = jnp.zeros((tc, LANES), F32)
    for p, piece in enumerate(_split3(cs)):
        tgt = 3 * hrow + p
        place = jnp.where((hrow < FOX_HEADS) & ((col == FQ0 + tgt) | (col == FK0 + tgt)), 1.0, 0.0).astype(BF16)
        placed = placed + _nn(piece, place)
    lane = lax.broadcasted_iota(jnp.int32, (tc, LANES), 1)
    in_q = lane < FK0
    in_k = (lane >= FK0) & (lane < 2 * FK0)
    qx_ref[...] = jnp.where(in_q, placed, jnp.where(in_k, 1.0, 0.0)).astype(BF16)
    kaug_ref[:, :LANES] = k_ref[...]
    kaug_ref[:, LANES:] = jnp.where(in_q, 1.0, jnp.where(in_k, -placed, 0.0)).astype(BF16)


def _cumsum_call(small3, kv3, tc):
    b, l, _ = small3.shape
    return pl.pallas_call(
        _cumsum_kernel,
        grid=(b, l // tc),
        in_specs=[pl.BlockSpec((None, tc, LANES), lambda i, j: (i, j, 0)),
                  pl.BlockSpec((None, tc, LANES), lambda i, j: (i, j, 2))],
        out_specs=[pl.BlockSpec((None, tc, 2 * LANES), lambda i, j: (i, j, 0)),
                   pl.BlockSpec((None, tc, LANES), lambda i, j: (i, j, 0))],
        out_shape=[jax.ShapeDtypeStruct((b, l, 2 * LANES), BF16),
                   jax.ShapeDtypeStruct((b, l, LANES), BF16)],
        scratch_shapes=[pltpu.VMEM((1, LANES), F32)],
        compiler_params=_params(("parallel", "arbitrary")),
        name="fox_cumsum",
    )(small3, kv3)


def _flash_step(s, v, m, l, acc):
    m_new = jnp.maximum(m, jnp.max(s, axis=-1, keepdims=True))
    alpha = jnp.exp(m - m_new)
    p = jnp.exp(s - m_new)
    l = alpha * l + jnp.sum(p, axis=-1, keepdims=True)
    acc = alpha * acc + _nn(p.astype(BF16), v)
    return m_new, l, acc


def _flash_init(rows):
    return (jnp.full((rows, 1), NEG_INF, F32), jnp.zeros((rows, 1), F32), jnp.zeros((rows, LANES), F32))


def _group_masked(qp, g):
    lane = lax.broadcasted_iota(jnp.int32, qp.shape, 1)
    keep = (lane < HEAD_DIM) if g == 0 else (lane >= HEAD_DIM)
    return jnp.where(keep, qp, jnp.zeros_like(qp))


def _top_blocks(score, n_pick, axis=1):
    idx = lax.broadcasted_iota(jnp.int32, score.shape, axis).astype(F32)
    sel = jnp.zeros(score.shape, F32)
    for _ in range(n_pick):
        mx = jnp.max(score, axis=axis, keepdims=True)
        first = jnp.min(jnp.where(score == mx, idx, 1e9), axis=axis, keepdims=True)
        pick = idx == first
        sel = jnp.where(pick, 1.0, sel)
        score = jnp.where(pick, -jnp.inf, score)
    return sel


XL_ONE_A = CMP_BLOCK
XL_ONE_B = CMP_BLOCK + 1
XL_HI = CMP_BLOCK + 2
XL_LO = CMP_BLOCK + 3
MASK_BIG = 1e30
N_FREE = SLC_TOPK - 3


VT_ROWS = HEAD_DIM + 16


def _col_max(s):
    k = s.shape[0]
    if k % 64 == 0:
        s = jnp.max(s.reshape(8, k // 8, s.shape[1]), axis=0)
    return jnp.max(s, axis=0, keepdims=True)


AHEAD = 1


def _heads_scores(k, qa_ref, ok):
    out = []
    for h in range(qa_ref.shape[0]):
        s = _nt(k, qa_ref[h])
        out.append(s if ok is None else jnp.where(ok, s, NEG_INF))
    return out


def _heads_stage(scores, s_ref):
    for h, s in enumerate(scores):
        s_ref[h] = s


def _heads_consume(vt, s_ref, m_ref, acc_ref):
    nh = s_ref.shape[0]
    probs, alphas = [], []
    for h in range(nh):
        s = s_ref[h]
        m_old = m_ref[h:h + 1, :]
        m_new = jnp.maximum(m_old, _col_max(s))
        m_ref[h:h + 1, :] = m_new
        alphas.append(jnp.exp(m_old - m_new))
        probs.append(jnp.exp(s - m_new).astype(BF16))
    for h in range(nh):
        acc_ref[h] = alphas[h] * acc_ref[h] + _nn(vt[h // HPG], probs[h])


def _heads_run(k_of, vt_ref, qa_ref, s_ref, m_ref, acc_ref, lo, qi, mask_of, mask_all, need_ref=None):
    m_ref[...] = jnp.full(m_ref.shape, NEG_INF, F32)
    acc_ref[...] = jnp.zeros_like(acc_ref)
    _heads_stage(_heads_scores(k_of(qi), qa_ref, mask_of(qi)), s_ref)

    def step(kj, staged):
        k = k_of(kj)
        ok = mask_of(kj) if mask_all else None
        vt = vt_ref[staged]
        nh = qa_ref.shape[0]
        new = [_nt(k, qa_ref[h]) if h < AHEAD else None for h in range(nh)]
        for h in range(nh):
            if h + AHEAD < nh:
                new[h + AHEAD] = _nt(k, qa_ref[h + AHEAD])
            s_new = new[h]
            s = s_ref[h]
            m_old = m_ref[h:h + 1, :]
            m_new = jnp.maximum(m_old, _col_max(s))
            m_ref[h:h + 1, :] = m_new
            p = jnp.exp(s - m_new).astype(BF16)
            acc_ref[h] = jnp.exp(m_old - m_new) * acc_ref[h] + _nn(vt[h // HPG], p)
            s_ref[h] = s_new if ok is None else jnp.where(ok, s_new, NEG_INF)
        return kj

    def body(kj, staged):
        if need_ref is None:
            return step(kj, staged)
        wanted = lax.shift_right_logical(need_ref[0], kj) & 1
        return lax.cond(wanted > 0, lambda st: step(kj, st), lambda st: st, staged)

    staged = lax.fori_loop(lo, qi, body, qi)
    _heads_consume(vt_ref[staged], s_ref, m_ref, acc_ref)


def _attn_scratch(tq, tk):
    return [pltpu.VMEM((NSA_HEADS, tq, 2 * LANES), BF16), pltpu.VMEM((NSA_HEADS, tq, 2 * LANES), BF16),
            pltpu.VMEM((NSA_HEADS, tq), F32), pltpu.VMEM((NSA_HEADS, tk, tq), F32),
            pltpu.VMEM((NSA_HEADS, VT_ROWS, tq), F32), pltpu.VMEM((NSA_HEADS, HEAD_DIM, tq), F32)]


def _finish_t(acc):
    return acc[:HEAD_DIM, :] / acc[HEAD_DIM:HEAD_DIM + 1, :]


def _nsa_prompt_kernel(q_ref, ks_ref, kw_ref, kx_ref, vst_ref, vwt_ref, cmp_ref, cx_ref, sm_ref, o_ref,
                       qas_ref, qaw_ref, m_ref, s_ref, acc_ref, out_ref, need_ref, *, tq, tk, nb):
    qi = pl.program_id(1)
    lane = lax.broadcasted_iota(jnp.int32, (tq, LANES), 1)
    t2 = qi * tq + lax.broadcasted_iota(jnp.int32, (tq, LANES), 0)
    t_hi = lax.shift_right_logical(t2, CMP_SHIFT).astype(F32)
    t_lo = (t2 & (CMP_BLOCK - 1)).astype(F32)
    alibi_q = jnp.where(lane == XL_ONE_A, -CMP_BLOCK * t_hi,
                        jnp.where(lane == XL_ONE_B, -t_lo,
                                  jnp.where(lane == XL_HI, float(CMP_BLOCK), jnp.where(lane == XL_LO, 1.0, 0.0))))
    krow = lax.broadcasted_iota(jnp.int32, (tk, tq), 0)
    qcol = lax.broadcasted_iota(jnp.int32, (tk, tq), 1)

    cmpv = cmp_ref[...]
    ck_aug = jnp.concatenate([cmpv[:, :LANES].astype(BF16), cx_ref[...]], axis=1)
    cvt = jnp.concatenate([cmpv[:, LANES:], jnp.zeros((LANES - nb, LANES), F32)],
                          axis=0).T[:, :nb].astype(BF16)
    brow = lax.broadcasted_iota(jnp.int32, (nb, tq), 0)
    tcol = qi * tq + lax.broadcasted_iota(jnp.int32, (nb, tq), 1)
    valid_c = (brow + 1) * CMP_BLOCK - 1 <= tcol
    cur = lax.shift_right_logical(tcol, CMP_SHIFT)
    forced = (brow == 0) | (brow == cur) | (brow == cur - 1)
    free = (brow * CMP_BLOCK <= tcol) & jnp.logical_not(forced)
    smt = sm_ref[...].T

    qx = [SLOPES[h] * alibi_q for h in range(NSA_HEADS)]
    for g in range(NSA_KV_HEADS):
        for j in range(HPG):
            h = g * HPG + j
            qaw_ref[h, :, :LANES] = _group_masked(q_ref[:, j * LANES:(j + 1) * LANES], g)
            qaw_ref[h, :, LANES:] = qx[h].astype(BF16)
    s_c = [jnp.where(valid_c, _nt(ck_aug, qaw_ref[h]), NEG_INF) for h in range(NSA_HEADS)]
    p_c = []
    for s in s_c:
        e = jnp.exp(s - jnp.max(s, axis=0, keepdims=True))
        p_c.append(jnp.where(valid_c, e / jnp.sum(e, axis=0, keepdims=True), 0.0))
    for h in range(NSA_HEADS):
        g = h // HPG
        out_ref[h] = smt[FOX_HEADS + h:FOX_HEADS + h + 1, :] * _nn(cvt[g * HEAD_DIM:(g + 1) * HEAD_DIM, :],
                                                                 p_c[h].astype(BF16))
    imp = jnp.stack([sum(p_c[g * HPG:(g + 1) * HPG]) for g in range(NSA_KV_HEADS)])
    score = jnp.where(free[None], imp, jnp.where(forced[None], 2 * NEG_INF, NEG_INF))
    sel_all = jnp.where(forced[None], 1.0, _top_blocks(score, N_FREE, axis=1))
    picked = jnp.zeros((8, LANES), F32)
    for g in range(NSA_KV_HEADS):
        sel = jnp.concatenate([sel_all[g], jnp.zeros((LANES - nb, tq), F32)], axis=0).T
        sel_q = jnp.where(lane < CMP_BLOCK, (sel - 1.0) * MASK_BIG, 0.0)
        picked = jnp.maximum(picked, jnp.max(sel, axis=0, keepdims=True))
        for j in range(HPG):
            h = g * HPG + j
            qas_ref[h, :, :LANES] = qaw_ref[h, :, :LANES]
            qas_ref[h, :, LANES:] = (qx[h] + sel_q).astype(BF16)

    assert tq == tk, "the pipelined key loop walks one key tile per query tile"
    bpt = tk // CMP_BLOCK
    assert bpt & (bpt - 1) == 0 and nb // bpt <= 24, "tile bits must be exact in f32"
    sh = 1
    while sh < bpt:
        picked = jnp.maximum(picked, pltpu.roll(picked, sh, axis=1))
        sh *= 2
    blane = lax.broadcasted_iota(jnp.int32, (8, LANES), 1)
    weight = jnp.where(((blane & (bpt - 1)) == bpt - 1) & (blane < nb),
                       lax.shift_left(jnp.ones_like(blane), lax.shift_right_logical(blane, bpt.bit_length() - 1)),
                       0).astype(F32)
    bits = jnp.sum(picked[0:1, :] * weight[0:1, :], axis=1, keepdims=True)
    need_ref[0] = bits[0, 0].astype(jnp.int32)

    def run_branch(qa_ref, k_ref, vt_ref, lo, window, gate_row, need=None):
        def k_of(kj):
            off = pl.multiple_of(kj * tk, tk)
            return jnp.concatenate([k_ref[pl.ds(off, tk), :], kx_ref[pl.ds(off, tk), :]], axis=1)

        def mask_of(kj):
            dist = (qi - kj) * tq + qcol - krow
            return (dist >= 0) & (dist < WINDOW) if window else dist >= 0

        _heads_run(k_of, vt_ref, qa_ref, s_ref, m_ref, acc_ref, lo, qi, mask_of, window, need)
        for h in range(NSA_HEADS):
            r = gate_row + h
            out_ref[h] = out_ref[h] + smt[r:r + 1, :] * _finish_t(acc_ref[h])

    run_branch(qas_ref, ks_ref, vst_ref, 0, False, FOX_HEADS + NSA_HEADS, need_ref)
    run_branch(qaw_ref, kw_ref, vwt_ref, jnp.maximum(qi - (WINDOW + tk - 1) // tk, 0), True,
               FOX_HEADS + 2 * NSA_HEADS)
    for j in range(HPG):
        pair_t = jnp.concatenate([out_ref[j], out_ref[HPG + j]], axis=0)
        o_ref[:, j * LANES:(j + 1) * LANES] = pair_t.T.astype(BF16)


def _nsa_prompt_call(qn3, kv3, kx, vst, vwt, cmp3, cx, small3, tq, tk):
    b, l, _ = qn3.shape
    nb = cmp3.shape[1]
    assert nb <= CMP_BLOCK, "one-hot block lanes cover at most CMP_BLOCK blocks"

    def kspec(c):
        return pl.BlockSpec((None, l, LANES), lambda i, j: (i, 0, c))

    vspec = pl.BlockSpec((None, l // tk, NSA_KV_HEADS, VT_ROWS, tk), lambda i, j: (i, 0, 0, 0, 0))
    return pl.pallas_call(
        functools.partial(_nsa_prompt_kernel, tq=tq, tk=tk, nb=nb),
        grid=(b, l // tq),
        in_specs=[pl.BlockSpec((None, tq, 512), lambda i, j: (i, j, 0)),
                  kspec(0), kspec(1), pl.BlockSpec((l, LANES), lambda i, j: (0, 0)), vspec, vspec,
                  pl.BlockSpec((None, nb, 256), lambda i, j: (i, 0, 0)),
                  pl.BlockSpec((nb, LANES), lambda i, j: (0, 0)),
                  pl.BlockSpec((None, tq, LANES), lambda i, j: (i, j, 0))],
        out_specs=pl.BlockSpec((None, tq, 512), lambda i, j: (i, j, 0)),
        out_shape=jax.ShapeDtypeStruct((b, l, 512), BF16),
        scratch_shapes=_attn_scratch(tq, tk) + [pltpu.SMEM((1,), jnp.int32)],
        compiler_params=_params(("parallel", "arbitrary")),
        name="nsa_prompt",
    )(qn3, kv3, kv3, kx, vst, vwt, cmp3, cx, small3)


def _fox_prompt_kernel(q_ref, qx_ref, k_ref, vt_ref, o_ref, qa_ref, m_ref, s_ref, acc_ref, *, tq, tk):
    assert tq == tk, "the pipelined key loop walks one key tile per query tile"
    qi = pl.program_id(1)
    lane = lax.broadcasted_iota(jnp.int32, (tq, LANES), 1)
    krow = lax.broadcasted_iota(jnp.int32, (tk, tq), 0)
    qcol = lax.broadcasted_iota(jnp.int32, (tk, tq), 1)
    qx = qx_ref[...]
    for g in range(FOX_KV_HEADS):
        for j in range(HPG):
            h = g * HPG + j
            own = ((lane >= FQ0 + 3 * h) & (lane < FQ0 + 3 * h + 3)) | ((lane >= FK0 + 3 * h) & (lane < FK0 + 3 * h + 3))
            qa_ref[h, :, :LANES] = _group_masked(q_ref[:, j * LANES:(j + 1) * LANES], g)
            qa_ref[h, :, LANES:] = jnp.where(own, qx, jnp.zeros_like(qx))
    _heads_run(lambda kj: k_ref[pl.ds(pl.multiple_of(kj * tk, tk), tk), :], vt_ref, qa_ref, s_ref, m_ref, acc_ref,
               0, qi, lambda kj: (qi - kj) * tq + qcol - krow >= 0, False)
    for j in range(HPG):
        pair_t = jnp.concatenate([_finish_t(acc_ref[j]), _finish_t(acc_ref[HPG + j])], axis=0)
        o_ref[:, j * LANES:(j + 1) * LANES] = pair_t.T.astype(BF16)


def _fox_prompt_call(qf3, qx3, kaug3, vft, tq, tk):
    b, l, _ = qf3.shape
    return pl.pallas_call(
        functools.partial(_fox_prompt_kernel, tq=tq, tk=tk),
        grid=(b, l // tq),
        in_specs=[pl.BlockSpec((None, tq, 512), lambda i, j: (i, j, 0)),
                  pl.BlockSpec((None, tq, LANES), lambda i, j: (i, j, 0)),
                  pl.BlockSpec((None, l, 2 * LANES), lambda i, j: (i, 0, 0)),
                  pl.BlockSpec((None, l // tk, FOX_KV_HEADS, VT_ROWS, tk), lambda i, j: (i, 0, 0, 0, 0))],
        out_specs=pl.BlockSpec((None, tq, 512), lambda i, j: (i, j, 0)),
        out_shape=jax.ShapeDtypeStruct((b, l, 512), BF16),
        scratch_shapes=_attn_scratch(tq, tk)[1:5],
        compiler_params=_params(("parallel", "arbitrary")),
        name="fox_prompt",
    )(qf3, qx3, kaug3, vft)


def _lane_cumsum(x):
    n = x.shape[1]
    r = lax.broadcasted_iota(jnp.int32, (n, n), 0)
    c = lax.broadcasted_iota(jnp.int32, (n, n), 1)
    upper = jnp.where(r <= c, 1.0, 0.0).astype(BF16)
    return sum(_nn(p, upper) for p in _split3(x))


def _dec_cmp_kernel(pt_ref, q_ref, rowc_ref, wct_ref, bsum_ref, *rest, pp, nb, past_len):
    del pt_ref
    pages = rest[:pp]
    lfs = rest[pp:2 * pp]
    oc_ref, sel_ref, ftot_ref, cmp_scr, ftot_scr = rest[2 * pp:]
    c = pl.program_id(1)

    @pl.when(c == 0)
    def _():
        ftot_scr[...] = jnp.zeros_like(ftot_scr)

    nblk = bsum_ref.shape[0]
    bsum = bsum_ref[...]
    for plane in range(2):
        w = wct_ref[plane]
        for g in range(NSA_KV_HEADS):
            prod = jnp.concatenate([(p[plane, g] * w).astype(BF16) for p in pages], axis=1)
            blk = _nt(bsum, prod)
            cmp_scr[2 * plane + g, pl.ds(pl.multiple_of(c * nblk, nblk), nblk), :] = blk
    tot = lfs[0][...]
    for p in lfs[1:]:
        tot = tot + p[...]
    ftot_scr[...] += jnp.broadcast_to(jnp.sum(tot, axis=-1, keepdims=True), ftot_scr.shape)

    @pl.when(c == pl.num_programs(1) - 1)
    def _():
        ftot_ref[...] = ftot_scr[...]
        ckb = jnp.concatenate([cmp_scr[0], cmp_scr[1]], axis=1).astype(BF16)
        cvb = jnp.concatenate([cmp_scr[2], cmp_scr[3]], axis=1).astype(BF16)
        nrow = q_ref.shape[0]
        rowc = rowc_ref[...]
        slope = rowc[:, 0:1]
        t = past_len + rowc[:, 1:2]
        nidx = lax.broadcasted_iota(jnp.int32, (nrow, nb), 1)
        cmid = nidx.astype(F32) * CMP_BLOCK + 0.5 * (CMP_BLOCK - 1)
        valid = cmid + 0.5 * (CMP_BLOCK - 1) <= t
        s = jnp.where(valid, _nt(q_ref[...], ckb) - slope * (t - cmid), NEG_INF)
        e = jnp.exp(s - jnp.max(s, axis=-1, keepdims=True))
        p = jnp.where(valid, e / jnp.sum(e, axis=-1, keepdims=True), 0.0)
        oc_ref[...] = _nn(p.astype(BF16), cvb)
        nq = nrow // NSA_HEADS
        n8 = lax.broadcasted_iota(jnp.int32, (nq, nb), 1)
        forced = (n8 == 0) | (n8 == nb - 1)
        sel_rows = []
        for g in range(NSA_KV_HEADS):
            imp = jnp.zeros((nq, nb), F32)
            for j in range(HPG):
                r0 = (g * HPG + j) * nq
                imp = imp + p[r0:r0 + nq, :]
            sel = _top_blocks(imp + jnp.where(forced, FORCE_BONUS, 0.0), SLC_TOPK - 1)
            sel_rows += [sel] * HPG
        sel_ref[...] = ((jnp.concatenate(sel_rows, axis=0) - 1.0) * MASK_BIG).astype(BF16)


def _page_specs(pp, n_pages, block, plane_pair):
    tail = (0,) * (len(block) - 2)

    def one(i):
        return pl.BlockSpec(block, lambda b, c, pt: (pt[b * n_pages + c * pp + i], plane_pair) + tail)
    return [one(i) for i in range(pp)]


def _dec_cmp_call(pt_flat, q_dec, rowc, wct, bsum, nsa_t, lf_pool, pp, n_pages, past_len):
    b, nrow, _ = q_dec.shape
    nb = past_len // CMP_BLOCK
    page = nsa_t.shape[-1]
    nc = n_pages // pp
    assert nb <= LANES, "block mask lanes cover at most 128 cached blocks"
    per = lambda shape: pl.BlockSpec((None,) + shape, lambda i, c, pt: (i, 0, 0))
    cst = lambda shape: pl.BlockSpec(shape, lambda i, c, pt: (0,) * len(shape))
    grid_spec = pltpu.PrefetchScalarGridSpec(
        num_scalar_prefetch=1,
        grid=(b, nc),
        in_specs=[per((nrow, LANES)), cst(rowc.shape), cst(wct.shape), cst(bsum.shape)]
        + _page_specs(pp, n_pages, (None, 2, NSA_KV_HEADS, HEAD_DIM, page), 0)
        + _page_specs(pp, n_pages, (None, FOX_HEADS, page), 0),
        out_specs=[per((nrow, LANES)), per((nrow, nb)), per((FOX_HEADS, LANES))],
        scratch_shapes=[pltpu.VMEM((4, nb, HEAD_DIM), F32), pltpu.VMEM((FOX_HEADS, LANES), F32)],
    )
    return pl.pallas_call(
        functools.partial(_dec_cmp_kernel, pp=pp, nb=nb, past_len=past_len),
        grid_spec=grid_spec,
        out_shape=[jax.ShapeDtypeStruct((b, nrow, LANES), F32),
                   jax.ShapeDtypeStruct((b, nrow, nb), BF16),
                   jax.ShapeDtypeStruct((b, FOX_HEADS, LANES), F32)],
        compiler_params=_params(("parallel", "arbitrary")),
        name="decode_cmp",
    )(pt_flat, q_dec, rowc, wct, bsum, *([nsa_t] * pp), *([lf_pool] * pp))


def _rep_rows(x, n):
    return jnp.concatenate([jnp.broadcast_to(x[h:h + 1, :], (n, x.shape[1])) for h in range(x.shape[0])], axis=0)


def _scr_flash_t(scores, vts, stats):
    staged = []
    for s, (m_ref, l_ref, _) in zip(scores, stats):
        m_old = m_ref[:, 0:1]
        m = jnp.maximum(m_old, jnp.max(s, axis=-1, keepdims=True))
        alpha = jnp.exp(m_old - m)
        p = jnp.exp(s - m)
        l = alpha * l_ref[:, 0:1] + jnp.sum(p, axis=-1, keepdims=True)
        m_ref[...] = jnp.broadcast_to(m, m_ref.shape)
        l_ref[...] = jnp.broadcast_to(l, l_ref.shape)
        staged.append((alpha, p.astype(BF16)))
    for (alpha, p), vt, (_, _, acc_ref) in zip(staged, vts, stats):
        acc_ref[...] = alpha * acc_ref[...] + _nt(p, vt)


def _pages_t(pages, plane):
    return jnp.concatenate(
        [jnp.concatenate([p[plane, g] for g in range(NSA_KV_HEADS)], axis=0) for p in pages], axis=1).astype(BF16)


def _dec_main_kernel(pt_ref, qn_ref, qf_ref, sel_ref, rowc_ref, oc_ref, gate_ref, ftot_ref,
                     newn_ref, newf_ref, newlf_ref, wbuf_ref, wnew_ref, hot_ref, *rest, pp, nb, past_len, nq):
    del pt_ref
    slc = rest[:pp]
    fox = rest[pp:2 * pp]
    lfs = rest[2 * pp:3 * pp]
    (on_ref, of_ref, wst_ref, ms_ref, ls_ref, as_ref, mf_ref, lf_ref, af_ref, fcar_ref, ftc_ref,
     qna_ref) = rest[3 * pp:]
    c = pl.program_id(1)
    nrow = qn_ref.shape[0]
    page = lfs[0].shape[1]
    kc = pp * page
    rowc = rowc_ref[...]
    slope = rowc[:, 0:1]
    t = past_len + rowc[:, 1:2]
    qn = qn_ref[...]
    qf = qf_ref[...]

    @pl.when(c == 0)
    def _():
        for r in (ms_ref, mf_ref):
            r[...] = jnp.full(r.shape, NEG_INF, F32)
        for r in (ls_ref, as_ref, lf_ref, af_ref, fcar_ref):
            r[...] = jnp.zeros_like(r)
        f_new = ftot_ref[...] + _lane_cumsum(newlf_ref[...])
        rep = _rep_rows(f_new, nq)
        lane = lax.broadcasted_iota(jnp.int32, rep.shape, 1)
        tok = lax.broadcasted_iota(jnp.int32, rep.shape, 0) & (nq - 1)
        ftc_ref[...] = jnp.broadcast_to(
            jnp.sum(jnp.where(lane == tok, rep, 0.0), axis=-1, keepdims=True), ftc_ref.shape)
        qna_ref[:, :LANES] = qn
        qna_ref[:, LANES:] = sel_ref[...]

    ft = ftc_ref[:, 0:1]

    s_fox = _nn(qf, _pages_t(fox, 0))
    cs = _lane_cumsum(jnp.concatenate([p[...] for p in lfs], axis=0))
    car = fcar_ref[:, 0:1]
    fs_parts = []
    for i in range(pp):
        cs_i = cs[i * FOX_HEADS:(i + 1) * FOX_HEADS, :]
        fs_parts.append(_rep_rows(car + cs_i, nq))
        car = car + cs_i[:, page - 1:page]
    fcar_ref[...] = jnp.broadcast_to(car, fcar_ref.shape)
    s_fox = s_fox + (ft - jnp.concatenate(fs_parts, axis=1))

    kts = jnp.concatenate([_pages_t(slc, 0), hot_ref[...]], axis=0)
    pos = (c * kc + lax.broadcasted_iota(jnp.int32, (1, kc), 1)).astype(F32)
    s_slc = _nn(qna_ref[...], kts) - slope * (t - pos)
    _scr_flash_t([s_slc, s_fox], [_pages_t(slc, 1), _pages_t(fox, 1)],
                 [(ms_ref, ls_ref, as_ref), (mf_ref, lf_ref, af_ref)])

    @pl.when(c == pl.num_programs(1) - 1)
    def _():
        npad = newn_ref.shape[0]
        lane_i = lax.broadcasted_iota(jnp.int32, (1, npad), 1)
        posn = (past_len + lane_i).astype(F32)
        okn = (lane_i < nq) & (posn <= t)
        newn = newn_ref[...]
        sn = jnp.where(okn, _nt(qn, newn[:, :LANES].astype(BF16)) - slope * (t - posn), NEG_INF)
        m, l, acc = _flash_step(sn, newn[:, LANES:].astype(BF16), ms_ref[:, 0:1], ls_ref[:, 0:1], as_ref[...])
        o_s = acc / l
        newf = newf_ref[...]
        fsn = _rep_rows(fcar_ref[:, 0:1] + _lane_cumsum(newlf_ref[...]), nq)
        sf = jnp.where(okn, _nt(qf, newf[:, :LANES].astype(BF16)) + (ft - fsn), NEG_INF)
        m, l, acc = _flash_step(sf, newf[:, LANES:].astype(BF16), mf_ref[:, 0:1], lf_ref[:, 0:1], af_ref[...])
        of_ref[...] = acc / l
        wbuf = wbuf_ref[...]
        wnew = wnew_ref[...]
        nbuf = wbuf.shape[0]
        kw = jnp.concatenate([wbuf[:, :LANES], wnew[:, :LANES]], axis=0).astype(BF16)
        vw = jnp.concatenate([wbuf[:, LANES:], wnew[:, LANES:]], axis=0).astype(BF16)
        wl = lax.broadcasted_iota(jnp.int32, (1, nbuf + npad), 1)
        posw = jnp.where(wl < nbuf, past_len - nbuf + wl, past_len + wl - nbuf).astype(F32)
        dw = t - posw
        okw = (dw >= 0) & (dw < WINDOW) & ((wl < nbuf) | (wl < nbuf + nq))
        sw = jnp.where(okw, _nt(qn, kw) - slope * dw, NEG_INF)
        ew = jnp.exp(sw - jnp.max(sw, axis=-1, keepdims=True))
        o_w = _nn(ew.astype(BF16), vw) / jnp.sum(ew, axis=-1, keepdims=True)
        gate = gate_ref[...]
        on_ref[...] = gate[:, 0:1] * oc_ref[...] + gate[:, 1:2] * o_s + gate[:, 2:3] * o_w
        wst_ref[0:nbuf - nq, :] = wbuf[nq:, :]
        wst_ref[nbuf - nq:, :] = wnew[0:nq, :]


def _dec_main_call(pt_flat, qn_dec, qf_dec, sel, rowc, o_c, gates, ftot, newn, newf, newlf, wbuf, wnew,
                   hot, nsa_t, fox_t, lf_pool, pp, n_pages, past_len, nq):
    b, nrow, _ = qn_dec.shape
    nb = past_len // CMP_BLOCK
    page = nsa_t.shape[-1]
    nc = n_pages // pp
    nbuf = wbuf.shape[1]
    assert nb % 16 == 0 and nb <= LANES, "query-side block mask lanes"
    per = lambda shape: pl.BlockSpec((None,) + shape, lambda i, c, *_: (i, 0, 0))
    cst = lambda shape: pl.BlockSpec(shape, lambda i, c, *_: (0, 0))
    rl = (nrow, LANES)
    kv_block = (None, 2, NSA_KV_HEADS, HEAD_DIM, page)
    grid_spec = pltpu.PrefetchScalarGridSpec(
        num_scalar_prefetch=1,
        grid=(b, nc),
        in_specs=[per(rl), per(rl), per((nrow, nb)), cst(rowc.shape), per(rl), per(rl), per((FOX_HEADS, LANES)),
                  per(newn.shape[1:]), per(newf.shape[1:]), per(newlf.shape[1:]),
                  per((nbuf, 2 * LANES)), per(wnew.shape[1:]),
                  pl.BlockSpec((None,) + hot.shape[1:], lambda i, c, *_: (c, 0, 0))]
        + _page_specs(pp, n_pages, kv_block, 1)
        + _page_specs(pp, n_pages, kv_block, 0)
        + _page_specs(pp, n_pages, (None, FOX_HEADS, page), 0),
        out_specs=[per(rl), per(rl), per((nbuf, 2 * LANES))],
        scratch_shapes=[pltpu.VMEM(rl, F32)] * 6 + [pltpu.VMEM((FOX_HEADS, LANES), F32), pltpu.VMEM(rl, F32),
                                                    pltpu.VMEM((nrow, LANES + nb), BF16)],
    )
    return pl.pallas_call(
        functools.partial(_dec_main_kernel, pp=pp, nb=nb, past_len=past_len, nq=nq),
        grid_spec=grid_spec,
        out_shape=[jax.ShapeDtypeStruct((b,) + rl, F32), jax.ShapeDtypeStruct((b,) + rl, F32),
                   jax.ShapeDtypeStruct((b, nbuf, 2 * LANES), F32)],
        compiler_params=_params(("parallel", "arbitrary")),
        name="decode_main",
    )(pt_flat, qn_dec, qf_dec, sel, rowc, o_c, gates, ftot, newn, newf, newlf, wbuf, wnew, hot,
      *([nsa_t] * pp), *([fox_t] * pp), *([lf_pool] * pp))


def _outproj_kernel(x_ref, mod_ref, g_ref, on_ref, of_ref, wm_ref, wa_ref, wb_ref, wo_ref, o_ref):
    x = x_ref[...]
    nb, nt, d = x.shape
    h = _pre(x, mod_ref, g_ref, 1).reshape(nb * nt, d).astype(BF16)
    gm = jax.nn.sigmoid(_nn(h, wm_ref[...]))
    y = gm[:, :d] * _nn(on_ref[...], wa_ref[...]) + gm[:, d:] * _nn(of_ref[...], wb_ref[...])
    mix = _nn(y.astype(BF16), wo_ref[...]).reshape(nb, nt, d)
    o_ref[...] = x + mod_ref[:, 5:6, :] * mix


def _outproj_call(x3, mod, g_norm, o_n, o_f, wm, wa, wb, wo, blk):
    bs, lt, d = x3.shape
    nb, nt = blk
    rows = nb * nt
    nj = lt // nt
    tok = pl.BlockSpec((rows, o_n.shape[1]), lambda i, j: (i * nj + j, 0))
    return pl.pallas_call(
        _outproj_kernel,
        grid=(bs // nb, nj),
        in_specs=[pl.BlockSpec((nb, nt, d), lambda i, j: (i, j, 0)),
                  pl.BlockSpec((nb, 3 * N_SUB, d), lambda i, j: (i, 0, 0)),
                  _const_spec(g_norm.shape), tok, tok,
                  _const_spec(wm.shape), _const_spec(wa.shape), _const_spec(wb.shape), _const_spec(wo.shape)],
        out_specs=pl.BlockSpec((nb, nt, d), lambda i, j: (i, j, 0)),
        out_shape=jax.ShapeDtypeStruct(x3.shape, F32),
        compiler_params=_params(("parallel", "parallel")),
        name="out_proj",
    )(x3, mod, g_norm, o_n, o_f, wm, wa, wb, wo)


def _pair_major_cols(w):
    d = w.shape[0]
    return w.reshape(d, NSA_HEADS, HEAD_DIM)[:, jnp.array(PAIR_PERM), :].reshape(d, NSA_HEADS * HEAD_DIM)


def _pair_major_rows(w):
    d = w.shape[1]
    return w.reshape(NSA_HEADS, HEAD_DIM, d)[jnp.array(PAIR_PERM)].reshape(NSA_HEADS * HEAD_DIM, d)


def _prep_weights(w_in, b_fgt, g_q_nsa, g_k_nsa, g_q_fox, g_k_fox, w_cmp):
    d = w_in.shape[0]
    hq = NSA_HEADS * HEAD_DIM
    o = 0
    w_qn = w_in[:, o:o + hq]; o += hq
    w_kvn = w_in[:, o:o + 3 * 2 * LANES]; o += 3 * 2 * LANES
    w_gt = w_in[:, o:o + 3 * NSA_HEADS]; o += 3 * NSA_HEADS
    w_qf = w_in[:, o:o + hq]; o += hq
    w_kvf = w_in[:, o:o + 2 * LANES]; o += 2 * LANES
    w_ff = w_in[:, o:o + FOX_HEADS]; o += FOX_HEADS
    w_merge = w_in[:, o:]
    k_n = [w_kvn[:, br * 256:br * 256 + LANES] for br in range(3)]
    v_n = [w_kvn[:, br * 256 + LANES:(br + 1) * 256] for br in range(3)]
    small = jnp.concatenate([w_ff, w_gt, jnp.zeros((d, LANES - FOX_HEADS - 3 * NSA_HEADS), w_in.dtype)], axis=1)
    w1 = jnp.concatenate([_pair_major_cols(w_qn), _pair_major_cols(w_qf), k_n[0], k_n[1], k_n[2], w_kvf[:, :LANES],
                          v_n[0], v_n[1], v_n[2], w_kvf[:, LANES:], small], axis=1).astype(BF16)
    scale = HEAD_DIM ** -0.5
    gains = jnp.concatenate([jnp.tile(g_q_nsa * scale, NSA_HEADS), jnp.tile(g_q_fox * scale, FOX_HEADS),
                             jnp.tile(g_k_nsa[0], 2), jnp.tile(g_k_nsa[1], 2), jnp.tile(g_k_nsa[2], 2),
                             jnp.tile(g_k_fox, 2)]).reshape(1, N_NORMED * LANES)
    bf_pad = jnp.concatenate([b_fgt, jnp.zeros((LANES - FOX_HEADS,), F32)]).reshape(1, LANES)
    lane = jnp.arange(LANES)
    seg = (lane[:, None] // HEAD_DIM == lane[None, :] // HEAD_DIM).astype(BF16)
    wc = jnp.concatenate([w_cmp[0], w_cmp[0], w_cmp[1], w_cmp[1]], axis=1)
    return w1, gains, bf_pad, seg, wc, w_merge.astype(BF16)


def _position_lanes(l):
    pos = jnp.arange(l)
    lane = jnp.arange(LANES)[None, :]
    blk = (pos // CMP_BLOCK)[:, None]
    kx = jnp.where(lane < CMP_BLOCK, (lane == blk).astype(F32),
                   jnp.where((lane == XL_ONE_A) | (lane == XL_ONE_B), 1.0,
                             jnp.where(lane == XL_HI, blk.astype(F32),
                                       jnp.where(lane == XL_LO, (pos % CMP_BLOCK)[:, None].astype(F32), 0.0))))
    n = jnp.arange(LANES)[:, None]
    cx = jnp.where((lane == XL_ONE_A) | (lane == XL_ONE_B), 1.0,
                   jnp.where(lane == XL_HI, n.astype(F32), jnp.where(lane == XL_LO, 0.5 * (CMP_BLOCK - 1), 0.0)))
    return kx.astype(BF16), cx.astype(BF16)


def _to_decode_rows(q, b, nq):
    q5 = q.reshape(b, nq, HPG, 2, HEAD_DIM)
    eye = jnp.eye(2, dtype=q.dtype)
    return jnp.einsum('btjgd,gk->bgjtkd', q5, eye).reshape(b, NSA_HEADS * nq, LANES)


def _from_decode_rows(o, b, nq):
    o6 = o.reshape(b, 2, HPG, nq, 2, HEAD_DIM)
    own = jnp.stack([o6[:, 0, :, :, 0], o6[:, 1, :, :, 1]], axis=1)
    return own.transpose(0, 3, 1, 2, 4).reshape(b * nq, NSA_HEADS * HEAD_DIM)


def _pad_rows(x, n):
    return jnp.pad(x, ((0, 0), (0, n - x.shape[1]), (0, 0)))


def kernel(x_prompt, x_sample, cache_nsa_kv, cache_fox_kv, cache_fox_logf, state_win_kv, page_table,
           c_prompt, c_sample, w_ada, b_ada, g_norm, w_ffn_up, w_ffn_down, w_in, b_fgt,
           g_q_nsa, g_k_nsa, g_q_fox, g_k_fox, w_cmp, w_br_nsa, w_br_fox, w_out):
    assert w_ada.shape[0] == 1, "single-layer trunk"
    bp, lp, d = x_prompt.shape
    bs, nq, _ = x_sample.shape
    n_pool, page = cache_nsa_kv.shape[1:3]
    n_pages = page_table.shape[1]
    past_len = n_pages * page
    nbuf = state_win_kv.shape[2]
    dff = w_ffn_down.shape[2]
    assert nq % 8 == 0 and nq & (nq - 1) == 0 and nq <= CMP_BLOCK and past_len // CMP_BLOCK >= SLC_TOPK
    assert nbuf == WINDOW and lp % 512 == 0

    mod = _mod_call(jnp.concatenate([c_prompt, c_sample], axis=0), w_ada[0], b_ada[0]).reshape(bp + bs, 3 * N_SUB, d)
    mod_p, mod_s = mod[:bp], mod[bp:]
    gn = g_norm[0]
    up = w_ffn_up[0].astype(BF16)
    ffn_w = [(up[i, :, :dff], up[i, :, dff:], w_ffn_down[0, i].astype(BF16)) for i in range(2)]
    w1, gains, bf_pad, seg, wc, w_merge = _prep_weights(
        w_in[0], b_fgt[0], g_q_nsa[0], g_k_nsa[0], g_q_fox[0], g_k_fox[0], w_cmp[0])
    wo = w_out[0].astype(BF16)
    wa_nat, wb_nat = w_br_nsa[0].astype(BF16), w_br_fox[0].astype(BF16)
    wa_pm, wb_pm = _pair_major_rows(wa_nat), _pair_major_rows(wb_nat)

    pblk = (1, 512)
    tq, tk = 256, 256
    x1 = _ffn_call(x_prompt, mod_p, gn, *ffn_w[0], 0, pblk)
    qn, qf, nsa_new, win_new, fox_new, small, kvb, cmpb, vt_slc, vt_win, vt_fox = _inproj_call(
        x1, mod_p, gn, w1, gains, bf_pad, seg, wc, pblk, True, tk)
    small3 = small.reshape(bp, lp, LANES)
    logf_p = small3[:, :, :FOX_HEADS]
    kv3 = kvb.reshape(bp, lp, 3 * LANES)
    kaug_f, qx_f = _cumsum_call(small3, kv3, 512)
    kx, cx = _position_lanes(lp)
    o_n = _nsa_prompt_call(qn.reshape(bp, lp, 512), kv3, kx, vt_slc, vt_win,
                           cmpb.reshape(bp, lp // CMP_BLOCK, 256), cx[:lp // CMP_BLOCK], small3, tq, tk)
    o_f = _fox_prompt_call(qf.reshape(bp, lp, 512), qx_f, kaug_f, vt_fox, tq, tk)
    x2 = _outproj_call(x1, mod_p, gn, o_n.reshape(bp * lp, 512), o_f.reshape(bp * lp, 512),
                       w_merge, wa_pm, wb_pm, wo, pblk)
    y_prompt = _ffn_call(x2, mod_p, gn, *ffn_w[1], 2, pblk)
    def rows_major(t, planes, rows):
        return t.reshape(bp, planes, NSA_KV_HEADS, HEAD_DIM, rows).transpose(0, 4, 1, 2, 3)[None]

    nwin = min(WINDOW, lp)
    nsa_p = rows_major(nsa_new, 4, lp)
    fox_p = rows_major(fox_new, 2, lp)
    win_p = rows_major(win_new[:, :, lp - nwin:], 2, nwin)

    sblk = (min(bs, 64), nq)
    pp = next(p for p in (16, 8, 4, 2, 1) if n_pages % p == 0)
    x1s = _ffn_call(x_sample, mod_s, gn, *ffn_w[0], 0, sblk)
    qn_s, qf_s, nsa_s, win_s, fox_s, small_s, _ = _inproj_call(
        x1s, mod_s, gn, w1, gains, bf_pad, seg, wc, sblk, False)
    nrow = NSA_HEADS * nq
    qn_dec = _to_decode_rows(qn_s, bs, nq)
    qf_dec = _to_decode_rows(qf_s, bs, nq)
    rows = jnp.arange(nrow)
    rowc = jnp.zeros((nrow, LANES), F32).at[:, 0].set(jnp.array(SLOPES, F32)[rows // nq]).at[:, 1].set(
        (rows % nq).astype(F32))
    sm3 = small_s.reshape(bs, nq, LANES)
    gates = sm3[:, :, FOX_HEADS:FOX_HEADS + 3 * NSA_HEADS].reshape(bs, nq, 3, NSA_HEADS)
    gates = jnp.pad(gates.transpose(0, 3, 1, 2).reshape(bs, nrow, 3), ((0, 0), (0, 0), (0, LANES - 3)))
    newlf = _pad_rows(sm3[:, :, :FOX_HEADS], LANES).transpose(0, 2, 1)
    newn = _pad_rows(nsa_s.reshape(bs, nq, 512)[:, :, 256:], LANES)
    newf = _pad_rows(fox_s.reshape(bs, nq, 256), LANES)
    wnew = _pad_rows(win_s.reshape(bs, nq, 256), LANES)
    nsa_t = cache_nsa_kv[0].transpose(0, 2, 3, 4, 1)
    fox_t = cache_fox_kv[0].transpose(0, 2, 3, 4, 1)
    wct = jnp.tile(w_cmp[0].transpose(0, 2, 1), (1, 1, page // CMP_BLOCK))
    kc = pp * page
    col_blk = jnp.arange(kc) // CMP_BLOCK
    bsum = (jnp.arange(kc // CMP_BLOCK)[:, None] == col_blk[None, :]).astype(BF16)
    nblocks = past_len // CMP_BLOCK
    hot = (jnp.arange(nblocks)[None, :, None]
           == (jnp.arange(n_pages // pp)[:, None, None] * (kc // CMP_BLOCK) + col_blk[None, None, :])).astype(BF16)
    lf_pool = cache_fox_logf[0].astype(F32).transpose(0, 2, 1)
    wbuf = state_win_kv[0].reshape(bs, nbuf, 256)
    pt_flat = page_table.reshape(-1).astype(jnp.int32)
    o_c, sel, ftot = _dec_cmp_call(pt_flat, qn_dec, rowc, wct, bsum, nsa_t, lf_pool, pp, n_pages, past_len)
    on_dec, of_dec, wst = _dec_main_call(pt_flat, qn_dec, qf_dec, sel, rowc, o_c, gates, ftot, newn, newf,
                                         newlf, wbuf, wnew, hot, nsa_t, fox_t, lf_pool, pp, n_pages, past_len, nq)
    o_ns = _from_decode_rows(on_dec, bs, nq).astype(BF16)
    o_fs = _from_decode_rows(of_dec, bs, nq).astype(BF16)
    x2s = _outproj_call(x1s, mod_s, gn, o_ns, o_fs, w_merge, wa_nat, wb_nat, wo, sblk)
    y_sample = _ffn_call(x2s, mod_s, gn, *ffn_w[1], 2, sblk)
    nsa_o = nsa_s.reshape(1, bs, nq, 4, NSA_KV_HEADS, HEAD_DIM)
    fox_o = fox_s.reshape(1, bs, nq, 2, FOX_KV_HEADS, HEAD_DIM)
    win_o = wst.reshape(1, bs, nbuf, 2, NSA_KV_HEADS, HEAD_DIM)

    return (y_prompt, y_sample, nsa_p, nsa_o, fox_p, fox_o,
            logf_p[None], sm3[None, :, :, :FOX_HEADS], win_p, win_o)
```

```python
import functools

import jax
import jax.numpy as jnp
from jax import lax
from jax.experimental import pallas as pl
from jax.experimental.pallas import tpu as pltpu

HEAD_DIM = 64
NSA_HEADS = 8
NSA_KV_HEADS = 2
FOX_HEADS = 8
FOX_KV_HEADS = 2
HPG = 4
CMP_BLOCK = 64
CMP_SHIFT = 6
SLC_TOPK = 8
WINDOW = 512
N_SUB = 3
RMS_EPS = 1e-6
NEG_INF = -1e30
FORCE_BONUS = 1e4
LANES = 128
PAIR_PERM = (0, 4, 1, 5, 2, 6, 3, 7)
SLOPES = tuple(2.0 ** -(h + 1) for h in range(NSA_HEADS))
VMEM_LIMIT = 56 * 1024 * 1024

F32 = jnp.float32
BF16 = jnp.bfloat16


def _nt(a, b):
    return lax.dot_general(a, b, (((1,), (1,)), ((), ())), preferred_element_type=F32)


def _nn(a, b):
    return jnp.dot(a, b, preferred_element_type=F32)


def _split3(x):
    hi = x.astype(BF16)
    r1 = x - hi.astype(F32)
    mid = r1.astype(BF16)
    lo = (r1 - mid.astype(F32)).astype(BF16)
    return hi, mid, lo


def _params(sem):
    return pltpu.CompilerParams(dimension_semantics=sem, vmem_limit_bytes=VMEM_LIMIT)


def _const_spec(shape):
    nd = len(shape)
    return pl.BlockSpec(shape, lambda *_: (0,) * nd, pipeline_mode=pl.Buffered(1))


def _mod_kernel(c_ref, w_ref, b_ref, o_ref):
    c = c_ref[...]
    a = (c * jax.nn.sigmoid(c)).astype(BF16)
    o_ref[...] = _nn(a, w_ref[...].astype(BF16)) + b_ref[...]


def _mod_call(c_all, w_ada, b_ada):
    bc, d = c_all.shape
    n = w_ada.shape[1]
    tn = 1024 if n % 1024 == 0 else n
    return pl.pallas_call(
        _mod_kernel,
        grid=(n // tn,),
        in_specs=[pl.BlockSpec((bc, d), lambda i: (0, 0)),
                  pl.BlockSpec((d, tn), lambda i: (0, i)),
                  pl.BlockSpec((1, tn), lambda i: (0, i))],
        out_specs=pl.BlockSpec((bc, tn), lambda i: (0, i)),
        out_shape=jax.ShapeDtypeStruct((bc, n), F32),
        compiler_params=_params(("parallel",)),
        name="adaln_mod",
    )(c_all, w_ada, b_ada.reshape(1, n))


def _pre(x, mod_ref, g_ref, sub):
    ms = jnp.mean(x * x, axis=-1, keepdims=True)
    y = x * lax.rsqrt(ms + RMS_EPS) * g_ref[sub:sub + 1, :]
    return y * (1.0 + mod_ref[:, 3 * sub + 1:3 * sub + 2, :]) + mod_ref[:, 3 * sub:3 * sub + 1, :]


def _ffn_kernel(x_ref, mod_ref, g_ref, wa_ref, wb_ref, wd_ref, o_ref, act_ref, *, sub, fc):
    x = x_ref[...]
    nb, nt, d = x.shape
    h = _pre(x, mod_ref, g_ref, sub).reshape(nb * nt, d).astype(BF16)
    dff = wa_ref.shape[1]
    for c in range(dff // fc):
        a = _nn(h, wa_ref[:, c * fc:(c + 1) * fc])
        b = _nn(h, wb_ref[:, c * fc:(c + 1) * fc])
        act_ref[:, c * fc:(c + 1) * fc] = (a * jax.nn.sigmoid(a) * b).astype(BF16)
    y = _nn(act_ref[...], wd_ref[...]).reshape(nb, nt, d)
    o_ref[...] = x + 0.5 * mod_ref[:, 3 * sub + 2:3 * sub + 3, :] * y


def _ffn_call(x3, mod, g_norm, wa, wb, wd, sub, blk):
    bs, lt, d = x3.shape
    nb, nt = blk
    dff = wa.shape[1]
    fc = 256 if dff % 256 == 0 else dff
    grid = (bs // nb, lt // nt)
    return pl.pallas_call(
        functools.partial(_ffn_kernel, sub=sub, fc=fc),
        grid=grid,
        in_specs=[pl.BlockSpec((nb, nt, d), lambda i, j: (i, j, 0)),
                  pl.BlockSpec((nb, 3 * N_SUB, d), lambda i, j: (i, 0, 0)),
                  _const_spec(g_norm.shape), _const_spec(wa.shape), _const_spec(wb.shape),
                  _const_spec(wd.shape)],
        out_specs=pl.BlockSpec((nb, nt, d), lambda i, j: (i, j, 0)),
        out_shape=jax.ShapeDtypeStruct(x3.shape, F32),
        scratch_shapes=[pltpu.VMEM((nb * nt, dff), BF16)],
        compiler_params=_params(("parallel", "parallel")),
        name="ffn_sub%d" % sub,
    )(x3, mod, g_norm, wa, wb, wd)


N_NORMED = 12


def _inproj_kernel(x_ref, mod_ref, g_ref, w_ref, gain_ref, bf_ref, seg_ref, *rest, with_cmp):
    if with_cmp:
        wc_ref, qn_ref, qf_ref, nsa_ref, win_ref, fox_ref, sm_ref, kv_ref, cmp_ref, *vt_refs = rest
    else:
        qn_ref, qf_ref, nsa_ref, win_ref, fox_ref, sm_ref, kv_ref = rest
    x = x_ref[...]
    nb, nt, d = x.shape
    h = _pre(x, mod_ref, g_ref, 1).reshape(nb * nt, d).astype(BF16)
    z = _nn(h, w_ref[...])
    seg = seg_ref[...]
    normed = []
    for c in range(N_NORMED):
        zc = z[:, c * LANES:(c + 1) * LANES]
        zz = zc * zc
        hi = zz.astype(BF16)
        lo = (zz - hi.astype(F32)).astype(BF16)
        ss = _nn(hi, seg) + _nn(lo, seg)
        normed.append(zc * lax.rsqrt(ss * (1.0 / HEAD_DIM) + RMS_EPS) * gain_ref[:, c * LANES:(c + 1) * LANES])
    raw = [z[:, (N_NORMED + c) * LANES:(N_NORMED + c + 1) * LANES] for c in range(4)]
    k_cmp, k_slc, k_win, k_fox = normed[8:12]
    v_cmp, v_slc, v_win, v_fox = raw
    for j in range(4):
        qn_ref[:, j * LANES:(j + 1) * LANES] = normed[j].astype(BF16)
        qf_ref[:, j * LANES:(j + 1) * LANES] = normed[4 + j].astype(BF16)
    def put(ref, c, t, vt_ref=None):
        if not with_cmp:
            ref[:, c * LANES:(c + 1) * LANES] = t
            return
        tt = t.T
        ref[c * LANES:(c + 1) * LANES, :] = tt
        if vt_ref is not None:
            n_kt, n_g, vrows, tkv = vt_ref.shape
            tail = jnp.where(lax.broadcasted_iota(jnp.int32, (vrows - HEAD_DIM, tkv), 0) == 0, 1.0, 0.0).astype(BF16)
            for kt in range(n_kt):
                for g in range(n_g):
                    vt_ref[kt, g, :HEAD_DIM, :] = tt[g * HEAD_DIM:(g + 1) * HEAD_DIM,
                                                      kt * tkv:(kt + 1) * tkv].astype(BF16)
                    vt_ref[kt, g, HEAD_DIM:, :] = tail

    vts = vt_refs if with_cmp else (None, None, None)
    for c, t in enumerate((k_cmp, v_cmp, k_slc)):
        put(nsa_ref, c, t)
    put(nsa_ref, 3, v_slc, vts[0])
    put(win_ref, 0, k_win)
    put(win_ref, 1, v_win, vts[1])
    put(fox_ref, 0, k_fox)
    put(fox_ref, 1, v_fox, vts[2])
    for c, t in enumerate((k_slc, k_win, k_fox)):
        kv_ref[:, c * LANES:(c + 1) * LANES] = t.astype(BF16)
    zs = z[:, 16 * LANES:17 * LANES]
    lane = lax.broadcasted_iota(jnp.int32, zs.shape, 1)
    zf = zs + bf_ref[...]
    logf = jnp.minimum(zf, 0.0) - jnp.log1p(jnp.exp(-jnp.abs(zf)))
    sm_ref[...] = jnp.where(lane < FOX_HEADS, logf, jax.nn.sigmoid(zs))
    if with_cmp:
        rows = nb * nt
        kvc = jnp.concatenate([k_cmp, v_cmp], axis=1).reshape(rows // CMP_BLOCK, CMP_BLOCK, 2 * LANES)
        cmp_ref[...] = jnp.sum(kvc * wc_ref[...][None], axis=1)


def _inproj_call(x3, mod, g_norm, w1, gains, bf_pad, seg, wc, blk, with_cmp, tk=None):
    bs, lt, d = x3.shape
    nb, nt = blk
    rows = nb * nt
    t = bs * lt
    grid = (bs // nb, lt // nt)
    nj = lt // nt

    def tok(cols):
        return pl.BlockSpec((rows, cols), lambda i, j: (i * nj + j, 0))

    in_specs = [pl.BlockSpec((nb, nt, d), lambda i, j: (i, j, 0)),
                pl.BlockSpec((nb, 3 * N_SUB, d), lambda i, j: (i, 0, 0)),
                _const_spec(g_norm.shape), _const_spec(w1.shape), _const_spec(gains.shape),
                _const_spec(bf_pad.shape), _const_spec(seg.shape)]
    args = [x3, mod, g_norm, w1, gains, bf_pad, seg]
    out_specs = [tok(512), tok(512), tok(512), tok(256), tok(256), tok(128), tok(384)]
    out_shape = [jax.ShapeDtypeStruct((t, 512), BF16), jax.ShapeDtypeStruct((t, 512), BF16),
                 jax.ShapeDtypeStruct((t, 512), F32), jax.ShapeDtypeStruct((t, 256), F32),
                 jax.ShapeDtypeStruct((t, 256), F32), jax.ShapeDtypeStruct((t, 128), F32),
                 jax.ShapeDtypeStruct((t, 384), BF16)]
    if with_cmp:
        assert nb == 1
        for k, cols in ((2, 512), (3, 256), (4, 256)):
            out_specs[k] = pl.BlockSpec((None, cols, nt), lambda i, j: (i, 0, j))
            out_shape[k] = jax.ShapeDtypeStruct((bs, cols, lt), F32)
        in_specs.append(_const_spec(wc.shape))
        args.append(wc)
        out_specs.append(pl.BlockSpec((rows // CMP_BLOCK, 256), lambda i, j: (i * nj + j, 0)))
        out_shape.append(jax.ShapeDtypeStruct((t // CMP_BLOCK, 256), F32))
        for _ in range(3):
            out_specs.append(pl.BlockSpec((None, nt // tk, NSA_KV_HEADS, VT_ROWS, tk), lambda i, j: (i, j, 0, 0, 0)))
            out_shape.append(jax.ShapeDtypeStruct((bs, lt // tk, NSA_KV_HEADS, VT_ROWS, tk), BF16))
    return pl.pallas_call(
        functools.partial(_inproj_kernel, with_cmp=with_cmp),
        grid=grid, in_specs=in_specs, out_specs=out_specs, out_shape=out_shape,
        compiler_params=_params(("parallel", "parallel")),
        name="in_proj",
    )(*args)


FQ0 = 0
FK0 = 3 * FOX_HEADS


def _cumsum_kernel(tok_ref, k_ref, kaug_ref, qx_ref, carry_ref):
    @pl.when(pl.program_id(1) == 0)
    def _():
        carry_ref[...] = jnp.zeros_like(carry_ref)

    tc = tok_ref.shape[0]
    r = lax.broadcasted_iota(jnp.int32, (tc, tc), 0)
    c = lax.broadcasted_iota(jnp.int32, (tc, tc), 1)
    lower = jnp.where(r >= c, 1.0, 0.0).astype(BF16)
    cs = carry_ref[...] + sum(_nn(lower, p) for p in _split3(tok_ref[...]))
    carry_ref[...] = cs[tc - 1:tc, :]
    hrow = lax.broadcasted_iota(jnp.int32, (LANES, LANES), 0)
    col = lax.broadcasted_iota(jnp.int32, (LANES, LANES), 1)
    placed = jnp.zeros((tc, LANES), F32)
    for p, piece in enumerate(_split3(cs)):
        tgt = 3 * hrow + p
        place = jnp.where((hrow < FOX_HEADS) & ((col == FQ0 + tgt) | (col == FK0 + tgt)), 1.0, 0.0).astype(BF16)
        placed = placed + _nn(piece, place)
    lane = lax.broadcasted_iota(jnp.int32, (tc, LANES), 1)
    in_q = lane < FK0
    in_k = (lane >= FK0) & (lane < 2 * FK0)
    qx_ref[...] = jnp.where(in_q, placed, jnp.where(in_k, 1.0, 0.0)).astype(BF16)
    kaug_ref[:, :LANES] = k_ref[...]
    kaug_ref[:, LANES:] = jnp.where(in_q, 1.0, jnp.where(in_k, -placed, 0.0)).astype(BF16)


def _cumsum_call(small3, kv3, tc):
    b, l, _ = small3.shape
    return pl.pallas_call(
        _cumsum_kernel,
        grid=(b, l // tc),
        in_specs=[pl.BlockSpec((None, tc, LANES), lambda i, j: (i, j, 0)),
                  pl.BlockSpec((None, tc, LANES), lambda i, j: (i, j, 2))],
        out_specs=[pl.BlockSpec((None, tc, 2 * LANES), lambda i, j: (i, j, 0)),
                   pl.BlockSpec((None, tc, LANES), lambda i, j: (i, j, 0))],
        out_shape=[jax.ShapeDtypeStruct((b, l, 2 * LANES), BF16),
                   jax.ShapeDtypeStruct((b, l, LANES), BF16)],
        scratch_shapes=[pltpu.VMEM((1, LANES), F32)],
        compiler_params=_params(("parallel", "arbitrary")),
        name="fox_cumsum",
    )(small3, kv3)


def _flash_step(s, v, m, l, acc):
    m_new = jnp.maximum(m, jnp.max(s, axis=-1, keepdims=True))
    alpha = jnp.exp(m - m_new)
    p = jnp.exp(s - m_new)
    l = alpha * l + jnp.sum(p, axis=-1, keepdims=True)
    acc = alpha * acc + _nn(p.astype(BF16), v)
    return m_new, l, acc


def _flash_init(rows):
    return (jnp.full((rows, 1), NEG_INF, F32), jnp.zeros((rows, 1), F32), jnp.zeros((rows, LANES), F32))


def _group_masked(qp, g):
    lane = lax.broadcasted_iota(jnp.int32, qp.shape, 1)
    keep = (lane < HEAD_DIM) if g == 0 else (lane >= HEAD_DIM)
    return jnp.where(keep, qp, jnp.zeros_like(qp))


def _top_blocks(score, n_pick, axis=1):
    idx = lax.broadcasted_iota(jnp.int32, score.shape, axis).astype(F32)
    sel = jnp.zeros(score.shape, F32)
    for _ in range(n_pick):
        mx = jnp.max(score, axis=axis, keepdims=True)
        first = jnp.min(jnp.where(score == mx, idx, 1e9), axis=axis, keepdims=True)
        pick = idx == first
        sel = jnp.where(pick, 1.0, sel)
        score = jnp.where(pick, -jnp.inf, score)
    return sel


XL_ONE_A = CMP_BLOCK
XL_ONE_B = CMP_BLOCK + 1
XL_HI = CMP_BLOCK + 2
XL_LO = CMP_BLOCK + 3
MASK_BIG = 1e30
N_FREE = SLC_TOPK - 3


VT_ROWS = HEAD_DIM + 16


def _col_max(s):
    k = s.shape[0]
    if k % 64 == 0:
        s = jnp.max(s.reshape(8, k // 8, s.shape[1]), axis=0)
    return jnp.max(s, axis=0, keepdims=True)


AHEAD = 1


def _heads_scores(k, qa_ref, ok):
    out = []
    for h in range(qa_ref.shape[0]):
        s = _nt(k, qa_ref[h])
        out.append(s if ok is None else jnp.where(ok, s, NEG_INF))
    return out


def _heads_stage(scores, s_ref):
    for h, s in enumerate(scores):
        s_ref[h] = s


def _heads_consume(vt, s_ref, m_ref, acc_ref):
    nh = s_ref.shape[0]
    probs, alphas = [], []
    for h in range(nh):
        s = s_ref[h]
        m_old = m_ref[h:h + 1, :]
        m_new = jnp.maximum(m_old, _col_max(s))
        m_ref[h:h + 1, :] = m_new
        alphas.append(jnp.exp(m_old - m_new))
        probs.append(jnp.exp(s - m_new).astype(BF16))
    for h in range(nh):
        acc_ref[h] = alphas[h] * acc_ref[h] + _nn(vt[h // HPG], probs[h])


def _heads_run(k_of, vt_ref, qa_ref, s_ref, m_ref, acc_ref, lo, qi, mask_of, mask_all, need_ref=None):
    m_ref[...] = jnp.full(m_ref.shape, NEG_INF, F32)
    acc_ref[...] = jnp.zeros_like(acc_ref)
    _heads_stage(_heads_scores(k_of(qi), qa_ref, mask_of(qi)), s_ref)

    def step(kj, staged):
        k = k_of(kj)
        ok = mask_of(kj) if mask_all else None
        vt = vt_ref[staged]
        nh = qa_ref.shape[0]
        new = [_nt(k, qa_ref[h]) if h < AHEAD else None for h in range(nh)]
        for h in range(nh):
            if h + AHEAD < nh:
                new[h + AHEAD] = _nt(k, qa_ref[h + AHEAD])
            s_new = new[h]
            s = s_ref[h]
            m_old = m_ref[h:h + 1, :]
            m_new = jnp.maximum(m_old, _col_max(s))
            m_ref[h:h + 1, :] = m_new
            p = jnp.exp(s - m_new).astype(BF16)
            acc_ref[h] = jnp.exp(m_old - m_new) * acc_ref[h] + _nn(vt[h // HPG], p)
            s_ref[h] = s_new if ok is None else jnp.where(ok, s_new, NEG_INF)
        return kj

    def body(kj, staged):
        if need_ref is None:
            return step(kj, staged)
        wanted = lax.shift_right_logical(need_ref[0], kj) & 1
        return lax.cond(wanted > 0, lambda st: step(kj, st), lambda st: st, staged)

    staged = lax.fori_loop(lo, qi, body, qi)
    _heads_consume(vt_ref[staged], s_ref, m_ref, acc_ref)


def _attn_scratch(tq, tk):
    return [pltpu.VMEM((NSA_HEADS, tq, 2 * LANES), BF16), pltpu.VMEM((NSA_HEADS, tq, 2 * LANES), BF16),
            pltpu.VMEM((NSA_HEADS, tq), F32), pltpu.VMEM((NSA_HEADS, tk, tq), F32),
            pltpu.VMEM((NSA_HEADS, VT_ROWS, tq), F32), pltpu.VMEM((NSA_HEADS, HEAD_DIM, tq), F32)]


def _finish_t(acc):
    return acc[:HEAD_DIM, :] / acc[HEAD_DIM:HEAD_DIM + 1, :]


def _nsa_prompt_kernel(q_ref, ks_ref, kw_ref, kx_ref, vst_ref, vwt_ref, cmp_ref, cx_ref, sm_ref, o_ref,
                       qas_ref, qaw_ref, m_ref, s_ref, acc_ref, out_ref, need_ref, *, tq, tk, nb):
    qi = pl.program_id(1)
    lane = lax.broadcasted_iota(jnp.int32, (tq, LANES), 1)
    t2 = qi * tq + lax.broadcasted_iota(jnp.int32, (tq, LANES), 0)
    t_hi = lax.shift_right_logical(t2, CMP_SHIFT).astype(F32)
    t_lo = (t2 & (CMP_BLOCK - 1)).astype(F32)
    alibi_q = jnp.where(lane == XL_ONE_A, -CMP_BLOCK * t_hi,
                        jnp.where(lane == XL_ONE_B, -t_lo,
                                  jnp.where(lane == XL_HI, float(CMP_BLOCK), jnp.where(lane == XL_LO, 1.0, 0.0))))
    krow = lax.broadcasted_iota(jnp.int32, (tk, tq), 0)
    qcol = lax.broadcasted_iota(jnp.int32, (tk, tq), 1)

    cmpv = cmp_ref[...]
    ck_aug = jnp.concatenate([cmpv[:, :LANES].astype(BF16), cx_ref[...]], axis=1)
    cvt = jnp.concatenate([cmpv[:, LANES:], jnp.zeros((LANES - nb, LANES), F32)],
                          axis=0).T[:, :nb].astype(BF16)
    brow = lax.broadcasted_iota(jnp.int32, (nb, tq), 0)
    tcol = qi * tq + lax.broadcasted_iota(jnp.int32, (nb, tq), 1)
    valid_c = (brow + 1) * CMP_BLOCK - 1 <= tcol
    cur = lax.shift_right_logical(tcol, CMP_SHIFT)
    forced = (brow == 0) | (brow == cur) | (brow == cur - 1)
    free = (brow * CMP_BLOCK <= tcol) & jnp.logical_not(forced)
    smt = sm_ref[...].T

    qx = [SLOPES[h] * alibi_q for h in range(NSA_HEADS)]
    for g in range(NSA_KV_HEADS):
        for j in range(HPG):
            h = g * HPG + j
            qaw_ref[h, :, :LANES] = _group_masked(q_ref[:, j * LANES:(j + 1) * LANES], g)
            qaw_ref[h, :, LANES:] = qx[h].astype(BF16)
    s_c = [jnp.where(valid_c, _nt(ck_aug, qaw_ref[h]), NEG_INF) for h in range(NSA_HEADS)]
    p_c = []
    for s in s_c:
        e = jnp.exp(s - jnp.max(s, axis=0, keepdims=True))
        p_c.append(jnp.where(valid_c, e / jnp.sum(e, axis=0, keepdims=True), 0.0))
    for h in range(NSA_HEADS):
        g = h // HPG
        out_ref[h] = smt[FOX_HEADS + h:FOX_HEADS + h + 1, :] * _nn(cvt[g * HEAD_DIM:(g + 1) * HEAD_DIM, :],
                                                                 p_c[h].astype(BF16))
    imp = jnp.stack([sum(p_c[g * HPG:(g + 1) * HPG]) for g in range(NSA_KV_HEADS)])
    score = jnp.where(free[None], imp, jnp.where(forced[None], 2 * NEG_INF, NEG_INF))
    sel_all = jnp.where(forced[None], 1.0, _top_blocks(score, N_FREE, axis=1))
    picked = jnp.zeros((8, LANES), F32)
    for g in range(NSA_KV_HEADS):
        sel = jnp.concatenate([sel_all[g], jnp.zeros((LANES - nb, tq), F32)], axis=0).T
        sel_q = jnp.where(lane < CMP_BLOCK, (sel - 1.0) * MASK_BIG, 0.0)
        picked = jnp.maximum(picked, jnp.max(sel, axis=0, keepdims=True))
        for j in range(HPG):
            h = g * HPG + j
            qas_ref[h, :, :LANES] = qaw_ref[h, :, :LANES]
            qas_ref[h, :, LANES:] = (qx[h] + sel_q).astype(BF16)

    assert tq == tk, "the pipelined key loop walks one key tile per query tile"
    bpt = tk // CMP_BLOCK
    assert bpt & (bpt - 1) == 0 and nb // bpt <= 24, "tile bits must be exact in f32"
    sh = 1
    while sh < bpt:
        picked = jnp.maximum(picked, pltpu.roll(picked, sh, axis=1))
        sh *= 2
    blane = lax.broadcasted_iota(jnp.int32, (8, LANES), 1)
    weight = jnp.where(((blane & (bpt - 1)) == bpt - 1) & (blane < nb),
                       lax.shift_left(jnp.ones_like(blane), lax.shift_right_logical(blane, bpt.bit_length() - 1)),
                       0).astype(F32)
    bits = jnp.sum(picked[0:1, :] * weight[0:1, :], axis=1, keepdims=True)
    need_ref[0] = bits[0, 0].astype(jnp.int32)

    def run_branch(qa_ref, k_ref, vt_ref, lo, window, gate_row, need=None):
        def k_of(kj):
            off = pl.multiple_of(kj * tk, tk)
            return jnp.concatenate([k_ref[pl.ds(off, tk), :], kx_ref[pl.ds(off, tk), :]], axis=1)

        def mask_of(kj):
            dist = (qi - kj) * tq + qcol - krow
            return (dist >= 0) & (dist < WINDOW) if window else dist >= 0

        _heads_run(k_of, vt_ref, qa_ref, s_ref, m_ref, acc_ref, lo, qi, mask_of, window, need)
        for h in range(NSA_HEADS):
            r = gate_row + h
            out_ref[h] = out_ref[h] + smt[r:r + 1, :] * _finish_t(acc_ref[h])

    run_branch(qas_ref, ks_ref, vst_ref, 0, False, FOX_HEADS + NSA_HEADS, need_ref)
    run_branch(qaw_ref, kw_ref, vwt_ref, jnp.maximum(qi - (WINDOW + tk - 1) // tk, 0), True,
               FOX_HEADS + 2 * NSA_HEADS)
    for j in range(HPG):
        pair_t = jnp.concatenate([out_ref[j], out_ref[HPG + j]], axis=0)
        o_ref[:, j * LANES:(j + 1) * LANES] = pair_t.T.astype(BF16)


def _nsa_prompt_call(qn3, kv3, kx, vst, vwt, cmp3, cx, small3, tq, tk):
    b, l, _ = qn3.shape
    nb = cmp3.shape[1]
    assert nb <= CMP_BLOCK, "one-hot block lanes cover at most CMP_BLOCK blocks"

    def kspec(c):
        return pl.BlockSpec((None, l, LANES), lambda i, j: (i, 0, c))

    vspec = pl.BlockSpec((None, l // tk, NSA_KV_HEADS, VT_ROWS, tk), lambda i, j: (i, 0, 0, 0, 0))
    return pl.pallas_call(
        functools.partial(_nsa_prompt_kernel, tq=tq, tk=tk, nb=nb),
        grid=(b, l // tq),
        in_specs=[pl.BlockSpec((None, tq, 512), lambda i, j: (i, j, 0)),
                  kspec(0), kspec(1), pl.BlockSpec((l, LANES), lambda i, j: (0, 0)), vspec, vspec,
                  pl.BlockSpec((None, nb, 256), lambda i, j: (i, 0, 0)),
                  pl.BlockSpec((nb, LANES), lambda i, j: (0, 0)),
                  pl.BlockSpec((None, tq, LANES), lambda i, j: (i, j, 0))],
        out_specs=pl.BlockSpec((None, tq, 512), lambda i, j: (i, j, 0)),
        out_shape=jax.ShapeDtypeStruct((b, l, 512), BF16),
        scratch_shapes=_attn_scratch(tq, tk) + [pltpu.SMEM((1,), jnp.int32)],
        compiler_params=_params(("parallel", "arbitrary")),
        name="nsa_prompt",
    )(qn3, kv3, kv3, kx, vst, vwt, cmp3, cx, small3)


def _fox_prompt_kernel(q_ref, qx_ref, k_ref, vt_ref, o_ref, qa_ref, m_ref, s_ref, acc_ref, *, tq, tk):
    assert tq == tk, "the pipelined key loop walks one key tile per query tile"
    qi = pl.program_id(1)
    lane = lax.broadcasted_iota(jnp.int32, (tq, LANES), 1)
    krow = lax.broadcasted_iota(jnp.int32, (tk, tq), 0)
    qcol = lax.broadcasted_iota(jnp.int32, (tk, tq), 1)
    qx = qx_ref[...]
    for g in range(FOX_KV_HEADS):
        for j in range(HPG):
            h = g * HPG + j
            own = ((lane >= FQ0 + 3 * h) & (lane < FQ0 + 3 * h + 3)) | ((lane >= FK0 + 3 * h) & (lane < FK0 + 3 * h + 3))
            qa_ref[h, :, :LANES] = _group_masked(q_ref[:, j * LANES:(j + 1) * LANES], g)
            qa_ref[h, :, LANES:] = jnp.where(own, qx, jnp.zeros_like(qx))
    _heads_run(lambda kj: k_ref[pl.ds(pl.multiple_of(kj * tk, tk), tk), :], vt_ref, qa_ref, s_ref, m_ref, acc_ref,
               0, qi, lambda kj: (qi - kj) * tq + qcol - krow >= 0, False)
    for j in range(HPG):
        pair_t = jnp.concatenate([_finish_t(acc_ref[j]), _finish_t(acc_ref[HPG + j])], axis=0)
        o_ref[:, j * LANES:(j + 1) * LANES] = pair_t.T.astype(BF16)


def _fox_prompt_call(qf3, qx3, kaug3, vft, tq, tk):
    b, l, _ = qf3.shape
    return pl.pallas_call(
        functools.partial(_fox_prompt_kernel, tq=tq, tk=tk),
        grid=(b, l // tq),
        in_specs=[pl.BlockSpec((None, tq, 512), lambda i, j: (i, j, 0)),
                  pl.BlockSpec((None, tq, LANES), lambda i, j: (i, j, 0)),
                  pl.BlockSpec((None, l, 2 * LANES), lambda i, j: (i, 0, 0)),
                  pl.BlockSpec((None, l // tk, FOX_KV_HEADS, VT_ROWS, tk), lambda i, j: (i, 0, 0, 0, 0))],
        out_specs=pl.BlockSpec((None, tq, 512), lambda i, j: (i, j, 0)),
        out_shape=jax.ShapeDtypeStruct((b, l, 512), BF16),
        scratch_shapes=_attn_scratch(tq, tk)[1:5],
        compiler_params=_params(("parallel", "arbitrary")),
        name="fox_prompt",
    )(qf3, qx3, kaug3, vft)


def _lane_cumsum(x):
    n = x.shape[1]
    r = lax.broadcasted_iota(jnp.int32, (n, n), 0)
    c = lax.broadcasted_iota(jnp.int32, (n, n), 1)
    upper = jnp.where(r <= c, 1.0, 0.0).astype(BF16)
    return sum(_nn(p, upper) for p in _split3(x))


def _dec_cmp_kernel(pt_ref, q_ref, rowc_ref, wct_ref, bsum_ref, *rest, pp, nb, past_len):
    del pt_ref
    pages = rest[:pp]
    lfs = rest[pp:2 * pp]
    oc_ref, sel_ref, ftot_ref, fs_ref, cmp_scr, ftot_scr = rest[2 * pp:]
    c = pl.program_id(1)

    @pl.when(c == 0)
    def _():
        ftot_scr[...] = jnp.zeros_like(ftot_scr)

    nblk = bsum_ref.shape[0]
    bsum = bsum_ref[...]
    for plane in range(2):
        w = wct_ref[plane]
        for g in range(NSA_KV_HEADS):
            prod = jnp.concatenate([(p[plane, g] * w).astype(BF16) for p in pages], axis=1)
            blk = _nt(bsum, prod)
            cmp_scr[2 * plane + g, pl.ds(pl.multiple_of(c * nblk, nblk), nblk), :] = blk
    cs = _lane_cumsum(jnp.concatenate([p[...] for p in lfs], axis=0))
    page = cs.shape[1]
    car = ftot_scr[:, 0:1]
    for i in range(pp):
        cs_i = cs[i * FOX_HEADS:(i + 1) * FOX_HEADS, :]
        fs_ref[i] = car + cs_i
        car = car + cs_i[:, page - 1:page]
    ftot_scr[...] = jnp.broadcast_to(car, ftot_scr.shape)

    @pl.when(c == pl.num_programs(1) - 1)
    def _():
        ftot_ref[...] = ftot_scr[...]
        ckb = jnp.concatenate([cmp_scr[0], cmp_scr[1]], axis=1).astype(BF16)
        cvb = jnp.concatenate([cmp_scr[2], cmp_scr[3]], axis=1).astype(BF16)
        nrow = q_ref.shape[0]
        rowc = rowc_ref[...]
        slope = rowc[:, 0:1]
        t = past_len + rowc[:, 1:2]
        nidx = lax.broadcasted_iota(jnp.int32, (nrow, nb), 1)
        cmid = nidx.astype(F32) * CMP_BLOCK + 0.5 * (CMP_BLOCK - 1)
        valid = cmid + 0.5 * (CMP_BLOCK - 1) <= t
        s = jnp.where(valid, _nt(q_ref[...], ckb) - slope * (t - cmid), NEG_INF)
        e = jnp.exp(s - jnp.max(s, axis=-1, keepdims=True))
        p = jnp.where(valid, e / jnp.sum(e, axis=-1, keepdims=True), 0.0)
        oc_ref[...] = _nn(p.astype(BF16), cvb)
        nq = nrow // NSA_HEADS
        n8 = lax.broadcasted_iota(jnp.int32, (nq, nb), 1)
        forced = (n8 == 0) | (n8 == nb - 1)
        sel_rows = []
        for g in range(NSA_KV_HEADS):
            imp = jnp.zeros((nq, nb), F32)
            for j in range(HPG):
                r0 = (g * HPG + j) * nq
                imp = imp + p[r0:r0 + nq, :]
            sel = _top_blocks(imp + jnp.where(forced, FORCE_BONUS, 0.0), SLC_TOPK - 1)
            sel_rows += [sel] * HPG
        sel_ref[...] = ((jnp.concatenate(sel_rows, axis=0) - 1.0) * MASK_BIG).astype(BF16)


def _page_specs(pp, n_pages, block, plane_pair):
    tail = (0,) * (len(block) - 2)

    def one(i):
        return pl.BlockSpec(block, lambda b, c, pt: (pt[b * n_pages + c * pp + i], plane_pair) + tail)
    return [one(i) for i in range(pp)]


def _dec_cmp_call(pt_flat, q_dec, rowc, wct, bsum, nsa_t, lf_pool, pp, n_pages, past_len):
    b, nrow, _ = q_dec.shape
    nb = past_len // CMP_BLOCK
    page = nsa_t.shape[-1]
    nc = n_pages // pp
    assert nb <= LANES, "block mask lanes cover at most 128 cached blocks"
    per = lambda shape: pl.BlockSpec((None,) + shape, lambda i, c, pt: (i, 0, 0))
    cst = lambda shape: pl.BlockSpec(shape, lambda i, c, pt: (0,) * len(shape))
    grid_spec = pltpu.PrefetchScalarGridSpec(
        num_scalar_prefetch=1,
        grid=(b, nc),
        in_specs=[per((nrow, LANES)), cst(rowc.shape), cst(wct.shape), cst(bsum.shape)]
        + _page_specs(pp, n_pages, (None, 2, NSA_KV_HEADS, HEAD_DIM, page), 0)
        + _page_specs(pp, n_pages, (None, FOX_HEADS, page), 0),
        out_specs=[per((nrow, LANES)), per((nrow, nb)), per((FOX_HEADS, LANES)),
                   pl.BlockSpec((None, pp, FOX_HEADS, page), lambda i, c, pt: (i, c, 0, 0))],
        scratch_shapes=[pltpu.VMEM((4, nb, HEAD_DIM), F32), pltpu.VMEM((FOX_HEADS, LANES), F32)],
    )
    return pl.pallas_call(
        functools.partial(_dec_cmp_kernel, pp=pp, nb=nb, past_len=past_len),
        grid_spec=grid_spec,
        out_shape=[jax.ShapeDtypeStruct((b, nrow, LANES), F32),
                   jax.ShapeDtypeStruct((b, nrow, nb), BF16),
                   jax.ShapeDtypeStruct((b, FOX_HEADS, LANES), F32),
                   jax.ShapeDtypeStruct((b, n_pages, FOX_HEADS, page), F32)],
        compiler_params=_params(("parallel", "arbitrary")),
        name="decode_cmp",
    )(pt_flat, q_dec, rowc, wct, bsum, *([nsa_t] * pp), *([lf_pool] * pp))


def _rep_rows(x, n):
    return jnp.concatenate([jnp.broadcast_to(x[h:h + 1, :], (n, x.shape[1])) for h in range(x.shape[0])], axis=0)


def _scr_flash_t(scores, vts, stats):
    staged = []
    for s, (m_ref, l_ref, _) in zip(scores, stats):
        m_old = m_ref[:, 0:1]
        m = jnp.maximum(m_old, jnp.max(s, axis=-1, keepdims=True))
        alpha = jnp.exp(m_old - m)
        p = jnp.exp(s - m)
        l = alpha * l_ref[:, 0:1] + jnp.sum(p, axis=-1, keepdims=True)
        m_ref[...] = jnp.broadcast_to(m, m_ref.shape)
        l_ref[...] = jnp.broadcast_to(l, l_ref.shape)
        staged.append((alpha, p.astype(BF16)))
    for (alpha, p), vt, (_, _, acc_ref) in zip(staged, vts, stats):
        acc_ref[...] = alpha * acc_ref[...] + _nt(p, vt)


def _pages_t(pages, plane):
    return jnp.concatenate(
        [jnp.concatenate([p[plane, g] for g in range(NSA_KV_HEADS)], axis=0) for p in pages], axis=1).astype(BF16)


def _dec_main_kernel(pt_ref, qn_ref, qf_ref, sel_ref, rowc_ref, oc_ref, gate_ref, ftot_ref,
                     newn_ref, newf_ref, newlf_ref, wbuf_ref, wnew_ref, hot_ref, fs_ref, *rest, pp, nb, past_len, nq):
    del pt_ref
    slc = rest[:pp]
    fox = rest[pp:2 * pp]
    (on_ref, of_ref, wst_ref, ms_ref, ls_ref, as_ref, mf_ref, lf_ref, af_ref, ftc_ref, qna_ref) = rest[2 * pp:]
    c = pl.program_id(1)
    nrow = qn_ref.shape[0]
    page = fs_ref.shape[2]
    kc = pp * page
    rowc = rowc_ref[...]
    slope = rowc[:, 0:1]
    t = past_len + rowc[:, 1:2]
    qn = qn_ref[...]
    qf = qf_ref[...]

    @pl.when(c == 0)
    def _():
        for r in (ms_ref, mf_ref):
            r[...] = jnp.full(r.shape, NEG_INF, F32)
        for r in (ls_ref, as_ref, lf_ref, af_ref):
            r[...] = jnp.zeros_like(r)
        f_new = ftot_ref[...] + _lane_cumsum(newlf_ref[...])
        rep = _rep_rows(f_new, nq)
        lane = lax.broadcasted_iota(jnp.int32, rep.shape, 1)
        tok = lax.broadcasted_iota(jnp.int32, rep.shape, 0) & (nq - 1)
        ftc_ref[...] = jnp.broadcast_to(
            jnp.sum(jnp.where(lane == tok, rep, 0.0), axis=-1, keepdims=True), ftc_ref.shape)
        qna_ref[:, :LANES] = qn
        qna_ref[:, LANES:] = sel_ref[...]

    ft = ftc_ref[:, 0:1]

    s_fox = _nn(qf, _pages_t(fox, 0))
    fs = jnp.concatenate([_rep_rows(fs_ref[i], nq) for i in range(pp)], axis=1)
    s_fox = s_fox + (ft - fs)

    kts = jnp.concatenate([_pages_t(slc, 0), hot_ref[...]], axis=0)
    pos = (c * kc + lax.broadcasted_iota(jnp.int32, (1, kc), 1)).astype(F32)
    s_slc = _nn(qna_ref[...], kts) - slope * (t - pos)
    _scr_flash_t([s_slc, s_fox], [_pages_t(slc, 1), _pages_t(fox, 1)],
                 [(ms_ref, ls_ref, as_ref), (mf_ref, lf_ref, af_ref)])

    @pl.when(c == pl.num_programs(1) - 1)
    def _():
        npad = newn_ref.shape[0]
        lane_i = lax.broadcasted_iota(jnp.int32, (1, npad), 1)
        posn = (past_len + lane_i).astype(F32)
        okn = (lane_i < nq) & (posn <= t)
        newn = newn_ref[...]
        sn = jnp.where(okn, _nt(qn, newn[:, :LANES].astype(BF16)) - slope * (t - posn), NEG_INF)
        m, l, acc = _flash_step(sn, newn[:, LANES:].astype(BF16), ms_ref[:, 0:1], ls_ref[:, 0:1], as_ref[...])
        o_s = acc / l
        newf = newf_ref[...]
        fsn = _rep_rows(ftot_ref[...] + _lane_cumsum(newlf_ref[...]), nq)
        sf = jnp.where(okn, _nt(qf, newf[:, :LANES].astype(BF16)) + (ft - fsn), NEG_INF)
        m, l, acc = _flash_step(sf, newf[:, LANES:].astype(BF16), mf_ref[:, 0:1], lf_ref[:, 0:1], af_ref[...])
        of_ref[...] = acc / l
        wbuf = wbuf_ref[...]
        wnew = wnew_ref[...]
        nbuf = wbuf.shape[0]
        kw = jnp.concatenate([wbuf[:, :LANES], wnew[:, :LANES]], axis=0).astype(BF16)
        vw = jnp.concatenate([wbuf[:, LANES:], wnew[:, LANES:]], axis=0).astype(BF16)
        wl = lax.broadcasted_iota(jnp.int32, (1, nbuf + npad), 1)
        posw = jnp.where(wl < nbuf, past_len - nbuf + wl, past_len + wl - nbuf).astype(F32)
        dw = t - posw
        okw = (dw >= 0) & (dw < WINDOW) & ((wl < nbuf) | (wl < nbuf + nq))
        sw = jnp.where(okw, _nt(qn, kw) - slope * dw, NEG_INF)
        ew = jnp.exp(sw - jnp.max(sw, axis=-1, keepdims=True))
        o_w = _nn(ew.astype(BF16), vw) / jnp.sum(ew, axis=-1, keepdims=True)
        gate = gate_ref[...]
        on_ref[...] = gate[:, 0:1] * oc_ref[...] + gate[:, 1:2] * o_s + gate[:, 2:3] * o_w
        wst_ref[0:nbuf - nq, :] = wbuf[nq:, :]
        wst_ref[nbuf - nq:, :] = wnew[0:nq, :]


def _dec_main_call(pt_flat, qn_dec, qf_dec, sel, rowc, o_c, gates, ftot, newn, newf, newlf, wbuf, wnew,
                   hot, fs_past, nsa_t, fox_t, pp, n_pages, past_len, nq):
    b, nrow, _ = qn_dec.shape
    nb = past_len // CMP_BLOCK
    page = nsa_t.shape[-1]
    nc = n_pages // pp
    nbuf = wbuf.shape[1]
    assert nb % 16 == 0 and nb <= LANES, "query-side block mask lanes"
    per = lambda shape: pl.BlockSpec((None,) + shape, lambda i, c, *_: (i, 0, 0))
    cst = lambda shape: pl.BlockSpec(shape, lambda i, c, *_: (0, 0))
    rl = (nrow, LANES)
    kv_block = (None, 2, NSA_KV_HEADS, HEAD_DIM, page)
    grid_spec = pltpu.PrefetchScalarGridSpec(
        num_scalar_prefetch=1,
        grid=(b, nc),
        in_specs=[per(rl), per(rl), per((nrow, nb)), cst(rowc.shape), per(rl), per(rl), per((FOX_HEADS, LANES)),
                  per(newn.shape[1:]), per(newf.shape[1:]), per(newlf.shape[1:]),
                  per((nbuf, 2 * LANES)), per(wnew.shape[1:]),
                  pl.BlockSpec((None,) + hot.shape[1:], lambda i, c, *_: (c, 0, 0)),
                  pl.BlockSpec((None, pp, FOX_HEADS, page), lambda i, c, *_: (i, c, 0, 0))]
        + _page_specs(pp, n_pages, kv_block, 1)
        + _page_specs(pp, n_pages, kv_block, 0),
        out_specs=[per(rl), per(rl), per((nbuf, 2 * LANES))],
        scratch_shapes=[pltpu.VMEM(rl, F32)] * 6 + [pltpu.VMEM(rl, F32), pltpu.VMEM((nrow, LANES + nb), BF16)],
    )
    return pl.pallas_call(
        functools.partial(_dec_main_kernel, pp=pp, nb=nb, past_len=past_len, nq=nq),
        grid_spec=grid_spec,
        out_shape=[jax.ShapeDtypeStruct((b,) + rl, F32), jax.ShapeDtypeStruct((b,) + rl, F32),
                   jax.ShapeDtypeStruct((b, nbuf, 2 * LANES), F32)],
        compiler_params=_params(("parallel", "arbitrary")),
        name="decode_main",
    )(pt_flat, qn_dec, qf_dec, sel, rowc, o_c, gates, ftot, newn, newf, newlf, wbuf, wnew, hot, fs_past,
      *([nsa_t] * pp), *([fox_t] * pp))


def _outproj_kernel(x_ref, mod_ref, g_ref, on_ref, of_ref, wm_ref, wa_ref, wb_ref, wo_ref, o_ref):
    x = x_ref[...]
    nb, nt, d = x.shape
    h = _pre(x, mod_ref, g_ref, 1).reshape(nb * nt, d).astype(BF16)
    gm = jax.nn.sigmoid(_nn(h, wm_ref[...]))
    y = gm[:, :d] * _nn(on_ref[...], wa_ref[...]) + gm[:, d:] * _nn(of_ref[...], wb_ref[...])
    mix = _nn(y.astype(BF16), wo_ref[...]).reshape(nb, nt, d)
    o_ref[...] = x + mod_ref[:, 5:6, :] * mix


def _outproj_call(x3, mod, g_norm, o_n, o_f, wm, wa, wb, wo, blk):
    bs, lt, d = x3.shape
    nb, nt = blk
    rows = nb * nt
    nj = lt // nt
    tok = pl.BlockSpec((rows, o_n.shape[1]), lambda i, j: (i * nj + j, 0))
    return pl.pallas_call(
        _outproj_kernel,
        grid=(bs // nb, nj),
        in_specs=[pl.BlockSpec((nb, nt, d), lambda i, j: (i, j, 0)),
                  pl.BlockSpec((nb, 3 * N_SUB, d), lambda i, j: (i, 0, 0)),
                  _const_spec(g_norm.shape), tok, tok,
                  _const_spec(wm.shape), _const_spec(wa.shape), _const_spec(wb.shape), _const_spec(wo.shape)],
        out_specs=pl.BlockSpec((nb, nt, d), lambda i, j: (i, j, 0)),
        out_shape=jax.ShapeDtypeStruct(x3.shape, F32),
        compiler_params=_params(("parallel", "parallel")),
        name="out_proj",
    )(x3, mod, g_norm, o_n, o_f, wm, wa, wb, wo)


def _pair_major_cols(w):
    d = w.shape[0]
    return w.reshape(d, NSA_HEADS, HEAD_DIM)[:, jnp.array(PAIR_PERM), :].reshape(d, NSA_HEADS * HEAD_DIM)


def _pair_major_rows(w):
    d = w.shape[1]
    return w.reshape(NSA_HEADS, HEAD_DIM, d)[jnp.array(PAIR_PERM)].reshape(NSA_HEADS * HEAD_DIM, d)


def _prep_weights(w_in, b_fgt, g_q_nsa, g_k_nsa, g_q_fox, g_k_fox, w_cmp):
    d = w_in.shape[0]
    hq = NSA_HEADS * HEAD_DIM
    o = 0
    w_qn = w_in[:, o:o + hq]; o += hq
    w_kvn = w_in[:, o:o + 3 * 2 * LANES]; o += 3 * 2 * LANES
    w_gt = w_in[:, o:o + 3 * NSA_HEADS]; o += 3 * NSA_HEADS
    w_qf = w_in[:, o:o + hq]; o += hq
    w_kvf = w_in[:, o:o + 2 * LANES]; o += 2 * LANES
    w_ff = w_in[:, o:o + FOX_HEADS]; o += FOX_HEADS
    w_merge = w_in[:, o:]
    k_n = [w_kvn[:, br * 256:br * 256 + LANES] for br in range(3)]
    v_n = [w_kvn[:, br * 256 + LANES:(br + 1) * 256] for br in range(3)]
    small = jnp.concatenate([w_ff, w_gt, jnp.zeros((d, LANES - FOX_HEADS - 3 * NSA_HEADS), w_in.dtype)], axis=1)
    w1 = jnp.concatenate([_pair_major_cols(w_qn), _pair_major_cols(w_qf), k_n[0], k_n[1], k_n[2], w_kvf[:, :LANES],
                          v_n[0], v_n[1], v_n[2], w_kvf[:, LANES:], small], axis=1).astype(BF16)
    scale = HEAD_DIM ** -0.5
    gains = jnp.concatenate([jnp.tile(g_q_nsa * scale, NSA_HEADS), jnp.tile(g_q_fox * scale, FOX_HEADS),
                             jnp.tile(g_k_nsa[0], 2), jnp.tile(g_k_nsa[1], 2), jnp.tile(g_k_nsa[2], 2),
                             jnp.tile(g_k_fox, 2)]).reshape(1, N_NORMED * LANES)
    bf_pad = jnp.concatenate([b_fgt, jnp.zeros((LANES - FOX_HEADS,), F32)]).reshape(1, LANES)
    lane = jnp.arange(LANES)
    seg = (lane[:, None] // HEAD_DIM == lane[None, :] // HEAD_DIM).astype(BF16)
    wc = jnp.concatenate([w_cmp[0], w_cmp[0], w_cmp[1], w_cmp[1]], axis=1)
    return w1, gains, bf_pad, seg, wc, w_merge.astype(BF16)


def _position_lanes(l):
    pos = jnp.arange(l)
    lane = jnp.arange(LANES)[None, :]
    blk = (pos // CMP_BLOCK)[:, None]
    kx = jnp.where(lane < CMP_BLOCK, (lane == blk).astype(F32),
                   jnp.where((lane == XL_ONE_A) | (lane == XL_ONE_B), 1.0,
                             jnp.where(lane == XL_HI, blk.astype(F32),
                                       jnp.where(lane == XL_LO, (pos % CMP_BLOCK)[:, None].astype(F32), 0.0))))
    n = jnp.arange(LANES)[:, None]
    cx = jnp.where((lane == XL_ONE_A) | (lane == XL_ONE_B), 1.0,
                   jnp.where(lane == XL_HI, n.astype(F32), jnp.where(lane == XL_LO, 0.5 * (CMP_BLOCK - 1), 0.0)))
    return kx.astype(BF16), cx.astype(BF16)


def _to_decode_rows(q, b, nq):
    q5 = q.reshape(b, nq, HPG, 2, HEAD_DIM)
    eye = jnp.eye(2, dtype=q.dtype)
    return jnp.einsum('btjgd,gk->bgjtkd', q5, eye).reshape(b, NSA_HEADS * nq, LANES)


def _from_decode_rows(o, b, nq):
    o6 = o.reshape(b, 2, HPG, nq, 2, HEAD_DIM)
    own = jnp.stack([o6[:, 0, :, :, 0], o6[:, 1, :, :, 1]], axis=1)
    return own.transpose(0, 3, 1, 2, 4).reshape(b * nq, NSA_HEADS * HEAD_DIM)


def _pad_rows(x, n):
    return jnp.pad(x, ((0, 0), (0, n - x.shape[1]), (0, 0)))


def kernel(x_prompt, x_sample, cache_nsa_kv, cache_fox_kv, cache_fox_logf, state_win_kv, page_table,
           c_prompt, c_sample, w_ada, b_ada, g_norm, w_ffn_up, w_ffn_down, w_in, b_fgt,
           g_q_nsa, g_k_nsa, g_q_fox, g_k_fox, w_cmp, w_br_nsa, w_br_fox, w_out):
    assert w_ada.shape[0] == 1, "single-layer trunk"
    bp, lp, d = x_prompt.shape
    bs, nq, _ = x_sample.shape
    n_pool, page = cache_nsa_kv.shape[1:3]
    n_pages = page_table.shape[1]
    past_len = n_pages * page
    nbuf = state_win_kv.shape[2]
    dff = w_ffn_down.shape[2]
    assert nq % 8 == 0 and nq & (nq - 1) == 0 and nq <= CMP_BLOCK and past_len // CMP_BLOCK >= SLC_TOPK
    assert nbuf == WINDOW and lp % 512 == 0

    mod = _mod_call(jnp.concatenate([c_prompt, c_sample], axis=0), w_ada[0], b_ada[0]).reshape(bp + bs, 3 * N_SUB, d)
    mod_p, mod_s = mod[:bp], mod[bp:]
    gn = g_norm[0]
    up = w_ffn_up[0].astype(BF16)
    ffn_w = [(up[i, :, :dff], up[i, :, dff:], w_ffn_down[0, i].astype(BF16)) for i in range(2)]
    w1, gains, bf_pad, seg, wc, w_merge = _prep_weights(
        w_in[0], b_fgt[0], g_q_nsa[0], g_k_nsa[0], g_q_fox[0], g_k_fox[0], w_cmp[0])
    wo = w_out[0].astype(BF16)
    wa_nat, wb_nat = w_br_nsa[0].astype(BF16), w_br_fox[0].astype(BF16)
    wa_pm, wb_pm = _pair_major_rows(wa_nat), _pair_major_rows(wb_nat)

    pblk = (1, 512)
    tq, tk = 256, 256
    x1 = _ffn_call(x_prompt, mod_p, gn, *ffn_w[0], 0, pblk)
    qn, qf, nsa_new, win_new, fox_new, small, kvb, cmpb, vt_slc, vt_win, vt_fox = _inproj_call(
        x1, mod_p, gn, w1, gains, bf_pad, seg, wc, pblk, True, tk)
    small3 = small.reshape(bp, lp, LANES)
    logf_p = small3[:, :, :FOX_HEADS]
    kv3 = kvb.reshape(bp, lp, 3 * LANES)
    kaug_f, qx_f = _cumsum_call(small3, kv3, 512)
    kx, cx = _position_lanes(lp)
    o_n = _nsa_prompt_call(qn.reshape(bp, lp, 512), kv3, kx, vt_slc, vt_win,
                           cmpb.reshape(bp, lp // CMP_BLOCK, 256), cx[:lp // CMP_BLOCK], small3, tq, tk)
    o_f = _fox_prompt_call(qf.reshape(bp, lp, 512), qx_f, kaug_f, vt_fox, tq, tk)
    x2 = _outproj_call(x1, mod_p, gn, o_n.reshape(bp * lp, 512), o_f.reshape(bp * lp, 512),
                       w_merge, wa_pm, wb_pm, wo, pblk)
    y_prompt = _ffn_call(x2, mod_p, gn, *ffn_w[1], 2, pblk)
    def rows_major(t, planes, rows):
        return t.reshape(bp, planes, NSA_KV_HEADS, HEAD_DIM, rows).transpose(0, 4, 1, 2, 3)[None]

    nwin = min(WINDOW, lp)
    nsa_p = rows_major(nsa_new, 4, lp)
    fox_p = rows_major(fox_new, 2, lp)
    win_p = rows_major(win_new[:, :, lp - nwin:], 2, nwin)

    sblk = (min(bs, 64), nq)
    pp = next(p for p in (16, 8, 4, 2, 1) if n_pages % p == 0)
    x1s = _ffn_call(x_sample, mod_s, gn, *ffn_w[0], 0, sblk)
    qn_s, qf_s, nsa_s, win_s, fox_s, small_s, _ = _inproj_call(
        x1s, mod_s, gn, w1, gains, bf_pad, seg, wc, sblk, False)
    nrow = NSA_HEADS * nq
    qn_dec = _to_decode_rows(qn_s, bs, nq)
    qf_dec = _to_decode_rows(qf_s, bs, nq)
    rows = jnp.arange(nrow)
    rowc = jnp.zeros((nrow, LANES), F32).at[:, 0].set(jnp.array(SLOPES, F32)[rows // nq]).at[:, 1].set(
        (rows % nq).astype(F32))
    sm3 = small_s.reshape(bs, nq, LANES)
    gates = sm3[:, :, FOX_HEADS:FOX_HEADS + 3 * NSA_HEADS].reshape(bs, nq, 3, NSA_HEADS)
    gates = jnp.pad(gates.transpose(0, 3, 1, 2).reshape(bs, nrow, 3), ((0, 0), (0, 0), (0, LANES - 3)))
    newlf = _pad_rows(sm3[:, :, :FOX_HEADS], LANES).transpose(0, 2, 1)
    newn = _pad_rows(nsa_s.reshape(bs, nq, 512)[:, :, 256:], LANES)
    newf = _pad_rows(fox_s.reshape(bs, nq, 256), LANES)
    wnew = _pad_rows(win_s.reshape(bs, nq, 256), LANES)
    nsa_t = cache_nsa_kv[0].transpose(0, 2, 3, 4, 1)
    fox_t = cache_fox_kv[0].transpose(0, 2, 3, 4, 1)
    wct = jnp.tile(w_cmp[0].transpose(0, 2, 1), (1, 1, page // CMP_BLOCK))
    kc = pp * page
    col_blk = jnp.arange(kc) // CMP_BLOCK
    bsum = (jnp.arange(kc // CMP_BLOCK)[:, None] == col_blk[None, :]).astype(BF16)
    nblocks = past_len // CMP_BLOCK
    hot = (jnp.arange(nblocks)[None, :, None]
           == (jnp.arange(n_pages // pp)[:, None, None] * (kc // CMP_BLOCK) + col_blk[None, None, :])).astype(BF16)
    lf_pool = cache_fox_logf[0].astype(F32).transpose(0, 2, 1)
    wbuf = state_win_kv[0].reshape(bs, nbuf, 256)
    pt_flat = page_table.reshape(-1).astype(jnp.int32)
    o_c, sel, ftot, fs_past = _dec_cmp_call(pt_flat, qn_dec, rowc, wct, bsum, nsa_t, lf_pool, pp, n_pages, past_len)
    on_dec, of_dec, wst = _dec_main_call(pt_flat, qn_dec, qf_dec, sel, rowc, o_c, gates, ftot, newn, newf,
                                         newlf, wbuf, wnew, hot, fs_past, nsa_t, fox_t, pp, n_pages, past_len, nq)
    o_ns = _from_decode_rows(on_dec, bs, nq).astype(BF16)
    o_fs = _from_decode_rows(of_dec, bs, nq).astype(BF16)
    x2s = _outproj_call(x1s, mod_s, gn, o_ns, o_fs, w_merge, wa_nat, wb_nat, wo, sblk)
    y_sample = _ffn_call(x2s, mod_s, gn, *ffn_w[1], 2, sblk)
    nsa_o = nsa_s.reshape(1, bs, nq, 4, NSA_KV_HEADS, HEAD_DIM)
    fox_o = fox_s.reshape(1, bs, nq, 2, FOX_KV_HEADS, HEAD_DIM)
    win_o = wst.reshape(1, bs, nbuf, 2, NSA_KV_HEADS, HEAD_DIM)

    return (y_prompt, y_sample, nsa_p, nsa_o, fox_p, fox_o,
            logf_p[None], sm3[None, :, :, :FOX_HEADS], win_p, win_o)
```

```python
import functools

import jax
import jax.numpy as jnp
from jax import lax
from jax.experimental import pallas as pl
from jax.experimental.pallas import tpu as pltpu

HEAD_DIM = 64
NSA_HEADS = 8
NSA_KV_HEADS = 2
FOX_HEADS = 8
FOX_KV_HEADS = 2
HPG = 4
CMP_BLOCK = 64
CMP_SHIFT = 6
SLC_TOPK = 8
WINDOW = 512
N_SUB = 3
RMS_EPS = 1e-6
NEG_INF = -1e30
FORCE_BONUS = 1e4
LANES = 128
PAIR_PERM = (0, 4, 1, 5, 2, 6, 3, 7)
SLOPES = tuple(2.0 ** -(h + 1) for h in range(NSA_HEADS))
VMEM_LIMIT = 56 * 1024 * 1024

F32 = jnp.float32
BF16 = jnp.bfloat16


def _nt(a, b):
    return lax.dot_general(a, b, (((1,), (1,)), ((), ())), preferred_element_type=F32)


def _nn(a, b):
    return jnp.dot(a, b, preferred_element_type=F32)


def _split3(x):
    hi = x.astype(BF16)
    r1 = x - hi.astype(F32)
    mid = r1.astype(BF16)
    lo = (r1 - mid.astype(F32)).astype(BF16)
    return hi, mid, lo


def _params(sem):
    return pltpu.CompilerParams(dimension_semantics=sem, vmem_limit_bytes=VMEM_LIMIT)


def _const_spec(shape):
    nd = len(shape)
    return pl.BlockSpec(shape, lambda *_: (0,) * nd, pipeline_mode=pl.Buffered(1))


def _mod_kernel(c_ref, w_ref, b_ref, o_ref):
    c = c_ref[...]
    a = (c * jax.nn.sigmoid(c)).astype(BF16)
    o_ref[...] = _nn(a, w_ref[...].astype(BF16)) + b_ref[...]


def _mod_call(c_all, w_ada, b_ada):
    bc, d = c_all.shape
    n = w_ada.shape[1]
    tn = 1024 if n % 1024 == 0 else n
    return pl.pallas_call(
        _mod_kernel,
        grid=(n // tn,),
        in_specs=[pl.BlockSpec((bc, d), lambda i: (0, 0)),
                  pl.BlockSpec((d, tn), lambda i: (0, i)),
                  pl.BlockSpec((1, tn), lambda i: (0, i))],
        out_specs=pl.BlockSpec((bc, tn), lambda i: (0, i)),
        out_shape=jax.ShapeDtypeStruct((bc, n), F32),
        compiler_params=_params(("parallel",)),
        name="adaln_mod",
    )(c_all, w_ada, b_ada.reshape(1, n))


def _pre(x, mod_ref, g_ref, sub):
    ms = jnp.mean(x * x, axis=-1, keepdims=True)
    y = x * lax.rsqrt(ms + RMS_EPS) * g_ref[sub:sub + 1, :]
    return y * (1.0 + mod_ref[:, 3 * sub + 1:3 * sub + 2, :]) + mod_ref[:, 3 * sub:3 * sub + 1, :]


def _ffn_kernel(x_ref, mod_ref, g_ref, wa_ref, wb_ref, wd_ref, o_ref, act_ref, *, sub, fc):
    x = x_ref[...]
    nb, nt, d = x.shape
    h = _pre(x, mod_ref, g_ref, sub).reshape(nb * nt, d).astype(BF16)
    dff = wa_ref.shape[1]
    for c in range(dff // fc):
        a = _nn(h, wa_ref[:, c * fc:(c + 1) * fc])
        b = _nn(h, wb_ref[:, c * fc:(c + 1) * fc])
        act_ref[:, c * fc:(c + 1) * fc] = (a * jax.nn.sigmoid(a) * b).astype(BF16)
    y = _nn(act_ref[...], wd_ref[...]).reshape(nb, nt, d)
    o_ref[...] = x + 0.5 * mod_ref[:, 3 * sub + 2:3 * sub + 3, :] * y


def _ffn_call(x3, mod, g_norm, wa, wb, wd, sub, blk):
    bs, lt, d = x3.shape
    nb, nt = blk
    dff = wa.shape[1]
    fc = 256 if dff % 256 == 0 else dff
    grid = (bs // nb, lt // nt)
    return pl.pallas_call(
        functools.partial(_ffn_kernel, sub=sub, fc=fc),
        grid=grid,
        in_specs=[pl.BlockSpec((nb, nt, d), lambda i, j: (i, j, 0)),
                  pl.BlockSpec((nb, 3 * N_SUB, d), lambda i, j: (i, 0, 0)),
                  _const_spec(g_norm.shape), _const_spec(wa.shape), _const_spec(wb.shape),
                  _const_spec(wd.shape)],
        out_specs=pl.BlockSpec((nb, nt, d), lambda i, j: (i, j, 0)),
        out_shape=jax.ShapeDtypeStruct(x3.shape, F32),
        scratch_shapes=[pltpu.VMEM((nb * nt, dff), BF16)],
        compiler_params=_params(("parallel", "parallel")),
        name="ffn_sub%d" % sub,
    )(x3, mod, g_norm, wa, wb, wd)


N_NORMED = 12


def _inproj_kernel(x_ref, mod_ref, g_ref, w_ref, gain_ref, bf_ref, seg_ref, *rest, with_cmp):
    if with_cmp:
        wc_ref, qn_ref, qf_ref, nsa_ref, win_ref, fox_ref, sm_ref, kv_ref, cmp_ref, *vt_refs = rest
    else:
        qn_ref, qf_ref, nsa_ref, win_ref, fox_ref, sm_ref, kv_ref = rest
    x = x_ref[...]
    nb, nt, d = x.shape
    h = _pre(x, mod_ref, g_ref, 1).reshape(nb * nt, d).astype(BF16)
    z = _nn(h, w_ref[...])
    seg = seg_ref[...]
    normed = []
    for c in range(N_NORMED):
        zc = z[:, c * LANES:(c + 1) * LANES]
        zz = zc * zc
        hi = zz.astype(BF16)
        lo = (zz - hi.astype(F32)).astype(BF16)
        ss = _nn(hi, seg) + _nn(lo, seg)
        normed.append(zc * lax.rsqrt(ss * (1.0 / HEAD_DIM) + RMS_EPS) * gain_ref[:, c * LANES:(c + 1) * LANES])
    raw = [z[:, (N_NORMED + c) * LANES:(N_NORMED + c + 1) * LANES] for c in range(4)]
    k_cmp, k_slc, k_win, k_fox = normed[8:12]
    v_cmp, v_slc, v_win, v_fox = raw
    for j in range(4):
        qn_ref[:, j * LANES:(j + 1) * LANES] = normed[j].astype(BF16)
        qf_ref[:, j * LANES:(j + 1) * LANES] = normed[4 + j].astype(BF16)
    def put(ref, c, t, vt_ref=None):
        if not with_cmp:
            ref[:, c * LANES:(c + 1) * LANES] = t
            return
        tt = t.T
        ref[c * LANES:(c + 1) * LANES, :] = tt
        if vt_ref is not None:
            n_kt, n_g, vrows, tkv = vt_ref.shape
            tail = jnp.where(lax.broadcasted_iota(jnp.int32, (vrows - HEAD_DIM, tkv), 0) == 0, 1.0, 0.0).astype(BF16)
            for kt in range(n_kt):
                for g in range(n_g):
                    vt_ref[kt, g, :HEAD_DIM, :] = tt[g * HEAD_DIM:(g + 1) * HEAD_DIM,
                                                      kt * tkv:(kt + 1) * tkv].astype(BF16)
                    vt_ref[kt, g, HEAD_DIM:, :] = tail

    vts = vt_refs if with_cmp else (None, None, None)
    for c, t in enumerate((k_cmp, v_cmp, k_slc)):
        put(nsa_ref, c, t)
    put(nsa_ref, 3, v_slc, vts[0])
    put(win_ref, 0, k_win)
    put(win_ref, 1, v_win, vts[1])
    put(fox_ref, 0, k_fox)
    put(fox_ref, 1, v_fox, vts[2])
    for c, t in enumerate((k_slc, k_win, k_fox)):
        kv_ref[:, c * LANES:(c + 1) * LANES] = t.astype(BF16)
    zs = z[:, 16 * LANES:17 * LANES]
    lane = lax.broadcasted_iota(jnp.int32, zs.shape, 1)
    zf = zs + bf_ref[...]
    logf = jnp.minimum(zf, 0.0) - jnp.log1p(jnp.exp(-jnp.abs(zf)))
    sm_ref[...] = jnp.where(lane < FOX_HEADS, logf, jax.nn.sigmoid(zs))
    if with_cmp:
        rows = nb * nt
        kvc = jnp.concatenate([k_cmp, v_cmp], axis=1).reshape(rows // CMP_BLOCK, CMP_BLOCK, 2 * LANES)
        cmp_ref[...] = jnp.sum(kvc * wc_ref[...][None], axis=1)


def _inproj_call(x3, mod, g_norm, w1, gains, bf_pad, seg, wc, blk, with_cmp, tk=None):
    bs, lt, d = x3.shape
    nb, nt = blk
    rows = nb * nt
    t = bs * lt
    grid = (bs // nb, lt // nt)
    nj = lt // nt

    def tok(cols):
        return pl.BlockSpec((rows, cols), lambda i, j: (i * nj + j, 0))

    in_specs = [pl.BlockSpec((nb, nt, d), lambda i, j: (i, j, 0)),
                pl.BlockSpec((nb, 3 * N_SUB, d), lambda i, j: (i, 0, 0)),
                _const_spec(g_norm.shape), _const_spec(w1.shape), _const_spec(gains.shape),
                _const_spec(bf_pad.shape), _const_spec(seg.shape)]
    args = [x3, mod, g_norm, w1, gains, bf_pad, seg]
    out_specs = [tok(512), tok(512), tok(512), tok(256), tok(256), tok(128), tok(384)]
    out_shape = [jax.ShapeDtypeStruct((t, 512), BF16), jax.ShapeDtypeStruct((t, 512), BF16),
                 jax.ShapeDtypeStruct((t, 512), F32), jax.ShapeDtypeStruct((t, 256), F32),
                 jax.ShapeDtypeStruct((t, 256), F32), jax.ShapeDtypeStruct((t, 128), F32),
                 jax.ShapeDtypeStruct((t, 384), BF16)]
    if with_cmp:
        assert nb == 1
        for k, cols in ((2, 512), (3, 256), (4, 256)):
            out_specs[k] = pl.BlockSpec((None, cols, nt), lambda i, j: (i, 0, j))
            out_shape[k] = jax.ShapeDtypeStruct((bs, cols, lt), F32)
        in_specs.append(_const_spec(wc.shape))
        args.append(wc)
        out_specs.append(pl.BlockSpec((rows // CMP_BLOCK, 256), lambda i, j: (i * nj + j, 0)))
        out_shape.append(jax.ShapeDtypeStruct((t // CMP_BLOCK, 256), F32))
        for _ in range(3):
            out_specs.append(pl.BlockSpec((None, nt // tk, NSA_KV_HEADS, VT_ROWS, tk), lambda i, j: (i, j, 0, 0, 0)))
            out_shape.append(jax.ShapeDtypeStruct((bs, lt // tk, NSA_KV_HEADS, VT_ROWS, tk), BF16))
    return pl.pallas_call(
        functools.partial(_inproj_kernel, with_cmp=with_cmp),
        grid=grid, in_specs=in_specs, out_specs=out_specs, out_shape=out_shape,
        compiler_params=_params(("parallel", "parallel")),
        name="in_proj",
    )(*args)


FQ0 = 0
FK0 = 3 * FOX_HEADS


def _cumsum_kernel(tok_ref, k_ref, kaug_ref, qx_ref, carry_ref):
    @pl.when(pl.program_id(1) == 0)
    def _():
        carry_ref[...] = jnp.zeros_like(carry_ref)

    tc = tok_ref.shape[0]
    r = lax.broadcasted_iota(jnp.int32, (tc, tc), 0)
    c = lax.broadcasted_iota(jnp.int32, (tc, tc), 1)
    lower = jnp.where(r >= c, 1.0, 0.0).astype(BF16)
    cs = carry_ref[...] + sum(_nn(lower, p) for p in _split3(tok_ref[...]))
    carry_ref[...] = cs[tc - 1:tc, :]
    hrow = lax.broadcasted_iota(jnp.int32, (LANES, LANES), 0)
    col = lax.broadcasted_iota(jnp.int32, (LANES, LANES), 1)
    placed = jnp.zeros((tc, LANES), F32)
    for p, piece in enumerate(_split3(cs)):
        tgt = 3 * hrow + p
        place = jnp.where((hrow < FOX_HEADS) & ((col == FQ0 + tgt) | (col == FK0 + tgt)), 1.0, 0.0).astype(BF16)
        placed = placed + _nn(piece, place)
    lane = lax.broadcasted_iota(jnp.int32, (tc, LANES), 1)
    in_q = lane < FK0
    in_k = (lane >= FK0) & (lane < 2 * FK0)
    qx_ref[...] = jnp.where(in_q, placed, jnp.where(in_k, 1.0, 0.0)).astype(BF16)
    kaug_ref[:, :LANES] = k_ref[...]
    kaug_ref[:, LANES:] = jnp.where(in_q, 1.0, jnp.where(in_k, -placed, 0.0)).astype(BF16)


def _cumsum_call(small3, kv3, tc):
    b, l, _ = small3.shape
    return pl.pallas_call(
        _cumsum_kernel,
        grid=(b, l // tc),
        in_specs=[pl.BlockSpec((None, tc, LANES), lambda i, j: (i, j, 0)),
                  pl.BlockSpec((None, tc, LANES), lambda i, j: (i, j, 2))],
        out_specs=[pl.BlockSpec((None, tc, 2 * LANES), lambda i, j: (i, j, 0)),
                   pl.BlockSpec((None, tc, LANES), lambda i, j: (i, j, 0))],
        out_shape=[jax.ShapeDtypeStruct((b, l, 2 * LANES), BF16),
                   jax.ShapeDtypeStruct((b, l, LANES), BF16)],
        scratch_shapes=[pltpu.VMEM((1, LANES), F32)],
        compiler_params=_params(("parallel", "arbitrary")),
        name="fox_cumsum",
    )(small3, kv3)


def _flash_step(s, v, m, l, acc):
    m_new = jnp.maximum(m, jnp.max(s, axis=-1, keepdims=True))
    alpha = jnp.exp(m - m_new)
    p = jnp.exp(s - m_new)
    l = alpha * l + jnp.sum(p, axis=-1, keepdims=True)
    acc = alpha * acc + _nn(p.astype(BF16), v)
    return m_new, l, acc


def _flash_init(rows):
    return (jnp.full((rows, 1), NEG_INF, F32), jnp.zeros((rows, 1), F32), jnp.zeros((rows, LANES), F32))


def _group_masked(qp, g):
    lane = lax.broadcasted_iota(jnp.int32, qp.shape, 1)
    keep = (lane < HEAD_DIM) if g == 0 else (lane >= HEAD_DIM)
    return jnp.where(keep, qp, jnp.zeros_like(qp))


def _top_blocks(score, n_pick, axis=1):
    idx = lax.broadcasted_iota(jnp.int32, score.shape, axis).astype(F32)
    sel = jnp.zeros(score.shape, F32)
    for _ in range(n_pick):
        mx = jnp.max(score, axis=axis, keepdims=True)
        first = jnp.min(jnp.where(score == mx, idx, 1e9), axis=axis, keepdims=True)
        pick = idx == first
        sel = jnp.where(pick, 1.0, sel)
        score = jnp.where(pick, -jnp.inf, score)
    return sel


XL_ONE_A = CMP_BLOCK
XL_ONE_B = CMP_BLOCK + 1
XL_HI = CMP_BLOCK + 2
XL_LO = CMP_BLOCK + 3
MASK_BIG = 1e30
N_FREE = SLC_TOPK - 3


VT_ROWS = HEAD_DIM + 16


def _col_max(s):
    k = s.shape[0]
    if k % 64 == 0:
        s = jnp.max(s.reshape(8, k // 8, s.shape[1]), axis=0)
    return jnp.max(s, axis=0, keepdims=True)


AHEAD = 1


def _heads_scores(k, qa_ref, ok):
    out = []
    for h in range(qa_ref.shape[0]):
        s = _nt(k, qa_ref[h])
        out.append(s if ok is None else jnp.where(ok, s, NEG_INF))
    return out


def _heads_stage(scores, s_ref):
    for h, s in enumerate(scores):
        s_ref[h] = s


def _heads_consume(vt, s_ref, m_ref, acc_ref):
    nh = s_ref.shape[0]
    probs, alphas = [], []
    for h in range(nh):
        s = s_ref[h]
        m_old = m_ref[h:h + 1, :]
        m_new = jnp.maximum(m_old, _col_max(s))
        m_ref[h:h + 1, :] = m_new
        alphas.append(jnp.exp(m_old - m_new))
        probs.append(jnp.exp(s - m_new).astype(BF16))
    for h in range(nh):
        acc_ref[h] = alphas[h] * acc_ref[h] + _nn(vt[h // HPG], probs[h])


def _heads_run(k_of, vt_ref, qa_ref, s_ref, m_ref, acc_ref, lo, qi, mask_of, mask_all, need_ref=None):
    m_ref[...] = jnp.full(m_ref.shape, NEG_INF, F32)
    acc_ref[...] = jnp.zeros_like(acc_ref)
    _heads_stage(_heads_scores(k_of(qi), qa_ref, mask_of(qi)), s_ref)

    def step(kj, staged):
        k = k_of(kj)
        ok = mask_of(kj) if mask_all else None
        vt = vt_ref[staged]
        nh = qa_ref.shape[0]
        new = [_nt(k, qa_ref[h]) if h < AHEAD else None for h in range(nh)]
        for h in range(nh):
            if h + AHEAD < nh:
                new[h + AHEAD] = _nt(k, qa_ref[h + AHEAD])
            s_new = new[h]
            s = s_ref[h]
            m_old = m_ref[h:h + 1, :]
            m_new = jnp.maximum(m_old, _col_max(s))
            m_ref[h:h + 1, :] = m_new
            p = jnp.exp(s - m_new).astype(BF16)
            acc_ref[h] = jnp.exp(m_old - m_new) * acc_ref[h] + _nn(vt[h // HPG], p)
            s_ref[h] = s_new if ok is None else jnp.where(ok, s_new, NEG_INF)
        return kj

    def body(kj, staged):
        if need_ref is None:
            return step(kj, staged)
        wanted = lax.shift_right_logical(need_ref[0], kj) & 1
        return lax.cond(wanted > 0, lambda st: step(kj, st), lambda st: st, staged)

    staged = lax.fori_loop(lo, qi, body, qi)
    _heads_consume(vt_ref[staged], s_ref, m_ref, acc_ref)


def _attn_scratch(tq, tk):
    return [pltpu.VMEM((NSA_HEADS, tq, 2 * LANES), BF16), pltpu.VMEM((NSA_HEADS, tq, 2 * LANES), BF16),
            pltpu.VMEM((NSA_HEADS, tq), F32), pltpu.VMEM((NSA_HEADS, tk, tq), F32),
            pltpu.VMEM((NSA_HEADS, VT_ROWS, tq), F32), pltpu.VMEM((NSA_HEADS, HEAD_DIM, tq), F32)]


def _finish_t(acc):
    return acc[:HEAD_DIM, :] / acc[HEAD_DIM:HEAD_DIM + 1, :]


def _nsa_prompt_kernel(q_ref, ks_ref, kw_ref, kx_ref, vst_ref, vwt_ref, cmp_ref, cx_ref, sm_ref, o_ref,
                       qas_ref, qaw_ref, m_ref, s_ref, acc_ref, out_ref, need_ref, *, tq, tk, nb):
    qi = pl.program_id(1)
    lane = lax.broadcasted_iota(jnp.int32, (tq, LANES), 1)
    t2 = qi * tq + lax.broadcasted_iota(jnp.int32, (tq, LANES), 0)
    t_hi = lax.shift_right_logical(t2, CMP_SHIFT).astype(F32)
    t_lo = (t2 & (CMP_BLOCK - 1)).astype(F32)
    alibi_q = jnp.where(lane == XL_ONE_A, -CMP_BLOCK * t_hi,
                        jnp.where(lane == XL_ONE_B, -t_lo,
                                  jnp.where(lane == XL_HI, float(CMP_BLOCK), jnp.where(lane == XL_LO, 1.0, 0.0))))
    krow = lax.broadcasted_iota(jnp.int32, (tk, tq), 0)
    qcol = lax.broadcasted_iota(jnp.int32, (tk, tq), 1)

    cmpv = cmp_ref[...]
    ck_aug = jnp.concatenate([cmpv[:, :LANES].astype(BF16), cx_ref[...]], axis=1)
    cvt = jnp.concatenate([cmpv[:, LANES:], jnp.zeros((LANES - nb, LANES), F32)],
                          axis=0).T[:, :nb].astype(BF16)
    brow = lax.broadcasted_iota(jnp.int32, (nb, tq), 0)
    tcol = qi * tq + lax.broadcasted_iota(jnp.int32, (nb, tq), 1)
    valid_c = (brow + 1) * CMP_BLOCK - 1 <= tcol
    cur = lax.shift_right_logical(tcol, CMP_SHIFT)
    forced = (brow == 0) | (brow == cur) | (brow == cur - 1)
    free = (brow * CMP_BLOCK <= tcol) & jnp.logical_not(forced)
    smt = sm_ref[...].T

    qx = [SLOPES[h] * alibi_q for h in range(NSA_HEADS)]
    for g in range(NSA_KV_HEADS):
        for j in range(HPG):
            h = g * HPG + j
            qaw_ref[h, :, :LANES] = _group_masked(q_ref[:, j * LANES:(j + 1) * LANES], g)
            qaw_ref[h, :, LANES:] = qx[h].astype(BF16)
    s_c = [jnp.where(valid_c, _nt(ck_aug, qaw_ref[h]), NEG_INF) for h in range(NSA_HEADS)]
    p_c = []
    for s in s_c:
        e = jnp.exp(s - jnp.max(s, axis=0, keepdims=True))
        p_c.append(jnp.where(valid_c, e / jnp.sum(e, axis=0, keepdims=True), 0.0))
    for h in range(NSA_HEADS):
        g = h // HPG
        out_ref[h] = smt[FOX_HEADS + h:FOX_HEADS + h + 1, :] * _nn(cvt[g * HEAD_DIM:(g + 1) * HEAD_DIM, :],
                                                                 p_c[h].astype(BF16))
    imp = jnp.stack([sum(p_c[g * HPG:(g + 1) * HPG]) for g in range(NSA_KV_HEADS)])
    score = jnp.where(free[None], imp, jnp.where(forced[None], 2 * NEG_INF, NEG_INF))
    sel_all = jnp.where(forced[None], 1.0, _top_blocks(score, N_FREE, axis=1))
    picked = jnp.zeros((8, LANES), F32)
    for g in range(NSA_KV_HEADS):
        sel = jnp.concatenate([sel_all[g], jnp.zeros((LANES - nb, tq), F32)], axis=0).T
        sel_q = jnp.where(lane < CMP_BLOCK, (sel - 1.0) * MASK_BIG, 0.0)
        picked = jnp.maximum(picked, jnp.max(sel, axis=0, keepdims=True))
        for j in range(HPG):
            h = g * HPG + j
            qas_ref[h, :, :LANES] = qaw_ref[h, :, :LANES]
            qas_ref[h, :, LANES:] = (qx[h] + sel_q).astype(BF16)

    assert tq == tk, "the pipelined key loop walks one key tile per query tile"
    bpt = tk // CMP_BLOCK
    assert bpt & (bpt - 1) == 0 and nb // bpt <= 24, "tile bits must be exact in f32"
    sh = 1
    while sh < bpt:
        picked = jnp.maximum(picked, pltpu.roll(picked, sh, axis=1))
        sh *= 2
    blane = lax.broadcasted_iota(jnp.int32, (8, LANES), 1)
    weight = jnp.where(((blane & (bpt - 1)) == bpt - 1) & (blane < nb),
                       lax.shift_left(jnp.ones_like(blane), lax.shift_right_logical(blane, bpt.bit_length() - 1)),
                       0).astype(F32)
    bits = jnp.sum(picked[0:1, :] * weight[0:1, :], axis=1, keepdims=True)
    need_ref[0] = bits[0, 0].astype(jnp.int32)

    def run_branch(qa_ref, k_ref, vt_ref, lo, window, gate_row, need=None):
        def k_of(kj):
            off = pl.multiple_of(kj * tk, tk)
            return jnp.concatenate([k_ref[pl.ds(off, tk), :], kx_ref[pl.ds(off, tk), :]], axis=1)

        def mask_of(kj):
            dist = (qi - kj) * tq + qcol - krow
            return (dist >= 0) & (dist < WINDOW) if window else dist >= 0

        _heads_run(k_of, vt_ref, qa_ref, s_ref, m_ref, acc_ref, lo, qi, mask_of, window, need)
        for h in range(NSA_HEADS):
            r = gate_row + h
            out_ref[h] = out_ref[h] + smt[r:r + 1, :] * _finish_t(acc_ref[h])

    run_branch(qas_ref, ks_ref, vst_ref, 0, False, FOX_HEADS + NSA_HEADS, need_ref)
    run_branch(qaw_ref, kw_ref, vwt_ref, jnp.maximum(qi - (WINDOW + tk - 1) // tk, 0), True,
               FOX_HEADS + 2 * NSA_HEADS)
    for j in range(HPG):
        pair_t = jnp.concatenate([out_ref[j], out_ref[HPG + j]], axis=0)
        o_ref[:, j * LANES:(j + 1) * LANES] = pair_t.T.astype(BF16)


def _nsa_prompt_call(qn3, kv3, kx, vst, vwt, cmp3, cx, small3, tq, tk):
    b, l, _ = qn3.shape
    nb = cmp3.shape[1]
    assert nb <= CMP_BLOCK, "one-hot block lanes cover at most CMP_BLOCK blocks"

    def kspec(c):
        return pl.BlockSpec((None, l, LANES), lambda i, j: (i, 0, c))

    vspec = pl.BlockSpec((None, l // tk, NSA_KV_HEADS, VT_ROWS, tk), lambda i, j: (i, 0, 0, 0, 0))
    return pl.pallas_call(
        functools.partial(_nsa_prompt_kernel, tq=tq, tk=tk, nb=nb),
        grid=(b, l // tq),
        in_specs=[pl.BlockSpec((None, tq, 512), lambda i, j: (i, j, 0)),
                  kspec(0), kspec(1), pl.BlockSpec((l, LANES), lambda i, j: (0, 0)), vspec, vspec,
                  pl.BlockSpec((None, nb, 256), lambda i, j: (i, 0, 0)),
                  pl.BlockSpec((nb, LANES), lambda i, j: (0, 0)),
                  pl.BlockSpec((None, tq, LANES), lambda i, j: (i, j, 0))],
        out_specs=pl.BlockSpec((None, tq, 512), lambda i, j: (i, j, 0)),
        out_shape=jax.ShapeDtypeStruct((b, l, 512), BF16),
        scratch_shapes=_attn_scratch(tq, tk) + [pltpu.SMEM((1,), jnp.int32)],
        compiler_params=_params(("parallel", "arbitrary")),
        name="nsa_prompt",
    )(qn3, kv3, kv3, kx, vst, vwt, cmp3, cx, small3)


def _fox_prompt_kernel(q_ref, qx_ref, k_ref, vt_ref, o_ref, qa_ref, m_ref, s_ref, acc_ref, *, tq, tk):
    assert tq == tk, "the pipelined key loop walks one key tile per query tile"
    qi = pl.program_id(1)
    lane = lax.broadcasted_iota(jnp.int32, (tq, LANES), 1)
    krow = lax.broadcasted_iota(jnp.int32, (tk, tq), 0)
    qcol = lax.broadcasted_iota(jnp.int32, (tk, tq), 1)
    qx = qx_ref[...]
    for g in range(FOX_KV_HEADS):
        for j in range(HPG):
            h = g * HPG + j
            own = ((lane >= FQ0 + 3 * h) & (lane < FQ0 + 3 * h + 3)) | ((lane >= FK0 + 3 * h) & (lane < FK0 + 3 * h + 3))
            qa_ref[h, :, :LANES] = _group_masked(q_ref[:, j * LANES:(j + 1) * LANES], g)
            qa_ref[h, :, LANES:] = jnp.where(own, qx, jnp.zeros_like(qx))
    _heads_run(lambda kj: k_ref[pl.ds(pl.multiple_of(kj * tk, tk), tk), :], vt_ref, qa_ref, s_ref, m_ref, acc_ref,
               0, qi, lambda kj: (qi - kj) * tq + qcol - krow >= 0, False)
    for j in range(HPG):
        pair_t = jnp.concatenate([_finish_t(acc_ref[j]), _finish_t(acc_ref[HPG + j])], axis=0)
        o_ref[:, j * LANES:(j + 1) * LANES] = pair_t.T.astype(BF16)


def _fox_prompt_call(qf3, qx3, kaug3, vft, tq, tk):
    b, l, _ = qf3.shape
    return pl.pallas_call(
        functools.partial(_fox_prompt_kernel, tq=tq, tk=tk),
        grid=(b, l // tq),
        in_specs=[pl.BlockSpec((None, tq, 512), lambda i, j: (i, j, 0)),
                  pl.BlockSpec((None, tq, LANES), lambda i, j: (i, j, 0)),
                  pl.BlockSpec((None, l, 2 * LANES), lambda i, j: (i, 0, 0)),
                  pl.BlockSpec((None, l // tk, FOX_KV_HEADS, VT_ROWS, tk), lambda i, j: (i, 0, 0, 0, 0))],
        out_specs=pl.BlockSpec((None, tq, 512), lambda i, j: (i, j, 0)),
        out_shape=jax.ShapeDtypeStruct((b, l, 512), BF16),
        scratch_shapes=_attn_scratch(tq, tk)[1:5],
        compiler_params=_params(("parallel", "arbitrary")),
        name="fox_prompt",
    )(qf3, qx3, kaug3, vft)


def _lane_cumsum(x):
    n = x.shape[1]
    r = lax.broadcasted_iota(jnp.int32, (n, n), 0)
    c = lax.broadcasted_iota(jnp.int32, (n, n), 1)
    upper = jnp.where(r <= c, 1.0, 0.0).astype(BF16)
    return sum(_nn(p, upper) for p in _split3(x))


def _dec_cmp_kernel(pt_ref, q_ref, rowc_ref, wct_ref, bsum_ref, *rest, pp, nb, past_len):
    del pt_ref
    pages = rest[:pp]
    lfs = rest[pp:2 * pp]
    oc_ref, sel_ref, ftot_ref, fs_ref, cmp_scr, ftot_scr = rest[2 * pp:]
    c = pl.program_id(1)

    @pl.when(c == 0)
    def _():
        ftot_scr[...] = jnp.zeros_like(ftot_scr)

    nblk = bsum_ref.shape[0]
    bsum = bsum_ref[...]
    for plane in range(2):
        w = wct_ref[plane]
        for g in range(NSA_KV_HEADS):
            prod = jnp.concatenate([(p[plane, g] * w).astype(BF16) for p in pages], axis=1)
            blk = _nt(bsum, prod)
            cmp_scr[2 * plane + g, pl.ds(pl.multiple_of(c * nblk, nblk), nblk), :] = blk
    cs = _lane_cumsum(jnp.concatenate([p[...] for p in lfs], axis=0))
    page = cs.shape[1]
    car = ftot_scr[:, 0:1]
    for i in range(pp):
        cs_i = cs[i * FOX_HEADS:(i + 1) * FOX_HEADS, :]
        fs_ref[i] = car + cs_i
        car = car + cs_i[:, page - 1:page]
    ftot_scr[...] = jnp.broadcast_to(car, ftot_scr.shape)

    @pl.when(c == pl.num_programs(1) - 1)
    def _():
        ftot_ref[...] = ftot_scr[...]
        ckb = jnp.concatenate([cmp_scr[0], cmp_scr[1]], axis=1).astype(BF16)
        cvb = jnp.concatenate([cmp_scr[2], cmp_scr[3]], axis=1).astype(BF16)
        nrow = q_ref.shape[0]
        rowc = rowc_ref[...]
        slope = rowc[:, 0:1]
        t = past_len + rowc[:, 1:2]
        nidx = lax.broadcasted_iota(jnp.int32, (nrow, nb), 1)
        cmid = nidx.astype(F32) * CMP_BLOCK + 0.5 * (CMP_BLOCK - 1)
        valid = cmid + 0.5 * (CMP_BLOCK - 1) <= t
        s = jnp.where(valid, _nt(q_ref[...], ckb) - slope * (t - cmid), NEG_INF)
        e = jnp.exp(s - jnp.max(s, axis=-1, keepdims=True))
        p = jnp.where(valid, e / jnp.sum(e, axis=-1, keepdims=True), 0.0)
        oc_ref[...] = _nn(p.astype(BF16), cvb)
        nq = nrow // NSA_HEADS
        n8 = lax.broadcasted_iota(jnp.int32, (nq, nb), 1)
        forced = (n8 == 0) | (n8 == nb - 1)
        sel_rows = []
        for g in range(NSA_KV_HEADS):
            imp = jnp.zeros((nq, nb), F32)
            for j in range(HPG):
                r0 = (g * HPG + j) * nq
                imp = imp + p[r0:r0 + nq, :]
            sel = _top_blocks(imp + jnp.where(forced, FORCE_BONUS, 0.0), SLC_TOPK - 1)
            sel_rows += [sel] * HPG
        sel_ref[...] = ((jnp.concatenate(sel_rows, axis=0) - 1.0) * MASK_BIG).astype(BF16)


def _page_specs(pp, n_pages, block, plane_pair):
    tail = (0,) * (len(block) - 2)

    def one(i):
        return pl.BlockSpec(block, lambda b, c, pt: (pt[b * n_pages + c * pp + i], plane_pair) + tail)
    return [one(i) for i in range(pp)]


def _dec_cmp_call(pt_flat, q_dec, rowc, wct, bsum, nsa_t, lf_pool, pp, n_pages, past_len):
    b, nrow, _ = q_dec.shape
    nb = past_len // CMP_BLOCK
    page = nsa_t.shape[-1]
    nc = n_pages // pp
    assert nb <= LANES, "block mask lanes cover at most 128 cached blocks"
    per = lambda shape: pl.BlockSpec((None,) + shape, lambda i, c, pt: (i, 0, 0))
    cst = lambda shape: pl.BlockSpec(shape, lambda i, c, pt: (0,) * len(shape))
    grid_spec = pltpu.PrefetchScalarGridSpec(
        num_scalar_prefetch=1,
        grid=(b, nc),
        in_specs=[per((nrow, LANES)), cst(rowc.shape), cst(wct.shape), cst(bsum.shape)]
        + _page_specs(pp, n_pages, (None, 2, NSA_KV_HEADS, HEAD_DIM, page), 0)
        + _page_specs(pp, n_pages, (None, FOX_HEADS, page), 0),
        out_specs=[per((nrow, LANES)), per((nrow, nb)), per((FOX_HEADS, LANES)),
                   pl.BlockSpec((None, pp, FOX_HEADS, page), lambda i, c, pt: (i, c, 0, 0))],
        scratch_shapes=[pltpu.VMEM((4, nb, HEAD_DIM), F32), pltpu.VMEM((FOX_HEADS, LANES), F32)],
    )
    return pl.pallas_call(
        functools.partial(_dec_cmp_kernel, pp=pp, nb=nb, past_len=past_len),
        grid_spec=grid_spec,
        out_shape=[jax.ShapeDtypeStruct((b, nrow, LANES), F32),
                   jax.ShapeDtypeStruct((b, nrow, nb), BF16),
                   jax.ShapeDtypeStruct((b, FOX_HEADS, LANES), F32),
                   jax.ShapeDtypeStruct((b, n_pages, FOX_HEADS, page), F32)],
        compiler_params=_params(("parallel", "arbitrary")),
        name="decode_cmp",
    )(pt_flat, q_dec, rowc, wct, bsum, *([nsa_t] * pp), *([lf_pool] * pp))


def _rep_rows(x, n):
    return jnp.concatenate([jnp.broadcast_to(x[h:h + 1, :], (n, x.shape[1])) for h in range(x.shape[0])], axis=0)


def _scr_flash_t(scores, vts, stats):
    staged = []
    for s, (m_ref, l_ref, _) in zip(scores, stats):
        m_old = m_ref[:, 0:1]
        m = jnp.maximum(m_old, jnp.max(s, axis=-1, keepdims=True))
        alpha = jnp.exp(m_old - m)
        p = jnp.exp(s - m)
        l = alpha * l_ref[:, 0:1] + jnp.sum(p, axis=-1, keepdims=True)
        m_ref[...] = jnp.broadcast_to(m, m_ref.shape)
        l_ref[...] = jnp.broadcast_to(l, l_ref.shape)
        staged.append((alpha, p.astype(BF16)))
    for (alpha, p), vt, (_, _, acc_ref) in zip(staged, vts, stats):
        acc_ref[...] = alpha * acc_ref[...] + _nt(p, vt)


def _pages_t(pages, plane):
    return jnp.concatenate(
        [jnp.concatenate([p[plane, g] for g in range(NSA_KV_HEADS)], axis=0) for p in pages], axis=1).astype(BF16)


def _dec_main_kernel(pt_ref, qn_ref, qf_ref, sel_ref, rowc_ref, oc_ref, gate_ref, ftot_ref,
                     newn_ref, newf_ref, newlf_ref, wbuf_ref, wnew_ref, hot_ref, fs_ref, *rest, pp, nb, past_len, nq):
    del pt_ref
    slc = rest[:pp]
    fox = rest[pp:2 * pp]
    (on_ref, of_ref, wst_ref, ms_ref, ls_ref, as_ref, mf_ref, lf_ref, af_ref, ftc_ref, qna_ref) = rest[2 * pp:]
    c = pl.program_id(1)
    nrow = qn_ref.shape[0]
    page = fs_ref.shape[2]
    kc = pp * page
    rowc = rowc_ref[...]
    slope = rowc[:, 0:1]
    t = past_len + rowc[:, 1:2]
    qn = qn_ref[...]
    qf = qf_ref[...]

    @pl.when(c == 0)
    def _():
        for r in (ms_ref, mf_ref):
            r[...] = jnp.full(r.shape, NEG_INF, F32)
        for r in (ls_ref, as_ref, lf_ref, af_ref):
            r[...] = jnp.zeros_like(r)
        f_new = ftot_ref[...] + _lane_cumsum(newlf_ref[...])
        rep = _rep_rows(f_new, nq)
        lane = lax.broadcasted_iota(jnp.int32, rep.shape, 1)
        tok = lax.broadcasted_iota(jnp.int32, rep.shape, 0) & (nq - 1)
        ftc_ref[...] = jnp.broadcast_to(
            jnp.sum(jnp.where(lane == tok, rep, 0.0), axis=-1, keepdims=True), ftc_ref.shape)
        qna_ref[:, :LANES] = qn
        qna_ref[:, LANES:] = sel_ref[...]

    ft = ftc_ref[:, 0:1]

    s_fox = _nn(qf, _pages_t(fox, 0))
    fs = jnp.concatenate([_rep_rows(fs_ref[i], nq) for i in range(pp)], axis=1)
    s_fox = s_fox + (ft - fs)

    kts = jnp.concatenate([_pages_t(slc, 0), hot_ref[...]], axis=0)
    pos = (c * kc + lax.broadcasted_iota(jnp.int32, (1, kc), 1)).astype(F32)
    s_slc = _nn(qna_ref[...], kts) - slope * (t - pos)
    _scr_flash_t([s_slc, s_fox], [_pages_t(slc, 1), _pages_t(fox, 1)],
                 [(ms_ref, ls_ref, as_ref), (mf_ref, lf_ref, af_ref)])

    @pl.when(c == pl.num_programs(1) - 1)
    def _():
        npad = newn_ref.shape[0]
        lane_i = lax.broadcasted_iota(jnp.int32, (1, npad), 1)
        posn = (past_len + lane_i).astype(F32)
        okn = (lane_i < nq) & (posn <= t)
        newn = newn_ref[...]
        sn = jnp.where(okn, _nt(qn, newn[:, :LANES].astype(BF16)) - slope * (t - posn), NEG_INF)
        m, l, acc = _flash_step(sn, newn[:, LANES:].astype(BF16), ms_ref[:, 0:1], ls_ref[:, 0:1], as_ref[...])
        o_s = acc / l
        newf = newf_ref[...]
        fsn = _rep_rows(ftot_ref[...] + _lane_cumsum(newlf_ref[...]), nq)
        sf = jnp.where(okn, _nt(qf, newf[:, :LANES].astype(BF16)) + (ft - fsn), NEG_INF)
        m, l, acc = _flash_step(sf, newf[:, LANES:].astype(BF16), mf_ref[:, 0:1], lf_ref[:, 0:1], af_ref[...])
        of_ref[...] = acc / l
        wbuf = wbuf_ref[...]
        wnew = wnew_ref[...]
        nbuf = wbuf.shape[0]
        kw = jnp.concatenate([wbuf[:, :LANES], wnew[:, :LANES]], axis=0).astype(BF16)
        vw = jnp.concatenate([wbuf[:, LANES:], wnew[:, LANES:]], axis=0).astype(BF16)
        wl = lax.broadcasted_iota(jnp.int32, (1, nbuf + npad), 1)
        posw = jnp.where(wl < nbuf, past_len - nbuf + wl, past_len + wl - nbuf).astype(F32)
        dw = t - posw
        okw = (dw >= 0) & (dw < WINDOW) & ((wl < nbuf) | (wl < nbuf + nq))
        sw = jnp.where(okw, _nt(qn, kw) - slope * dw, NEG_INF)
        ew = jnp.exp(sw - jnp.max(sw, axis=-1, keepdims=True))
        o_w = _nn(ew.astype(BF16), vw) / jnp.sum(ew, axis=-1, keepdims=True)
        gate = gate_ref[...]
        on_ref[...] = gate[:, 0:1] * oc_ref[...] + gate[:, 1:2] * o_s + gate[:, 2:3] * o_w
        wst_ref[0:nbuf - nq, :] = wbuf[nq:, :]
        wst_ref[nbuf - nq:, :] = wnew[0:nq, :]


def _dec_main_call(pt_flat, qn_dec, qf_dec, sel, rowc, o_c, gates, ftot, newn, newf, newlf, wbuf, wnew,
                   hot, fs_past, nsa_t, fox_t, pp, n_pages, past_len, nq):
    b, nrow, _ = qn_dec.shape
    nb = past_len // CMP_BLOCK
    page = nsa_t.shape[-1]
    nc = n_pages // pp
    nbuf = wbuf.shape[1]
    assert nb % 16 == 0 and nb <= LANES, "query-side block mask lanes"
    per = lambda shape: pl.BlockSpec((None,) + shape, lambda i, c, *_: (i, 0, 0))
    cst = lambda shape: pl.BlockSpec(shape, lambda i, c, *_: (0, 0))
    rl = (nrow, LANES)
    kv_block = (None, 2, NSA_KV_HEADS, HEAD_DIM, page)
    grid_spec = pltpu.PrefetchScalarGridSpec(
        num_scalar_prefetch=1,
        grid=(b, nc),
        in_specs=[per(rl), per(rl), per((nrow, nb)), cst(rowc.shape), per(rl), per(rl), per((FOX_HEADS, LANES)),
                  per(newn.shape[1:]), per(newf.shape[1:]), per(newlf.shape[1:]),
                  per((nbuf, 2 * LANES)), per(wnew.shape[1:]),
                  pl.BlockSpec((None,) + hot.shape[1:], lambda i, c, *_: (c, 0, 0)),
                  pl.BlockSpec((None, pp, FOX_HEADS, page), lambda i, c, *_: (i, c, 0, 0))]
        + _page_specs(pp, n_pages, kv_block, 1)
        + _page_specs(pp, n_pages, kv_block, 0),
        out_specs=[per(rl), per(rl), per((nbuf, 2 * LANES))],
        scratch_shapes=[pltpu.VMEM(rl, F32)] * 6 + [pltpu.VMEM(rl, F32), pltpu.VMEM((nrow, LANES + nb), BF16)],
    )
    return pl.pallas_call(
        functools.partial(_dec_main_kernel, pp=pp, nb=nb, past_len=past_len, nq=nq),
        grid_spec=grid_spec,
        out_shape=[jax.ShapeDtypeStruct((b,) + rl, F32), jax.ShapeDtypeStruct((b,) + rl, F32),
                   jax.ShapeDtypeStruct((b, nbuf, 2 * LANES), F32)],
        compiler_params=_params(("parallel", "arbitrary")),
        name="decode_main",
    )(pt_flat, qn_dec, qf_dec, sel, rowc, o_c, gates, ftot, newn, newf, newlf, wbuf, wnew, hot, fs_past,
      *([nsa_t] * pp), *([fox_t] * pp))


def _outproj_kernel(x_ref, mod_ref, g_ref, on_ref, of_ref, wm_ref, wa_ref, wb_ref, wo_ref, o_ref):
    x = x_ref[...]
    nb, nt, d = x.shape
    h = _pre(x, mod_ref, g_ref, 1).reshape(nb * nt, d).astype(BF16)
    gm = jax.nn.sigmoid(_nn(h, wm_ref[...]))
    y = gm[:, :d] * _nn(on_ref[...], wa_ref[...]) + gm[:, d:] * _nn(of_ref[...], wb_ref[...])
    mix = _nn(y.astype(BF16), wo_ref[...]).reshape(nb, nt, d)
    o_ref[...] = x + mod_ref[:, 5:6, :] * mix


def _outproj_call(x3, mod, g_norm, o_n, o_f, wm, wa, wb, wo, blk):
    bs, lt, d = x3.shape
    nb, nt = blk
    rows = nb * nt
    nj = lt // nt
    tok = pl.BlockSpec((rows, o_n.shape[1]), lambda i, j: (i * nj + j, 0))
    return pl.pallas_call(
        _outproj_kernel,
        grid=(bs // nb, nj),
        in_specs=[pl.BlockSpec((nb, nt, d), lambda i, j: (i, j, 0)),
                  pl.BlockSpec((nb, 3 * N_SUB, d), lambda i, j: (i, 0, 0)),
                  _const_spec(g_norm.shape), tok, tok,
                  _const_spec(wm.shape), _const_spec(wa.shape), _const_spec(wb.shape), _const_spec(wo.shape)],
        out_specs=pl.BlockSpec((nb, nt, d), lambda i, j: (i, j, 0)),
        out_shape=jax.ShapeDtypeStruct(x3.shape, F32),
        compiler_params=_params(("parallel", "parallel")),
        name="out_proj",
    )(x3, mod, g_norm, o_n, o_f, wm, wa, wb, wo)


def _pair_major_cols(w):
    d = w.shape[0]
    return w.reshape(d, NSA_HEADS, HEAD_DIM)[:, jnp.array(PAIR_PERM), :].reshape(d, NSA_HEADS * HEAD_DIM)


def _pair_major_rows(w):
    d = w.shape[1]
    return w.reshape(NSA_HEADS, HEAD_DIM, d)[jnp.array(PAIR_PERM)].reshape(NSA_HEADS * HEAD_DIM, d)


def _prep_weights(w_in, b_fgt, g_q_nsa, g_k_nsa, g_q_fox, g_k_fox, w_cmp):
    d = w_in.shape[0]
    hq = NSA_HEADS * HEAD_DIM
    o = 0
    w_qn = w_in[:, o:o + hq]; o += hq
    w_kvn = w_in[:, o:o + 3 * 2 * LANES]; o += 3 * 2 * LANES
    w_gt = w_in[:, o:o + 3 * NSA_HEADS]; o += 3 * NSA_HEADS
    w_qf = w_in[:, o:o + hq]; o += hq
    w_kvf = w_in[:, o:o + 2 * LANES]; o += 2 * LANES
    w_ff = w_in[:, o:o + FOX_HEADS]; o += FOX_HEADS
    w_merge = w_in[:, o:]
    k_n = [w_kvn[:, br * 256:br * 256 + LANES] for br in range(3)]
    v_n = [w_kvn[:, br * 256 + LANES:(br + 1) * 256] for br in range(3)]
    small = jnp.concatenate([w_ff, w_gt, jnp.zeros((d, LANES - FOX_HEADS - 3 * NSA_HEADS), w_in.dtype)], axis=1)
    w1 = jnp.concatenate([_pair_major_cols(w_qn), _pair_major_cols(w_qf), k_n[0], k_n[1], k_n[2], w_kvf[:, :LANES],
                          v_n[0], v_n[1], v_n[2], w_kvf[:, LANES:], small], axis=1).astype(BF16)
    scale = HEAD_DIM ** -0.5
    gains = jnp.concatenate([jnp.tile(g_q_nsa * scale, NSA_HEADS), jnp.tile(g_q_fox * scale, FOX_HEADS),
                             jnp.tile(g_k_nsa[0], 2), jnp.tile(g_k_nsa[1], 2), jnp.tile(g_k_nsa[2], 2),
                             jnp.tile(g_k_fox, 2)]).reshape(1, N_NORMED * LANES)
    bf_pad = jnp.concatenate([b_fgt, jnp.zeros((LANES - FOX_HEADS,), F32)]).reshape(1, LANES)
    lane = jnp.arange(LANES)
    seg = (lane[:, None] // HEAD_DIM == lane[None, :] // HEAD_DIM).astype(BF16)
    wc = jnp.concatenate([w_cmp[0], w_cmp[0], w_cmp[1], w_cmp[1]], axis=1)
    return w1, gains, bf_pad, seg, wc, w_merge.astype(BF16)


def _position_lanes(l):
    pos = jnp.arange(l)
    lane = jnp.arange(LANES)[None, :]
    blk = (pos // CMP_BLOCK)[:, None]
    kx = jnp.where(lane < CMP_BLOCK, (lane == blk).astype(F32),
                   jnp.where((lane == XL_ONE_A) | (lane == XL_ONE_B), 1.0,
                             jnp.where(lane == XL_HI, blk.astype(F32),
                                       jnp.where(lane == XL_LO, (pos % CMP_BLOCK)[:, None].astype(F32), 0.0))))
    n = jnp.arange(LANES)[:, None]
    cx = jnp.where((lane == XL_ONE_A) | (lane == XL_ONE_B), 1.0,
                   jnp.where(lane == XL_HI, n.astype(F32), jnp.where(lane == XL_LO, 0.5 * (CMP_BLOCK - 1), 0.0)))
    return kx.astype(BF16), cx.astype(BF16)


def _to_decode_rows(q, b, nq):
    q5 = q.reshape(b, nq, HPG, 2, HEAD_DIM)
    eye = jnp.eye(2, dtype=q.dtype)
    return jnp.einsum('btjgd,gk->bgjtkd', q5, eye).reshape(b, NSA_HEADS * nq, LANES)


def _from_decode_rows(o, b, nq):
    o6 = o.reshape(b, 2, HPG, nq, 2, HEAD_DIM)
    own = jnp.stack([o6[:, 0, :, :, 0], o6[:, 1, :, :, 1]], axis=1)
    return own.transpose(0, 3, 1, 2, 4).reshape(b * nq, NSA_HEADS * HEAD_DIM)


def _pad_rows(x, n):
    return jnp.pad(x, ((0, 0), (0, n - x.shape[1]), (0, 0)))


def kernel(x_prompt, x_sample, cache_nsa_kv, cache_fox_kv, cache_fox_logf, state_win_kv, page_table,
           c_prompt, c_sample, w_ada, b_ada, g_norm, w_ffn_up, w_ffn_down, w_in, b_fgt,
           g_q_nsa, g_k_nsa, g_q_fox, g_k_fox, w_cmp, w_br_nsa, w_br_fox, w_out):
    assert w_ada.shape[0] == 1, "single-layer trunk"
    bp, lp, d = x_prompt.shape
    bs, nq, _ = x_sample.shape
    n_pool, page = cache_nsa_kv.shape[1:3]
    n_pages = page_table.shape[1]
    past_len = n_pages * page
    nbuf = state_win_kv.shape[2]
    dff = w_ffn_down.shape[2]
    assert nq % 8 == 0 and nq & (nq - 1) == 0 and nq <= CMP_BLOCK and past_len // CMP_BLOCK >= SLC_TOPK
    assert nbuf == WINDOW and lp % 512 == 0

    mod = _mod_call(jnp.concatenate([c_prompt, c_sample], axis=0), w_ada[0], b_ada[0]).reshape(bp + bs, 3 * N_SUB, d)
    mod_p, mod_s = mod[:bp], mod[bp:]
    gn = g_norm[0]
    up = w_ffn_up[0].astype(BF16)
    ffn_w = [(up[i, :, :dff], up[i, :, dff:], w_ffn_down[0, i].astype(BF16)) for i in range(2)]
    w1, gains, bf_pad, seg, wc, w_merge = _prep_weights(
        w_in[0], b_fgt[0], g_q_nsa[0], g_k_nsa[0], g_q_fox[0], g_k_fox[0], w_cmp[0])
    wo = w_out[0].astype(BF16)
    wa_nat, wb_nat = w_br_nsa[0].astype(BF16), w_br_fox[0].astype(BF16)
    wa_pm, wb_pm = _pair_major_rows(wa_nat), _pair_major_rows(wb_nat)

    pblk = (1, 512)
    tq, tk = 256, 256
    x1 = _ffn_call(x_prompt, mod_p, gn, *ffn_w[0], 0, pblk)
    qn, qf, nsa_new, win_new, fox_new, small, kvb, cmpb, vt_slc, vt_win, vt_fox = _inproj_call(
        x1, mod_p, gn, w1, gains, bf_pad, seg, wc, pblk, True, tk)
    small3 = small.reshape(bp, lp, LANES)
    logf_p = small3[:, :, :FOX_HEADS]
    kv3 = kvb.reshape(bp, lp, 3 * LANES)
    kaug_f, qx_f = _cumsum_call(small3, kv3, 512)
    kx, cx = _position_lanes(lp)
    o_n = _nsa_prompt_call(qn.reshape(bp, lp, 512), kv3, kx, vt_slc, vt_win,
                           cmpb.reshape(bp, lp // CMP_BLOCK, 256), cx[:lp // CMP_BLOCK], small3, tq, tk)
    o_f = _fox_prompt_call(qf.reshape(bp, lp, 512), qx_f, kaug_f, vt_fox, tq, tk)
    x2 = _outproj_call(x1, mod_p, gn, o_n.reshape(bp * lp, 512), o_f.reshape(bp * lp, 512),
                       w_merge, wa_pm, wb_pm, wo, pblk)
    y_prompt = _ffn_call(x2, mod_p, gn, *ffn_w[1], 2, pblk)
    def rows_major(t, planes, rows):
        return t.reshape(bp, planes, NSA_KV_HEADS, HEAD_DIM, rows).transpose(0, 4, 1, 2, 3)[None]

    nwin = min(WINDOW, lp)
    nsa_p = rows_major(nsa_new, 4, lp)
    fox_p = rows_major(fox_new, 2, lp)
    win_p = rows_major(win_new[:, :, lp - nwin:], 2, nwin)

    sblk = (min(bs, 64), nq)
    pp = next(p for p in (32, 16, 8, 4, 2, 1) if n_pages % p == 0)
    x1s = _ffn_call(x_sample, mod_s, gn, *ffn_w[0], 0, sblk)
    qn_s, qf_s, nsa_s, win_s, fox_s, small_s, _ = _inproj_call(
        x1s, mod_s, gn, w1, gains, bf_pad, seg, wc, sblk, False)
    nrow = NSA_HEADS * nq
    qn_dec = _to_decode_rows(qn_s, bs, nq)
    qf_dec = _to_decode_rows(qf_s, bs, nq)
    rows = jnp.arange(nrow)
    rowc = jnp.zeros((nrow, LANES), F32).at[:, 0].set(jnp.array(SLOPES, F32)[rows // nq]).at[:, 1].set(
        (rows % nq).astype(F32))
    sm3 = small_s.reshape(bs, nq, LANES)
    gates = sm3[:, :, FOX_HEADS:FOX_HEADS + 3 * NSA_HEADS].reshape(bs, nq, 3, NSA_HEADS)
    gates = jnp.pad(gates.transpose(0, 3, 1, 2).reshape(bs, nrow, 3), ((0, 0), (0, 0), (0, LANES - 3)))
    newlf = _pad_rows(sm3[:, :, :FOX_HEADS], LANES).transpose(0, 2, 1)
    newn = _pad_rows(nsa_s.reshape(bs, nq, 512)[:, :, 256:], LANES)
    newf = _pad_rows(fox_s.reshape(bs, nq, 256), LANES)
    wnew = _pad_rows(win_s.reshape(bs, nq, 256), LANES)
    nsa_t = cache_nsa_kv[0].transpose(0, 2, 3, 4, 1)
    fox_t = cache_fox_kv[0].transpose(0, 2, 3, 4, 1)
    wct = jnp.tile(w_cmp[0].transpose(0, 2, 1), (1, 1, page // CMP_BLOCK))
    kc = pp * page
    col_blk = jnp.arange(kc) // CMP_BLOCK
    bsum = (jnp.arange(kc // CMP_BLOCK)[:, None] == col_blk[None, :]).astype(BF16)
    nblocks = past_len // CMP_BLOCK
    hot = (jnp.arange(nblocks)[None, :, None]
           == (jnp.arange(n_pages // pp)[:, None, None] * (kc // CMP_BLOCK) + col_blk[None, None, :])).astype(BF16)
    lf_pool = cache_fox_logf[0].astype(F32).transpose(0, 2, 1)
    wbuf = state_win_kv[0].reshape(bs, nbuf, 256)
    pt_flat = page_table.reshape(-1).astype(jnp.int32)
    o_c, sel, ftot, fs_past = _dec_cmp_call(pt_flat, qn_dec, rowc, wct, bsum, nsa_t, lf_pool, pp, n_pages, past_len)
    on_dec, of_dec, wst = _dec_main_call(pt_flat, qn_dec, qf_dec, sel, rowc, o_c, gates, ftot, newn, newf,
                                         newlf, wbuf, wnew, hot, fs_past, nsa_t, fox_t, pp, n_pages, past_len, nq)
    o_ns = _from_decode_rows(on_dec, bs, nq).astype(BF16)
    o_fs = _from_decode_rows(of_dec, bs, nq).astype(BF16)
    x2s = _outproj_call(x1s, mod_s, gn, o_ns, o_fs, w_merge, wa_nat, wb_nat, wo, sblk)
    y_sample = _ffn_call(x2s, mod_s, gn, *ffn_w[1], 2, sblk)
    nsa_o = nsa_s.reshape(1, bs, nq, 4, NSA_KV_HEADS, HEAD_DIM)
    fox_o = fox_s.reshape(1, bs, nq, 2, FOX_KV_HEADS, HEAD_DIM)
    win_o = wst.reshape(1, bs, nbuf, 2, NSA_KV_HEADS, HEAD_DIM)

    return (y_prompt, y_sample, nsa_p, nsa_o, fox_p, fox_o,
            logf_p[None], sm3[None, :, :, :FOX_HEADS], win_p, win_o)
```
